```python
import jax
import jax.numpy as jnp
from jax import lax
import numpy as np


D_MODEL = 4096
BATCH = 1
SEQ = 8192
DEPTH = 4

CHUNK = 64
N_MEM = 256
NORM_EPS = 1e-6
MIX_W = D_MODEL
A_W = MIX_W // 2
A_HEAD = 64
A_H = A_W // A_HEAD
A_LORA = 96
A_LN_EPS = 64e-5
B_W = MIX_W // 2
B_H = 8
B_HEAD = B_W // B_H
GATE_CAP = 15.0
C_VW = MIX_W // 2
C_H = 8
C_VHEAD = C_VW // C_H
C_KHEAD = C_VHEAD // 2
C_KW = C_H * C_KHEAD
ROPE_BASE = 10000.0
D_W = MIX_W // 2
D_BLOCKS = 16
D_BW = D_W // D_BLOCKS
CONV_W = 4
LRU_C = 8.0
D_FF = 4 * D_MODEL
XA_H = 4
XA_HEAD = 256
XA_W = XA_H * XA_HEAD
A_COLS = 4 * A_W + 2 * A_LORA
B_COLS = 4 * B_W + 2 * B_H
EVEN_COLS = A_COLS + B_COLS
ODD_COLS = 2 * C_KW + 2 * C_VW + 2 * D_W
N_EVEN = (DEPTH + 1) // 2
N_ODD = DEPTH // 2

kernel_name = 'hybrid_chunk_causal_encoder'


def rmsnorm(x, g):
    xf = x.astype(jnp.float32)
    y = xf * lax.rsqrt(jnp.mean(xf * xf, axis=-1, keepdims=True) + NORM_EPS)
    return y.astype(x.dtype) * g


def head_rmsnorm(t, g):
    b, s, h, d = t.shape
    tf = t.astype(jnp.float32)
    y = tf * lax.rsqrt(jnp.mean(tf * tf, axis=-1, keepdims=True) + NORM_EPS)
    return y.reshape(b, s, h * d).astype(t.dtype) * g


def head_layernorm(t, w, bias, eps):
    b, s, h, d = t.shape
    tf = t.astype(jnp.float32)
    mu = jnp.mean(tf, axis=-1, keepdims=True)
    var = jnp.mean(jnp.square(tf - mu), axis=-1, keepdims=True)
    y = (tf - mu) * lax.rsqrt(var + eps)
    return y.reshape(b, s, h * d).astype(t.dtype) * w + bias


def to_chunks(t):
    b, s, h, d = t.shape
    return t.reshape(b, s // CHUNK, CHUNK, h, d).transpose(1, 0, 3, 2, 4)


def from_chunks(t):
    nc, b, h, l, d = t.shape
    return t.transpose(1, 0, 3, 2, 4).reshape(b, nc * l, h, d)


def gate_chunks(t):
    b, s, h = t.shape
    return t.reshape(b, s // CHUNK, CHUNK, h).transpose(1, 0, 3, 2)


def shift1(t):
    return jnp.pad(t, ((0, 0), (1, 0), (0, 0)))[:, :-1]


def softcap(t):
    return GATE_CAP * jnp.tanh(t / GATE_CAP)


def rope(t, cos, sin):
    t1, t2 = jnp.split(t, 2, axis=-1)
    c, sn = cos[None, :, None, :], sin[None, :, None, :]
    return jnp.concatenate([t1 * c - t2 * sn, t1 * sn + t2 * c], axis=-1).astype(t.dtype)


def rwkv7_mix(p, mu, w0, w_up, a0, a_up, k_k, k_a, r_k, ln_w, ln_b):
    b, s, _ = p.shape
    p = p + (shift1(p) - p) * mu
    r, k, v, g, w_lo, a_lo = jnp.split(
        p, [A_W, 2 * A_W, 3 * A_W, 4 * A_W, 4 * A_W + A_LORA], axis=-1)
    w = w0 + jnp.tanh(w_lo) @ w_up
    decay = jnp.exp(-jnp.exp(-jax.nn.softplus(-w) - 0.5))
    a = jax.nn.sigmoid(a0 + a_lo @ a_up)
    kk = k * k_k
    k = k * (1.0 + (a - 1.0) * k_a)
    heads = lambda t: t.reshape(b, s, A_H, A_HEAD)
    r, decay, k, v, kk, a = (heads(t) for t in (r, decay, k, v, kk, a))
    kk = kk * lax.rsqrt(jnp.sum(kk * kk, axis=-1, keepdims=True) + 1e-12)

    def step(state, inp):
        r_t, w_t, k_t, v_t, kk_t, a_t = inp
        sa = jnp.einsum('bhvk,bhk->bhv', state, -kk_t)
        state = (state * w_t[:, :, None, :] + sa[..., None] * (kk_t * a_t)[:, :, None, :]
                 + v_t[..., None] * k_t[:, :, None, :])
        return state, jnp.einsum('bhvk,bhk->bhv', state, r_t)

    xs = tuple(jnp.moveaxis(t, 1, 0) for t in (r, decay, k, v, kk, a))
    state0 = jnp.zeros((b, A_H, A_HEAD, A_HEAD), p.dtype)
    _, o = lax.scan(step, state0, xs)
    o = head_layernorm(jnp.moveaxis(o, 0, 1), ln_w, ln_b, A_LN_EPS)
    bonus = jnp.sum(r * k * r_k.reshape(A_H, A_HEAD), axis=-1, keepdims=True) * v
    return (o + bonus.reshape(b, s, A_W)) * jax.nn.sigmoid(g)


def mlstm_mix(p, i_b, f_b, norm_g):
    b, s, _ = p.shape
    q, k, v, o, i_pre, f_pre = jnp.split(
        p, [B_W, 2 * B_W, 3 * B_W, 4 * B_W, 4 * B_W + B_H], axis=-1)
    heads = lambda t: to_chunks(t.reshape(b, s, B_H, B_HEAD))
    q, k, v = heads(q), heads(k) * (B_HEAD ** -0.5), heads(v)
    log_i = gate_chunks(softcap(i_pre + i_b).astype(jnp.float32))
    log_f = gate_chunks(jax.nn.log_sigmoid(softcap(f_pre + f_b).astype(jnp.float32)))
    causal = jnp.tril(jnp.ones((CHUNK, CHUNK), dtype=bool))

    def step(carry, inp):
        c_st, n_st, m_st = carry
        q_c, k_c, v_c, li, lf = inp
        cum = jnp.cumsum(lf, axis=-1)
        d_log = jnp.where(causal, cum[..., :, None] - cum[..., None, :] + li[..., None, :], -jnp.inf)
        m_inter = cum + m_st[..., None]
        m_row = jnp.maximum(jnp.max(d_log, axis=-1), m_inter)
        s_qk = jnp.einsum('bhid,bhjd->bhij', q_c, k_c) * jnp.exp(d_log - m_row[..., None])
        w_inter = jnp.exp(m_inter - m_row)
        num = (jnp.einsum('bhij,bhjd->bhid', s_qk, v_c)
               + w_inter[..., None] * jnp.einsum('bhid,bhde->bhie', q_c, c_st))
        den = jnp.sum(s_qk, axis=-1) + w_inter * jnp.einsum('bhid,bhd->bhi', q_c, n_st)
        h = num / jnp.maximum(jnp.abs(den), jnp.exp(-m_row))[..., None]
        total = cum[..., -1]
        log_w = total[..., None] - cum + li
        m_new = jnp.maximum(total + m_st, jnp.max(log_w, axis=-1))
        w_k = jnp.exp(log_w - m_new[..., None])
        carry_decay = jnp.exp(total + m_st - m_new)
        c_st = carry_decay[..., None, None] * c_st + jnp.einsum('bhj,bhjd,bhje->bhde', w_k, k_c, v_c)
        n_st = carry_decay[..., None] * n_st + jnp.einsum('bhj,bhjd->bhd', w_k, k_c)
        return (c_st, n_st, m_new), h

    f32 = jnp.float32
    carry0 = (jnp.zeros((b, B_H, B_HEAD, B_HEAD), f32),
              jnp.zeros((b, B_H, B_HEAD), f32),
              jnp.zeros((b, B_H), f32))
    _, h = lax.scan(step, carry0, (q, k, v, log_i, log_f))
    h = head_rmsnorm(from_chunks(h), norm_g).astype(p.dtype)
    return h * jax.nn.sigmoid(o)


def retention_mix(q, k, v, g, cos, sin, norm_g):
    b, s, _ = q.shape
    q = to_chunks(rope(q.reshape(b, s, C_H, C_KHEAD), cos, sin))
    k = to_chunks(rope(k.reshape(b, s, C_H, C_KHEAD), cos, sin) * (C_KHEAD ** -0.5))
    v = to_chunks(v.reshape(b, s, C_H, C_VHEAD))
    log_gamma = jnp.log1p(-jnp.exp2(-5.0 - jnp.arange(C_H, dtype=jnp.float32)))
    pos = jnp.arange(CHUNK, dtype=jnp.float32)
    intra = jnp.exp(log_gamma[:, None, None] * jnp.abs(pos[:, None] - pos[None, :]))
    q_decay = jnp.exp(log_gamma[:, None] * (pos + 1.0))
    k_decay = jnp.exp(log_gamma[:, None] * (CHUNK - 1.0 - pos))
    chunk_decay = jnp.exp(log_gamma * CHUNK)

    def step(r_st, inp):
        q_c, k_c, v_c = inp
        s_qk = jnp.einsum('bhid,bhjd->bhij', q_c, k_c) * intra
        o = (jnp.einsum('bhij,bhje->bhie', s_qk, v_c)
             + q_decay[:, :, None] * jnp.einsum('bhid,bhde->bhie', q_c, r_st))
        r_st = chunk_decay[:, None, None] * r_st + jnp.einsum('bhjd,hj,bhje->bhde', k_c, k_decay, v_c)
        return r_st, o

    r0 = jnp.zeros((b, C_H, C_KHEAD, C_VHEAD), jnp.float32)
    _, o = lax.scan(step, r0, (q, k, v))
    o = head_rmsnorm(from_chunks(o), norm_g).astype(g.dtype)
    return o * jax.nn.silu(g)


def rglru_mix(xb, gate, conv_w, conv_b, w_r, b_r, w_i, b_i, lam):
    b, s, _ = xb.shape
    xp = jnp.pad(xb, ((0, 0), (CONV_W - 1, 0), (0, 0)))
    xc = conv_b + sum(xp[:, j:j + s] * conv_w[j] for j in range(CONV_W))
    xh = xc.reshape(b, s, D_BLOCKS, D_BW)
    r = jax.nn.sigmoid(jnp.einsum('bsni,nij->bsnj', xh, w_r).reshape(b, s, D_W) + b_r)
    i = jax.nn.sigmoid(jnp.einsum('bsni,nij->bsnj', xh, w_i).reshape(b, s, D_W) + b_i)
    log_a = -LRU_C * r * jax.nn.softplus(-lam)
    a = jnp.exp(log_a)
    u = jnp.sqrt(-jnp.expm1(2.0 * log_a)) * (i * xc)

    def combine(left, right):
        a_l, u_l = left
        a_r, u_r = right
        return a_l * a_r, a_r * u_l + u_r

    _, h = lax.associative_scan(combine, (a, u), axis=1)
    return h * jax.nn.gelu(gate)


def cross_attend(xn, mem_n, wq, wk, wv, wo):
    b, s, _ = xn.shape
    q = (xn @ wq).reshape(b, s, XA_H, XA_HEAD)
    k = (mem_n @ wk).reshape(b, -1, XA_H, XA_HEAD)
    v = (mem_n @ wv).reshape(b, -1, XA_H, XA_HEAD)
    sc = jnp.einsum('bshd,bmhd->bhsm', q, k).astype(jnp.float32) * (XA_HEAD ** -0.5)
    pr = jax.nn.softmax(sc, axis=-1).astype(v.dtype)
    o = jnp.einsum('bhsm,bmhd->bshd', pr, v).reshape(b, s, XA_W)
    return o @ wo


def sq_relu_mlp(xn, w1, w2):
    return jnp.square(jax.nn.relu(xn @ w1)) @ w2


def setup_inputs(seed: int = 0) -> dict:
    key = jax.random.key(seed)
    ks = iter(jax.random.split(key, 64))
    f32 = jnp.float32

    def nrm(shape, scale):
        return jax.random.normal(next(ks), shape, f32) * scale

    def gain(shape):
        return 1.0 + nrm(shape, 0.02)

    def unif(shape, lo, hi):
        return jax.random.uniform(next(ks), shape, f32, lo, hi)

    NE, NO = N_EVEN, N_ODD
    lru_base = unif((NO, D_W), 0.9, 0.999) ** (1.0 / LRU_C)
    lru_lambda = jnp.log(lru_base) - jnp.log1p(-lru_base)
    return {
        'x': nrm((BATCH, SEQ, D_MODEL), 1.0),
        'mem': nrm((BATCH, N_MEM, D_MODEL), 1.0),
        'mem_norm_g': gain((D_MODEL,)),
        'norm_mix_g': gain((DEPTH, D_MODEL)),
        'norm_xattn_g': gain((DEPTH, D_MODEL)),
        'norm_mlp_g': gain((DEPTH, D_MODEL)),
        'xattn_wq': nrm((DEPTH, D_MODEL, XA_W), D_MODEL ** -0.5),
        'xattn_wk': nrm((DEPTH, D_MODEL, XA_W), D_MODEL ** -0.5),
        'xattn_wv': nrm((DEPTH, D_MODEL, XA_W), D_MODEL ** -0.5),
        'xattn_wo': nrm((DEPTH, XA_W, D_MODEL), XA_W ** -0.5),
        'mlp_w1': nrm((DEPTH, D_MODEL, D_FF), D_MODEL ** -0.5),
        'mlp_w2': nrm((DEPTH, D_FF, D_MODEL), D_FF ** -0.5),
        'even_w_in': nrm((NE, D_MODEL, EVEN_COLS), D_MODEL ** -0.5),
        'even_w_out': nrm((NE, MIX_W, D_MODEL), MIX_W ** -0.5),
        'rwkv_mu': unif((NE, A_COLS), 0.0, 1.0),
        'rwkv_w0': unif((NE, A_W), -6.0, 1.0),
        'rwkv_w_up': nrm((NE, A_LORA, A_W), 0.1 * A_LORA ** -0.5),
        'rwkv_a0': nrm((NE, A_W), 0.5),
        'rwkv_a_up': nrm((NE, A_LORA, A_W), 0.5 * A_LORA ** -0.5),
        'rwkv_k_k': 0.85 + nrm((NE, A_W), 0.05),
        'rwkv_k_a': 1.0 + nrm((NE, A_W), 0.05),
        'rwkv_r_k': nrm((NE, A_W), 0.1),
        'rwkv_ln_w': gain((NE, A_W)),
        'rwkv_ln_b': nrm((NE, A_W), 0.01),
        'mlstm_i_b': nrm((NE, B_H), 0.5),
        'mlstm_f_b': jnp.broadcast_to(jnp.linspace(3.0, 6.0, B_H, dtype=f32), (NE, B_H)) + nrm((NE, B_H), 0.1),
        'mlstm_norm_g': gain((NE, B_W)),
        'odd_w_in': nrm((NO, D_MODEL, ODD_COLS), D_MODEL ** -0.5),
        'odd_w_out': nrm((NO, MIX_W, D_MODEL), MIX_W ** -0.5),
        'ret_norm_g': gain((NO, C_VW)),
        'lru_conv_w': nrm((NO, CONV_W, D_W), CONV_W ** -0.5),
        'lru_conv_b': nrm((NO, D_W), 0.01),
        'lru_w_r': nrm((NO, D_BLOCKS, D_BW, D_BW), D_BW ** -0.5),
        'lru_b_r': nrm((NO, D_W), 0.1),
        'lru_w_i': nrm((NO, D_BLOCKS, D_BW, D_BW), D_BW ** -0.5),
        'lru_b_i': nrm((NO, D_W), 0.1),
        'lru_lambda': lru_lambda,
        'final_norm_g': gain((D_MODEL,)),
    }


def reference(x, mem, mem_norm_g, norm_mix_g, norm_xattn_g, norm_mlp_g,
              xattn_wq, xattn_wk, xattn_wv, xattn_wo, mlp_w1, mlp_w2,
              even_w_in, even_w_out, rwkv_mu, rwkv_w0, rwkv_w_up, rwkv_a0, rwkv_a_up,
              rwkv_k_k, rwkv_k_a, rwkv_r_k, rwkv_ln_w, rwkv_ln_b,
              mlstm_i_b, mlstm_f_b, mlstm_norm_g,
              odd_w_in, odd_w_out, ret_norm_g, lru_conv_w, lru_conv_b,
              lru_w_r, lru_b_r, lru_w_i, lru_b_i, lru_lambda, final_norm_g):
    b, s, _ = x.shape
    mem_n = rmsnorm(mem, mem_norm_g)
    positions = jnp.arange(s, dtype=jnp.float32)
    inv_freq = ROPE_BASE ** (-jnp.arange(0, C_KHEAD, 2, dtype=jnp.float32) / C_KHEAD)
    ang = positions[:, None] * inv_freq[None, :]
    cos, sin = jnp.cos(ang), jnp.sin(ang)
    h = x
    for layer in range(DEPTH):
        j = layer // 2
        xn = rmsnorm(h, norm_mix_g[layer])
        if layer % 2 == 0:
            p = xn @ even_w_in[j]
            y_a = rwkv7_mix(p[..., :A_COLS], rwkv_mu[j], rwkv_w0[j], rwkv_w_up[j], rwkv_a0[j],
                            rwkv_a_up[j], rwkv_k_k[j], rwkv_k_a[j], rwkv_r_k[j],
                            rwkv_ln_w[j], rwkv_ln_b[j])
            y_b = mlstm_mix(p[..., A_COLS:], mlstm_i_b[j], mlstm_f_b[j], mlstm_norm_g[j])
            y = jnp.concatenate([y_a, y_b], axis=-1) @ even_w_out[j]
        else:
            p = xn @ odd_w_in[j]
            q, k, v, g, xb, gate = jnp.split(
                p, [C_KW, 2 * C_KW, 2 * C_KW + C_VW, 2 * C_KW + 2 * C_VW,
                    2 * C_KW + 2 * C_VW + D_W], axis=-1)
            y_c = retention_mix(q, k, v, g, cos, sin, ret_norm_g[j])
            y_d = rglru_mix(xb, gate, lru_conv_w[j], lru_conv_b[j], lru_w_r[j], lru_b_r[j],
                            lru_w_i[j], lru_b_i[j], lru_lambda[j])
            y = jnp.concatenate([y_c, y_d], axis=-1) @ odd_w_out[j]
        h = h + y
        h = h + cross_attend(rmsnorm(h, norm_xattn_g[layer]), mem_n, xattn_wq[layer],
                             xattn_wk[layer], xattn_wv[layer], xattn_wo[layer])
        h = h + sq_relu_mlp(rmsnorm(h, norm_mlp_g[layer]), mlp_w1[layer], mlp_w2[layer])
    return rmsnorm(h, final_norm_g)
```

```python
import functools

import jax
import jax.numpy as jnp
from jax import lax
from jax.experimental import pallas as pl
from jax.experimental.pallas import tpu as pltpu

D_MODEL = 4096
DEPTH = 4
CHUNK = 64
NORM_EPS = 1e-6
MIX_W = D_MODEL
A_W = MIX_W // 2
A_HEAD = 64
A_H = A_W // A_HEAD
A_LORA = 96
A_LN_EPS = 64e-5
B_W = MIX_W // 2
B_H = 8
B_HEAD = B_W // B_H
GATE_CAP = 15.0
C_VW = MIX_W // 2
C_H = 8
C_VHEAD = C_VW // C_H
C_KHEAD = C_VHEAD // 2
C_KW = C_H * C_KHEAD
ROPE_BASE = 10000.0
D_W = MIX_W // 2
D_BLOCKS = 16
D_BW = D_W // D_BLOCKS
CONV_W = 4
LRU_C = 8.0
D_FF = 4 * D_MODEL
XA_H = 4
XA_HEAD = 256
XA_W = XA_H * XA_HEAD
A_COLS = 4 * A_W + 2 * A_LORA
B_COLS = 4 * B_W + 2 * B_H

LANES = 128
SUBLANES = 8
VMEM_BYTES_V7X = 64 * 1024 * 1024

EV_LORA_W = 0
EV_LORA_A = LANES
EV_A_MAIN = 0
EV_LORA = 4 * A_W
EV_B_MAIN = EV_LORA + 2 * LANES
EV_GATES = EV_B_MAIN + 4 * B_W
EV_COLS_PAD = 17 * 1024
assert EV_GATES + 2 * B_H <= EV_COLS_PAD

BF16 = jnp.bfloat16
F32 = jnp.float32


def _cparams(dims, vmem_bytes):
    limit = min(int(vmem_bytes), VMEM_BYTES_V7X - 6 * 1024 * 1024)
    return pltpu.CompilerParams(dimension_semantics=dims, vmem_limit_bytes=limit)


def _dot(a, b):
    return jnp.dot(a, b, preferred_element_type=F32)


def _dot_nt(a, b):
    return lax.dot_general(a, b, (((1,), (1,)), ((), ())), preferred_element_type=F32)


def _dot_tn(a, b):
    return lax.dot_general(a, b, (((0,), (0,)), ((), ())), preferred_element_type=F32)


def _sigmoid(x):
    return 1.0 / (1.0 + jnp.exp(-x))


def _softplus(x):
    return jnp.maximum(x, 0.0) + jnp.log1p(jnp.exp(-jnp.abs(x)))


def _rmsnorm_kernel(x_ref, g_ref, o_ref):
    x = x_ref[...]
    ms = jnp.mean(x * x, axis=-1, keepdims=True)
    o_ref[...] = (x * lax.rsqrt(ms + NORM_EPS) * g_ref[...]).astype(o_ref.dtype)


def rmsnorm(x, g, out_dtype, tm=256):
    m, d = x.shape
    tm = min(tm, m)
    vmem = 2 * tm * d * (4 + jnp.dtype(out_dtype).itemsize) + 4 * d * 4 + (4 << 20)
    return pl.pallas_call(
        _rmsnorm_kernel,
        grid=(m // tm,),
        in_specs=[pl.BlockSpec((tm, d), lambda i: (i, 0)), pl.BlockSpec((1, d), lambda i: (0, 0))],
        out_specs=pl.BlockSpec((tm, d), lambda i: (i, 0)),
        out_shape=jax.ShapeDtypeStruct((m, d), out_dtype),
        compiler_params=_cparams(("parallel",), vmem),
        name="rmsnorm",
    )(x, g.reshape(1, d))


def _mm_kernel(*refs, nk, epilogue):
    if epilogue == "residual":
        x_ref, w_ref, r_ref, o_ref = refs[:4]
        rest = refs[4:]
    else:
        x_ref, w_ref, o_ref = refs[:3]
        r_ref = None
        rest = refs[3:]

    def finish(acc):
        if epilogue == "residual":
            acc = r_ref[...] + acc
        elif epilogue == "relu2":
            acc = jnp.square(jnp.maximum(acc, 0.0))
        o_ref[...] = acc.astype(o_ref.dtype)

    part = _dot(x_ref[...], w_ref[...])
    if nk == 1:
        finish(part)
        return
    acc_ref = rest[0]
    k = pl.program_id(2)

    @pl.when(k == 0)
    def _():
        acc_ref[...] = part

    @pl.when(jnp.logical_and(k > 0, k < nk - 1))
    def _():
        acc_ref[...] += part

    @pl.when(k == nk - 1)
    def _():
        finish(acc_ref[...] + part)


def matmul(x, w, out_dtype, epilogue="none", residual=None, tm=1024, tn=None, tk=None):
    m, kdim = x.shape
    _, n = w.shape
    tm = min(tm, m)
    if tk is None:
        tk = kdim if kdim <= 4096 else 2048
    nk = kdim // tk
    if tn is None:
        tn = 512 if tk > 2048 else 1024
    tn = min(tn, n)
    assert m % tm == 0 and n % tn == 0 and kdim % tk == 0
    in_specs = [pl.BlockSpec((tm, tk), lambda i, j, k: (i, k)), pl.BlockSpec((tk, tn), lambda i, j, k: (k, j))]
    args = [x, w]
    osz = jnp.dtype(out_dtype).itemsize
    vmem = 2 * (tm * tk * 2 + tk * tn * 2 + tm * tn * osz) + 2 * tm * tn * 4 + (4 << 20)
    if epilogue == "residual":
        in_specs.append(pl.BlockSpec((tm, tn), lambda i, j, k: (i, j)))
        args.append(residual)
        vmem += 2 * tm * tn * 4
    scratch = [pltpu.VMEM((tm, tn), F32)] if nk > 1 else []
    return pl.pallas_call(
        functools.partial(_mm_kernel, nk=nk, epilogue=epilogue),
        grid=(m // tm, n // tn, nk),
        in_specs=in_specs,
        out_specs=pl.BlockSpec((tm, tn), lambda i, j, k: (i, j)),
        out_shape=jax.ShapeDtypeStruct((m, n), out_dtype),
        scratch_shapes=scratch,
        compiler_params=_cparams(("parallel", "parallel", "arbitrary"), vmem),
        name="matmul_" + epilogue,
    )(*args)


def _shift_rows(x, prev_ref, d):
    ext = jnp.concatenate([prev_ref[...], x], axis=0)
    return pltpu.roll(ext, d, 0)[SUBLANES:]


def _pair_sum(x, first):
    s1 = jnp.sum(jnp.where(first, x, 0.0), axis=1, keepdims=True)
    s2 = jnp.sum(jnp.where(first, 0.0, x), axis=1, keepdims=True)
    return jnp.where(first, s1, s2)


def _rwkv_kernel(r_ref, k_ref, v_ref, g_ref, lo_ref, cp_ref, mulo_ref, wup_ref, aup_ref, o_ref,
                 state_ref, pr_ref, pk_ref, pv_ref, pg_ref, plo_ref, *, tb):
    L = CHUNK
    i = pl.program_id(1)

    @pl.when(i == 0)
    def _():
        state_ref[...] = jnp.zeros_like(state_ref)
        pr_ref[...] = jnp.zeros_like(pr_ref)
        pk_ref[...] = jnp.zeros_like(pk_ref)
        pv_ref[...] = jnp.zeros_like(pv_ref)
        pg_ref[...] = jnp.zeros_like(pg_ref)
        plo_ref[...] = jnp.zeros_like(plo_ref)

    cp = cp_ref[...]
    mu_r, mu_k, mu_v, mu_g = cp[0:1], cp[1:2], cp[2:3], cp[3:4]
    w0, a0, k_k, k_a, r_k, ln_w, ln_b = (cp[4:5], cp[5:6], cp[6:7], cp[7:8], cp[8:9], cp[9:10], cp[10:11])

    def shift_lerp(x_ref, prev_ref, mu):
        x = x_ref[...]
        sh = _shift_rows(x, prev_ref, 1)
        prev_ref[...] = x[tb - SUBLANES:tb]
        return x + (sh - x) * mu

    r = shift_lerp(r_ref, pr_ref, mu_r)
    k = shift_lerp(k_ref, pk_ref, mu_k)
    v = shift_lerp(v_ref, pv_ref, mu_v)
    g = shift_lerp(g_ref, pg_ref, mu_g)
    lo = shift_lerp(lo_ref, plo_ref, mulo_ref[...])
    w_lo = lo[:, EV_LORA_W:EV_LORA_W + LANES]
    a_lo = lo[:, EV_LORA_A:EV_LORA_A + LANES]

    lane = lax.broadcasted_iota(jnp.int32, (1, LANES), 1)
    first = lane < A_HEAD
    m1 = first.astype(F32)
    m2 = 1.0 - m1

    w = w0 + _dot(jnp.tanh(w_lo).astype(BF16), wup_ref[...])
    logw = -jnp.exp(-_softplus(-w) - 0.5)
    a = _sigmoid(a0 + _dot(a_lo.astype(BF16), aup_ref[...]))
    kk = k * k_k
    k = k * (1.0 + (a - 1.0) * k_a)
    kk = kk * lax.rsqrt(_pair_sum(kk * kk, first) + 1e-12)
    alpha = -kk
    beta = kk * a
    bonus = _pair_sum(r * k * r_k, first) * v

    row = lax.broadcasted_iota(jnp.int32, (L, L), 0)
    col = lax.broadcasted_iota(jnp.int32, (L, L), 1)
    tril_incl = row >= col
    tril_strict = row > col
    tril_f = tril_incl.astype(F32)
    eye_f = (row == col).astype(F32)
    bd_r = lax.broadcasted_iota(jnp.int32, (LANES, LANES), 0) // A_HEAD
    bd_c = lax.broadcasted_iota(jnp.int32, (LANES, LANES), 1) // A_HEAD
    blockdiag = bd_r == bd_c

    state = state_ref[...]
    outs = []
    for c in range(tb // L):
        sl = slice(c * L, (c + 1) * L)
        lw = logw[sl]
        cum = jnp.dot(tril_f, lw, preferred_element_type=F32, precision=lax.Precision.HIGHEST)
        e_pos = jnp.exp(cum)
        e_neg = jnp.exp(-cum)
        e_prev = jnp.exp(cum - lw)
        ab = alpha[sl] * e_prev
        rb = r[sl] * e_pos
        kt = (k[sl] * e_neg).astype(BF16)
        bt = (beta[sl] * e_neg).astype(BF16)
        vv = v[sl].astype(BF16)
        x = jnp.concatenate([ab * m1, ab * m2, rb * m1, rb * m2], axis=0).astype(BF16)
        xk = _dot_nt(x, kt)
        xb = _dot_nt(x, bt)
        st_b = state.astype(BF16)
        rhs = _dot_nt(ab.astype(BF16), st_b)
        o_c = _dot_nt(rb.astype(BF16), st_b)
        u = jnp.zeros((L, LANES), F32)
        for h, mh in ((0, m1), (1, m2)):
            a_ak = jnp.where(tril_strict, xk[h * L:(h + 1) * L], 0.0)
            n_h = jnp.where(tril_strict, xb[h * L:(h + 1) * L], 0.0)
            a_rk = jnp.where(tril_incl, xk[(2 + h) * L:(3 + h) * L], 0.0)
            a_rb = jnp.where(tril_incl, xb[(2 + h) * L:(3 + h) * L], 0.0)
            t_h = eye_f + n_h
            n_pow = n_h
            for _ in range(5):
                n_b = n_pow.astype(BF16)
                n_pow = _dot(n_b, n_b)
                t_h = t_h + _dot(t_h.astype(BF16), n_pow.astype(BF16))
            rhs_h = rhs + _dot(a_ak.astype(BF16), vv)
            u_h = _dot(t_h.astype(BF16), rhs_h.astype(BF16)) * mh
            u = u + u_h
            o_c = o_c + (_dot(a_rk.astype(BF16), vv) + _dot(a_rb.astype(BF16), u_h.astype(BF16))) * mh
        upd = _dot_tn(vv, kt) + _dot_tn(u.astype(BF16), bt)
        state = (state + jnp.where(blockdiag, upd, 0.0)) * e_pos[L - 1:L]
        outs.append(o_c)
    state_ref[...] = state
    o = jnp.concatenate(outs, axis=0) if len(outs) > 1 else outs[0]

    mean = _pair_sum(o, first) * (1.0 / A_HEAD)
    d = o - mean
    var = _pair_sum(d * d, first) * (1.0 / A_HEAD)
    y = d * lax.rsqrt(var + A_LN_EPS) * ln_w + ln_b
    o_ref[...] = ((y + bonus) * _sigmoid(g)).astype(o_ref.dtype)


def rwkv7_mix(p, mu, w0, w_up, a0, a_up, k_k, k_a, r_k, ln_w, ln_b, tb=256):
    s = p.shape[0]
    npair = A_H // 2
    nblk = A_W // LANES
    zeros = jnp.zeros((A_W,), F32)
    cp = jnp.stack([mu[0:A_W], mu[A_W:2 * A_W], mu[2 * A_W:3 * A_W], mu[3 * A_W:4 * A_W],
                    w0, a0, k_k, k_a, r_k, ln_w, ln_b, zeros, zeros, zeros, zeros, zeros])
    pad = jnp.zeros((LANES - A_LORA,), F32)
    mulo = jnp.concatenate([mu[4 * A_W:4 * A_W + A_LORA], pad, mu[4 * A_W + A_LORA:], pad]).reshape(1, 2 * LANES)
    rpad = jnp.zeros((LANES - A_LORA, A_W), F32)
    wup = jnp.concatenate([w_up, rpad], axis=0).astype(BF16)
    aup = jnp.concatenate([a_up, rpad], axis=0).astype(BF16)
    lora_blk = EV_LORA // (2 * LANES)

    def col(j):
        return pl.BlockSpec((tb, LANES), lambda hp, i, j=j: (i, j * nblk + hp))

    vmem = 2 * (4 * tb * LANES * 4 + tb * 2 * LANES * 4 + tb * LANES * 2) + (24 << 20)
    return pl.pallas_call(
        functools.partial(_rwkv_kernel, tb=tb),
        grid=(npair, s // tb),
        in_specs=[col(0), col(1), col(2), col(3),
                  pl.BlockSpec((tb, 2 * LANES), lambda hp, i: (i, lora_blk)),
                  pl.BlockSpec((16, LANES), lambda hp, i: (0, hp)),
                  pl.BlockSpec((1, 2 * LANES), lambda hp, i: (0, 0)),
                  pl.BlockSpec((LANES, LANES), lambda hp, i: (0, hp)),
                  pl.BlockSpec((LANES, LANES), lambda hp, i: (0, hp))],
        out_specs=pl.BlockSpec((tb, LANES), lambda hp, i: (i, hp)),
        out_shape=jax.ShapeDtypeStruct((s, A_W), BF16),
        scratch_shapes=[pltpu.VMEM((LANES, LANES), F32),
                        pltpu.VMEM((SUBLANES, LANES), F32), pltpu.VMEM((SUBLANES, LANES), F32),
                        pltpu.VMEM((SUBLANES, LANES), F32), pltpu.VMEM((SUBLANES, LANES), F32),
                        pltpu.VMEM((SUBLANES, 2 * LANES), F32)],
        compiler_params=_cparams(("parallel", "arbitrary"), vmem),
        name="rwkv7",
    )(p, p, p, p, p, cp, mulo, wup, aup)


def _mlstm_kernel(ib_ref, fb_ref, q_ref, k_ref, v_ref, og_ref, gt_ref, ng_ref, o_ref,
                  c_ref, n_ref, m_ref, *, tb):
    L = CHUNK
    h = pl.program_id(0)
    i = pl.program_id(1)

    @pl.when(i == 0)
    def _():
        c_ref[...] = jnp.zeros_like(c_ref)
        n_ref[...] = jnp.zeros_like(n_ref)
        m_ref[...] = jnp.zeros_like(m_ref)

    i_b = ib_ref[h]
    f_b = fb_ref[h]
    row = lax.broadcasted_iota(jnp.int32, (L, L), 0)
    col = lax.broadcasted_iota(jnp.int32, (L, L), 1)
    causal = row >= col
    eye = row == col

    def to_col(x_row):
        return jnp.sum(jnp.where(eye, jnp.broadcast_to(x_row, (L, L)), 0.0), axis=1, keepdims=True)

    c_st = c_ref[...]
    n_st = n_ref[...]
    m_st = m_ref[0:1, 0:1]
    outs = []
    for c in range(tb // L):
        sl = slice(c * L, (c + 1) * L)
        gi = gt_ref[c, pl.ds(h, 1), :]
        gf = gt_ref[c, pl.ds(B_H + h, 1), :]
        li_row = GATE_CAP * jnp.tanh((gi + i_b) / GATE_CAP)
        lf_row = -_softplus(-(GATE_CAP * jnp.tanh((gf + f_b) / GATE_CAP)))
        li_col = to_col(li_row)
        lf_col = to_col(lf_row)
        cum_col = jnp.sum(jnp.where(causal, jnp.broadcast_to(lf_row, (L, L)), 0.0), axis=1, keepdims=True)
        cum_row = jnp.sum(jnp.where(row <= col, jnp.broadcast_to(lf_col, (L, L)), 0.0), axis=0, keepdims=True)
        total = jnp.sum(lf_row, axis=1, keepdims=True)

        q = q_ref[sl, :].astype(BF16)
        kf = k_ref[sl, :] * (B_HEAD ** -0.5)
        kb = kf.astype(BF16)
        vf = v_ref[sl, :]
        vb = vf.astype(BF16)

        d_log = jnp.where(causal, cum_col - cum_row + li_row, -jnp.inf)
        m_inter = cum_col + m_st
        m_row = jnp.maximum(jnp.max(d_log, axis=1, keepdims=True), m_inter)
        s_qk = _dot_nt(q, kb) * jnp.exp(d_log - m_row)
        w_inter = jnp.exp(m_inter - m_row)
        num = _dot(s_qk.astype(BF16), vb) + w_inter * _dot(q, c_st.astype(BF16))
        qn = jnp.sum(q_ref[sl, :] * n_st, axis=1, keepdims=True)
        den = jnp.sum(s_qk, axis=1, keepdims=True) + w_inter * qn
        outs.append(num / jnp.maximum(jnp.abs(den), jnp.exp(-m_row)))

        log_w_col = total - cum_col + li_col
        m_new = jnp.maximum(total + m_st, jnp.max(log_w_col, axis=0, keepdims=True))
        w_k = jnp.exp(log_w_col - m_new)
        carry = jnp.exp(total + m_st - m_new)
        kw = kf * w_k
        c_st = carry * c_st + _dot_tn(kw.astype(BF16), vb)
        n_st = carry * n_st + jnp.sum(kw, axis=0, keepdims=True)
        m_st = m_new
    c_ref[...] = c_st
    n_ref[...] = n_st
    m_ref[...] = jnp.broadcast_to(m_st, m_ref.shape)

    hh = jnp.concatenate(outs, axis=0) if len(outs) > 1 else outs[0]
    ms = jnp.mean(hh * hh, axis=1, keepdims=True)
    y = hh * lax.rsqrt(ms + NORM_EPS) * ng_ref[...]
    o_ref[...] = (y * _sigmoid(og_ref[...])).astype(o_ref.dtype)


def mlstm_mix(p, gates_t, i_b, f_b, norm_g, tb=256):
    s = p.shape[0]
    base = EV_B_MAIN // B_HEAD
    nblk = B_W // B_HEAD

    def col(j):
        return pl.BlockSpec((tb, B_HEAD), lambda h, i, j=j: (i, base + j * nblk + h))

    smem = pl.BlockSpec(memory_space=pltpu.SMEM)
    vmem = 2 * (4 * tb * B_HEAD * 4 + tb * B_HEAD * 2) + (24 << 20)
    return pl.pallas_call(
        functools.partial(_mlstm_kernel, tb=tb),
        grid=(B_H, s // tb),
        in_specs=[smem, smem, col(0), col(1), col(2), col(3),
                  pl.BlockSpec((tb // CHUNK, 2 * B_H, CHUNK), lambda h, i: (i, 0, 0)),
                  pl.BlockSpec((1, B_HEAD), lambda h, i: (0, h))],
        out_specs=pl.BlockSpec((tb, B_HEAD), lambda h, i: (i, h)),
        out_shape=jax.ShapeDtypeStruct((s, B_W), BF16),
        scratch_shapes=[pltpu.VMEM((B_HEAD, B_HEAD), F32), pltpu.VMEM((1, B_HEAD), F32),
                        pltpu.VMEM((1, LANES), F32)],
        compiler_params=_cparams(("parallel", "arbitrary"), vmem),
        name="mlstm",
    )(i_b, f_b, p, p, p, p, gates_t, norm_g.reshape(1, B_W))


def _retention_kernel(lg_ref, q_ref, k_ref, v_ref, g_ref, cc_ref, ss_ref, ng_ref, o_ref, r_ref, *, tb):
    L = CHUNK
    h = pl.program_id(0)
    i = pl.program_id(1)

    @pl.when(i == 0)
    def _():
        r_ref[...] = jnp.zeros_like(r_ref)

    lg = lg_ref[h]
    row = lax.broadcasted_iota(jnp.int32, (L, L), 0)
    col = lax.broadcasted_iota(jnp.int32, (L, L), 1)
    intra = jnp.exp(lg * jnp.abs(row - col).astype(F32))
    pos = lax.broadcasted_iota(jnp.int32, (L, 1), 0).astype(F32)
    q_decay = jnp.exp(lg * (pos + 1.0))
    k_decay = jnp.exp(lg * (L - 1.0 - pos))
    chunk_decay = jnp.exp(jnp.full((1, 1), lg * L, F32))

    cc = cc_ref[...]
    ss = ss_ref[...]

    def rope(t):
        return t * cc + pltpu.roll(t, C_KHEAD // 2, 1) * ss

    q = rope(q_ref[...])
    k = rope(k_ref[...]) * (C_KHEAD ** -0.5)
    r_st = r_ref[...]
    outs = []
    for c in range(tb // L):
        sl = slice(c * L, (c + 1) * L)
        qb = q[sl].astype(BF16)
        kc = k[sl]
        vb = v_ref[sl, :].astype(BF16)
        s_qk = _dot_nt(qb, kc.astype(BF16)) * intra
        o_c = _dot(s_qk.astype(BF16), vb) + q_decay * _dot(qb, r_st.astype(BF16))
        r_st = chunk_decay * r_st + _dot_tn((kc * k_decay).astype(BF16), vb)
        outs.append(o_c)
    r_ref[...] = r_st
    o = jnp.concatenate(outs, axis=0) if len(outs) > 1 else outs[0]
    ms = jnp.mean(o * o, axis=1, keepdims=True)
    y = o * lax.rsqrt(ms + NORM_EPS) * ng_ref[...]
    g = g_ref[...]
    o_ref[...] = (y * (g * _sigmoid(g))).astype(o_ref.dtype)


def retention_mix(p, cc, ss, log_gamma, norm_g, tb=256):
    s = p.shape[0]
    kb = C_KW // C_KHEAD
    vb = (2 * C_KW) // C_VHEAD
    gb = (2 * C_KW + C_VW) // C_VHEAD
    smem = pl.BlockSpec(memory_space=pltpu.SMEM)
    vmem = 2 * (2 * tb * C_KHEAD * 4 + 2 * tb * C_VHEAD * 4 + 2 * tb * C_KHEAD * 4 + tb * C_VHEAD * 2) + (24 << 20)
    return pl.pallas_call(
        functools.partial(_retention_kernel, tb=tb),
        grid=(C_H, s // tb),
        in_specs=[smem,
                  pl.BlockSpec((tb, C_KHEAD), lambda h, i: (i, h)),
                  pl.BlockSpec((tb, C_KHEAD), lambda h, i: (i, kb + h)),
                  pl.BlockSpec((tb, C_VHEAD), lambda h, i: (i, vb + h)),
                  pl.BlockSpec((tb, C_VHEAD), lambda h, i: (i, gb + h)),
                  pl.BlockSpec((tb, C_KHEAD), lambda h, i: (i, 0)),
                  pl.BlockSpec((tb, C_KHEAD), lambda h, i: (i, 0)),
                  pl.BlockSpec((1, C_VHEAD), lambda h, i: (0, h))],
        out_specs=pl.BlockSpec((tb, C_VHEAD), lambda h, i: (i, h)),
        out_shape=jax.ShapeDtypeStruct((s, C_VW), BF16),
        scratch_shapes=[pltpu.VMEM((C_KHEAD, C_VHEAD), F32)],
        compiler_params=_cparams(("parallel", "arbitrary"), vmem),
        name="retention",
    )(log_gamma, p, p, p, p, cc, ss, norm_g.reshape(1, C_VW))


def _rglru_kernel(x_ref, gate_ref, cw_ref, cp_ref, wr_ref, wi_ref, o_ref, px_ref, h_ref, *, tb):
    i = pl.program_id(1)

    @pl.when(i == 0)
    def _():
        px_ref[...] = jnp.zeros_like(px_ref)
        h_ref[...] = jnp.zeros_like(h_ref)

    cw = cw_ref[...]
    cp = cp_ref[...]
    conv_b, b_r, b_i, lam = cp[0:1], cp[1:2], cp[2:3], cp[3:4]
    x = x_ref[...]
    ext = jnp.concatenate([px_ref[...], x], axis=0)
    xc = conv_b + x * cw[CONV_W - 1:CONV_W]
    for d in range(1, CONV_W):
        xc = xc + pltpu.roll(ext, d, 0)[SUBLANES:] * cw[CONV_W - 1 - d:CONV_W - d]
    px_ref[...] = x[tb - SUBLANES:tb]

    xcb = xc.astype(BF16)
    r = _sigmoid(_dot(xcb, wr_ref[0]) + b_r)
    ig = _sigmoid(_dot(xcb, wi_ref[0]) + b_i)
    log_a = -LRU_C * r * _softplus(-lam)
    a = jnp.exp(log_a)
    u = jnp.sqrt(-jnp.tanh(log_a) * (a * a + 1.0)) * (ig * xc)

    rows = lax.broadcasted_iota(jnp.int32, (tb, 1), 0)
    d = 1
    while d < tb:
        keep = rows >= d
        a_sh = jnp.where(keep, pltpu.roll(a, d, 0), 1.0)
        u_sh = jnp.where(keep, pltpu.roll(u, d, 0), 0.0)
        u = a * u_sh + u
        a = a * a_sh
        d *= 2
    hh = a * h_ref[0:1, :] + u
    h_ref[...] = jnp.broadcast_to(hh[tb - 1:tb], h_ref.shape)

    gt = gate_ref[...]
    gelu = 0.5 * gt * (1.0 + jnp.tanh(0.7978845608028654 * (gt + 0.044715 * (gt * gt * gt))))
    o_ref[...] = (hh * gelu).astype(o_ref.dtype)


def rglru_mix(p, conv_w, conv_b, w_r, b_r, w_i, b_i, lam, tb=256):
    s = p.shape[0]
    xb0 = (2 * C_KW + 2 * C_VW) // D_BW
    gt0 = xb0 + D_W // D_BW
    zeros = jnp.zeros((D_W,), F32)
    cp = jnp.stack([conv_b, b_r, b_i, lam, zeros, zeros, zeros, zeros])
    vmem = 2 * (2 * tb * D_BW * 4 + tb * D_BW * 2 + 2 * D_BW * D_BW * 2) + (24 << 20)
    return pl.pallas_call(
        functools.partial(_rglru_kernel, tb=tb),
        grid=(D_BLOCKS, s // tb),
        in_specs=[pl.BlockSpec((tb, D_BW), lambda n, i: (i, xb0 + n)),
                  pl.BlockSpec((tb, D_BW), lambda n, i: (i, gt0 + n)),
                  pl.BlockSpec((CONV_W, D_BW), lambda n, i: (0, n)),
                  pl.BlockSpec((SUBLANES, D_BW), lambda n, i: (0, n)),
                  pl.BlockSpec((1, D_BW, D_BW), lambda n, i: (n, 0, 0)),
                  pl.BlockSpec((1, D_BW, D_BW), lambda n, i: (n, 0, 0))],
        out_specs=pl.BlockSpec((tb, D_BW), lambda n, i: (i, n)),
        out_shape=jax.ShapeDtypeStruct((s, D_W), BF16),
        scratch_shapes=[pltpu.VMEM((SUBLANES, D_BW), F32), pltpu.VMEM((SUBLANES, D_BW), F32)],
        compiler_params=_cparams(("parallel", "arbitrary"), vmem),
        name="rglru",
    )(p, p, conv_w, cp, w_r.astype(BF16), w_i.astype(BF16))


def _xattn_kernel(q_ref, k_ref, v_ref, o_ref):
    for hd in range(XA_H):
        sl = slice(hd * XA_HEAD, (hd + 1) * XA_HEAD)
        sc = _dot_nt(q_ref[:, sl], k_ref[:, sl]) * (XA_HEAD ** -0.5)
        sc = sc - jnp.max(sc, axis=1, keepdims=True)
        e = jnp.exp(sc)
        pr = e / jnp.sum(e, axis=1, keepdims=True)
        o_ref[:, sl] = _dot(pr.astype(BF16), v_ref[:, sl]).astype(o_ref.dtype)


def xattn_core(q, kmem, vmem_, tm=512):
    s = q.shape[0]
    nm = kmem.shape[0]
    vmem = 2 * (2 * tm * XA_W * 2 + 2 * nm * XA_W * 2) + (16 << 20)
    return pl.pallas_call(
        _xattn_kernel,
        grid=(s // tm,),
        in_specs=[pl.BlockSpec((tm, XA_W), lambda i: (i, 0)),
                  pl.BlockSpec((nm, XA_W), lambda i: (0, 0)),
                  pl.BlockSpec((nm, XA_W), lambda i: (0, 0))],
        out_specs=pl.BlockSpec((tm, XA_W), lambda i: (i, 0)),
        out_shape=jax.ShapeDtypeStruct((s, XA_W), BF16),
        compiler_params=_cparams(("parallel",), vmem),
        name="xattn",
    )(q, kmem, vmem_)


def _even_w_in_padded(w):
    d = w.shape[0]
    wb = w.astype(BF16)

    def z(n):
        return jnp.zeros((d, n), BF16)

    a_main = wb[:, :4 * A_W]
    w_lo = wb[:, 4 * A_W:4 * A_W + A_LORA]
    a_lo = wb[:, 4 * A_W + A_LORA:A_COLS]
    b_main = wb[:, A_COLS:A_COLS + 4 * B_W]
    gates = wb[:, A_COLS + 4 * B_W:]
    used = EV_GATES + 2 * B_H
    return jnp.concatenate([a_main, w_lo, z(LANES - A_LORA), a_lo, z(LANES - A_LORA), b_main, gates,
                            z(EV_COLS_PAD - used)], axis=1)


def kernel(x, mem, mem_norm_g, norm_mix_g, norm_xattn_g, norm_mlp_g, xattn_wq, xattn_wk, xattn_wv, xattn_wo, mlp_w1, mlp_w2, even_w_in, even_w_out, rwkv_mu, rwkv_w0, rwkv_w_up, rwkv_a0, rwkv_a_up, rwkv_k_k, rwkv_k_a, rwkv_r_k, rwkv_ln_w, rwkv_ln_b, mlstm_i_b, mlstm_f_b, mlstm_norm_g, odd_w_in, odd_w_out, ret_norm_g, lru_conv_w, lru_conv_b, lru_w_r, lru_b_r, lru_w_i, lru_b_i, lru_lambda, final_norm_g):
    b, s, d = x.shape
    assert b == 1 and d == D_MODEL and s % 256 == 0
    h = x.reshape(s, d)
    mem_n = rmsnorm(mem.reshape(-1, d), mem_norm_g, BF16)

    positions = jnp.arange(s, dtype=F32)
    inv_freq = ROPE_BASE ** (-jnp.arange(0, C_KHEAD, 2, dtype=F32) / C_KHEAD)
    ang = positions[:, None] * inv_freq[None, :]
    cos, sin = jnp.cos(ang), jnp.sin(ang)
    rope_cc = jnp.concatenate([cos, cos], axis=1)
    rope_ss = jnp.concatenate([-sin, sin], axis=1)
    log_gamma = jnp.log1p(-jnp.exp2(-5.0 - jnp.arange(C_H, dtype=F32)))

    for layer in range(DEPTH):
        j = layer // 2
        xn = rmsnorm(h, norm_mix_g[layer], BF16)
        if layer % 2 == 0:
            p = matmul(xn, _even_w_in_padded(even_w_in[j]), F32)
            y_a = rwkv7_mix(p, rwkv_mu[j], rwkv_w0[j], rwkv_w_up[j], rwkv_a0[j], rwkv_a_up[j],
                            rwkv_k_k[j], rwkv_k_a[j], rwkv_r_k[j], rwkv_ln_w[j], rwkv_ln_b[j])
            gates = p[:, EV_GATES:EV_GATES + 2 * B_H]
            gates_t = gates.reshape(s // CHUNK, CHUNK, 2 * B_H).transpose(0, 2, 1)
            y_b = mlstm_mix(p, gates_t, mlstm_i_b[j], mlstm_f_b[j], mlstm_norm_g[j])
            y = jnp.concatenate([y_a, y_b], axis=1)
            w_out = even_w_out[j]
        else:
            p = matmul(xn, odd_w_in[j].astype(BF16), F32)
            y_c = retention_mix(p, rope_cc, rope_ss, log_gamma, ret_norm_g[j])
            y_d = rglru_mix(p, lru_conv_w[j], lru_conv_b[j], lru_w_r[j], lru_b_r[j],
                            lru_w_i[j], lru_b_i[j], lru_lambda[j])
            y = jnp.concatenate([y_c, y_d], axis=1)
            w_out = odd_w_out[j]
        h = matmul(y, w_out.astype(BF16), F32, epilogue="residual", residual=h)

        xn = rmsnorm(h, norm_xattn_g[layer], BF16)
        q = matmul(xn, xattn_wq[layer].astype(BF16), BF16)
        kmem = matmul(mem_n, xattn_wk[layer].astype(BF16), BF16)
        vmem_ = matmul(mem_n, xattn_wv[layer].astype(BF16), BF16)
        o = xattn_core(q, kmem, vmem_)
        h = matmul(o, xattn_wo[layer].astype(BF16), F32, epilogue="residual", residual=h)

        xn = rmsnorm(h, norm_mlp_g[layer], BF16)
        u = matmul(xn, mlp_w1[layer].astype(BF16), BF16, epilogue="relu2")
        h = matmul(u, mlp_w2[layer].astype(BF16), F32, epilogue="residual", residual=h)

    return rmsnorm(h, final_norm_g, F32).reshape(b, s, d)
```

```python
import functools

import jax
import jax.numpy as jnp
from jax import lax
from jax.experimental import pallas as pl
from jax.experimental.pallas import tpu as pltpu

D_MODEL = 4096
DEPTH = 4
CHUNK = 64
NORM_EPS = 1e-6
MIX_W = D_MODEL
A_W = MIX_W // 2
A_HEAD = 64
A_H = A_W // A_HEAD
A_LORA = 96
A_LN_EPS = 64e-5
B_W = MIX_W // 2
B_H = 8
B_HEAD = B_W // B_H
GATE_CAP = 15.0
C_VW = MIX_W // 2
C_H = 8
C_VHEAD = C_VW // C_H
C_KHEAD = C_VHEAD // 2
C_KW = C_H * C_KHEAD
ROPE_BASE = 10000.0
D_W = MIX_W // 2
D_BLOCKS = 16
D_BW = D_W // D_BLOCKS
CONV_W = 4
LRU_C = 8.0
D_FF = 4 * D_MODEL
XA_H = 4
XA_HEAD = 256
XA_W = XA_H * XA_HEAD
A_COLS = 4 * A_W + 2 * A_LORA
B_COLS = 4 * B_W + 2 * B_H

LANES = 128
SUBLANES = 8
VMEM_BYTES_V7X = 64 * 1024 * 1024

EV_LORA = 4 * A_W
EV_A_COLS = EV_LORA + 2 * LANES
EV_GATES = 4 * B_W
EV_B_COLS = EV_GATES + 2 * LANES
assert A_COLS <= EV_A_COLS and B_COLS <= EV_B_COLS

BF16 = jnp.bfloat16
F32 = jnp.float32


def _cparams(dims, vmem_bytes):
    limit = min(int(vmem_bytes), VMEM_BYTES_V7X - 4 * 1024 * 1024)
    return pltpu.CompilerParams(dimension_semantics=dims, vmem_limit_bytes=limit)


def _dot(a, b):
    return jnp.dot(a, b, preferred_element_type=F32)


def _dot_nt(a, b):
    return lax.dot_general(a, b, (((1,), (1,)), ((), ())), preferred_element_type=F32)


def _dot_tn(a, b):
    return lax.dot_general(a, b, (((0,), (0,)), ((), ())), preferred_element_type=F32)


def _sigmoid(x):
    return 1.0 / (1.0 + jnp.exp(-x))


def _softplus(x):
    return jnp.maximum(x, 0.0) + jnp.log1p(jnp.exp(-jnp.abs(x)))


def _rmsnorm_kernel(x_ref, g_ref, o_ref):
    x = x_ref[...]
    ms = jnp.mean(x * x, axis=-1, keepdims=True)
    o_ref[...] = (x * lax.rsqrt(ms + NORM_EPS) * g_ref[...]).astype(o_ref.dtype)


def rmsnorm(x, g, out_dtype, tm=256):
    m, d = x.shape
    tm = min(tm, m)
    vmem = 2 * tm * d * (4 + jnp.dtype(out_dtype).itemsize) + 4 * d * 4 + (4 << 20)
    return pl.pallas_call(
        _rmsnorm_kernel,
        grid=(m // tm,),
        in_specs=[pl.BlockSpec((tm, d), lambda i: (i, 0)), pl.BlockSpec((1, d), lambda i: (0, 0))],
        out_specs=pl.BlockSpec((tm, d), lambda i: (i, 0)),
        out_shape=jax.ShapeDtypeStruct((m, d), out_dtype),
        compiler_params=_cparams(("parallel",), vmem),
        name="rmsnorm",
    )(x, g.reshape(1, d))


def _mm_kernel(*refs, nk, epilogue):
    if epilogue == "residual":
        x_ref, w_ref, r_ref, o_ref = refs[:4]
        rest = refs[4:]
    else:
        x_ref, w_ref, o_ref = refs[:3]
        r_ref = None
        rest = refs[3:]

    def finish(acc):
        if epilogue == "residual":
            acc = r_ref[...] + acc
        elif epilogue == "relu2":
            acc = jnp.square(jnp.maximum(acc, 0.0))
        o_ref[...] = acc.astype(o_ref.dtype)

    part = _dot(x_ref[...], w_ref[...].astype(BF16))
    if nk == 1:
        finish(part)
        return
    acc_ref = rest[0]
    k = pl.program_id(2)

    @pl.when(k == 0)
    def _():
        acc_ref[...] = part

    @pl.when(jnp.logical_and(k > 0, k < nk - 1))
    def _():
        acc_ref[...] += part

    @pl.when(k == nk - 1)
    def _():
        finish(acc_ref[...] + part)


def matmul(x, w, layer, out_dtype, n=None, epilogue="none", residual=None, tm=2048, tn=None, tk=None):
    m, kdim = x.shape
    n = w.shape[2] if n is None else n
    tm = min(tm, m)
    if tk is None:
        tk = kdim if kdim <= 4096 else 2048
    nk = kdim // tk
    if tn is None:
        tn = 256 if nk == 1 else 512
    tn = min(tn, n)
    assert m % tm == 0 and n % tn == 0 and kdim % tk == 0
    in_specs = [pl.BlockSpec((tm, tk), lambda i, j, k: (i, k)),
                pl.BlockSpec((pl.Squeezed(), tk, tn), lambda i, j, k: (layer, k, j))]
    args = [x, w]
    osz = jnp.dtype(out_dtype).itemsize
    wsz = jnp.dtype(w.dtype).itemsize
    vmem = 2 * (tm * tk * 2 + tk * tn * wsz + tm * tn * osz) + tk * tn * 2 + 2 * tm * tn * 4 + (8 << 20)
    if epilogue == "residual":
        in_specs.append(pl.BlockSpec((tm, tn), lambda i, j, k: (i, j)))
        args.append(residual)
        vmem += 2 * tm * tn * 4
    scratch = [pltpu.VMEM((tm, tn), F32)] if nk > 1 else []
    return pl.pallas_call(
        functools.partial(_mm_kernel, nk=nk, epilogue=epilogue),
        grid=(m // tm, n // tn, nk),
        in_specs=in_specs,
        out_specs=pl.BlockSpec((tm, tn), lambda i, j, k: (i, j)),
        out_shape=jax.ShapeDtypeStruct((m, n), out_dtype),
        scratch_shapes=scratch,
        compiler_params=_cparams(("parallel", "parallel", "arbitrary"), vmem),
        name="matmul_" + epilogue,
    )(*args)


def _shift_rows(x, prev_ref, d):
    ext = jnp.concatenate([prev_ref[...], x], axis=0)
    return pltpu.roll(ext, d, 0)[SUBLANES:]


def _pair_sum(x, first):
    s1 = jnp.sum(jnp.where(first, x, 0.0), axis=1, keepdims=True)
    s2 = jnp.sum(jnp.where(first, 0.0, x), axis=1, keepdims=True)
    return jnp.where(first, s1, s2)


def _rwkv_kernel(r_ref, k_ref, v_ref, g_ref, lo_ref, cp_ref, mulo_ref, wup_ref, aup_ref, o_ref,
                 state_ref, pr_ref, pk_ref, pv_ref, pg_ref, plo_ref, *, tb):
    L = CHUNK
    i = pl.program_id(1)

    @pl.when(i == 0)
    def _():
        state_ref[...] = jnp.zeros_like(state_ref)
        pr_ref[...] = jnp.zeros_like(pr_ref)
        pk_ref[...] = jnp.zeros_like(pk_ref)
        pv_ref[...] = jnp.zeros_like(pv_ref)
        pg_ref[...] = jnp.zeros_like(pg_ref)
        plo_ref[...] = jnp.zeros_like(plo_ref)

    cp = cp_ref[...]
    mu_r, mu_k, mu_v, mu_g = cp[0:1], cp[1:2], cp[2:3], cp[3:4]
    w0, a0, k_k, k_a, r_k, ln_w, ln_b = (cp[4:5], cp[5:6], cp[6:7], cp[7:8], cp[8:9], cp[9:10], cp[10:11])

    def shift_lerp(x_ref, prev_ref, mu):
        x = x_ref[...]
        sh = _shift_rows(x, prev_ref, 1)
        prev_ref[...] = x[tb - SUBLANES:tb]
        return x + (sh - x) * mu

    r = shift_lerp(r_ref, pr_ref, mu_r)
    k = shift_lerp(k_ref, pk_ref, mu_k)
    v = shift_lerp(v_ref, pv_ref, mu_v)
    g = shift_lerp(g_ref, pg_ref, mu_g)
    lo = shift_lerp(lo_ref, plo_ref, mulo_ref[...])

    lane = lax.broadcasted_iota(jnp.int32, (1, LANES), 1)
    first = lane < A_HEAD
    m1 = first.astype(F32)
    m2 = 1.0 - m1

    w = w0 + _dot(jnp.tanh(lo).astype(BF16), wup_ref[...])
    logw = -jnp.exp(-_softplus(-w) - 0.5)
    a = _sigmoid(a0 + _dot(lo.astype(BF16), aup_ref[...]))
    kk = k * k_k
    k = k * (1.0 + (a - 1.0) * k_a)
    kk = kk * lax.rsqrt(_pair_sum(kk * kk, first) + 1e-12)
    alpha = -kk
    beta = kk * a
    bonus = _pair_sum(r * k * r_k, first) * v

    nc = tb // L
    row = lax.broadcasted_iota(jnp.int32, (L, L), 0)
    col = lax.broadcasted_iota(jnp.int32, (L, L), 1)
    tril_strict = row > col
    row2 = lax.broadcasted_iota(jnp.int32, (L, LANES), 0)
    col2 = lax.broadcasted_iota(jnp.int32, (L, LANES), 1) % L
    tril_incl2 = row2 >= col2
    bd_r = lax.broadcasted_iota(jnp.int32, (LANES, LANES), 0) // A_HEAD
    bd_c = lax.broadcasted_iota(jnp.int32, (LANES, LANES), 1) // A_HEAD
    blockdiag = bd_r == bd_c

    pos = lax.broadcasted_iota(jnp.int32, (tb, 1), 0) % L
    cum = logw
    step = 1
    while step < L:
        cum = cum + jnp.where(pos >= step, pltpu.roll(cum, step, 0), 0.0)
        step *= 2
    e_pos = jnp.exp(cum)
    e_neg = jnp.exp(-cum)
    ab = alpha * jnp.exp(cum - logw)
    rb = r * e_pos
    kt = (k * e_neg).astype(BF16)
    bt = (beta * e_neg).astype(BF16)
    vv = v.astype(BF16)
    heads = ((m1, ab * m1, (rb * m1).astype(BF16)), (m2, ab * m2, (rb * m2).astype(BF16)))
    zero_top = jnp.zeros((L, LANES), BF16)

    sls = [slice(c * L, (c + 1) * L) for c in range(nc)]
    kb = [jnp.concatenate([kt[sl], bt[sl]], axis=0) for sl in sls]
    xa = []
    for c, sl in enumerate(sls):
        x = jnp.concatenate([heads[0][1][sl].astype(BF16), heads[1][1][sl].astype(BF16),
                             heads[0][2][sl], heads[1][2][sl]], axis=0)
        xa.append(_dot_nt(x, kb[c]))
    prob = [(c, h) for c in range(nc) for h in range(2)]
    n_pow, ys, a_r = [], [], []
    for c, h in prob:
        an = xa[c][h * L:(h + 1) * L]
        a_ak = jnp.where(tril_strict, an[:, :L], 0.0).astype(BF16)
        n_pow.append(jnp.where(tril_strict, an[:, L:], 0.0).astype(BF16))
        a_r.append(jnp.where(tril_incl2, xa[c][(2 + h) * L:(3 + h) * L], 0.0).astype(BF16))
        ys.append(jnp.concatenate([_dot(a_ak, vv[sls[c]]), heads[h][1][sls[c]]], axis=1))
    for lvl in range(6):
        ys = [y + _dot(n, y.astype(BF16)) for n, y in zip(n_pow, ys)]
        if lvl < 5:
            n_pow = [_dot(n, n).astype(BF16) for n in n_pow]
    q_c, o0_c, g_c, pb_c = [], [], [], []
    for c, sl in enumerate(sls):
        top = jnp.concatenate([vv[sl], zero_top], axis=1)
        y2 = []
        o0 = jnp.zeros((L, LANES), F32)
        qq = rb[sl]
        for h in range(2):
            mh = heads[h][0]
            y = ys[2 * c + h]
            y = jnp.concatenate([y[:, :LANES] * mh, y[:, LANES:]], axis=1)
            y2.append(y)
            out = _dot(a_r[2 * c + h], jnp.concatenate([top, y.astype(BF16)], axis=0))
            o0 = o0 + out[:, :LANES] * mh
            qq = qq + out[:, LANES:]
        ysum = (y2[0] + y2[1]).astype(BF16)
        gp = _dot_tn(jnp.concatenate([top, ysum], axis=0), kb[c])
        g_c.append(jnp.where(blockdiag, gp[:LANES], 0.0))
        pb_c.append(jnp.where(blockdiag, gp[LANES:], 0.0).astype(BF16))
        q_c.append(qq.astype(BF16))
        o0_c.append(o0)

    state = state_ref[...]
    outs = []
    for c in range(nc):
        st_b = state.astype(BF16)
        outs.append(o0_c[c] + _dot_nt(q_c[c], st_b))
        state = (state + _dot(st_b, pb_c[c]) + g_c[c]) * e_pos[(c + 1) * L - 1:(c + 1) * L]
    state_ref[...] = state
    o = jnp.concatenate(outs, axis=0) if len(outs) > 1 else outs[0]

    mean = _pair_sum(o, first) * (1.0 / A_HEAD)
    d = o - mean
    var = _pair_sum(d * d, first) * (1.0 / A_HEAD)
    y = d * lax.rsqrt(var + A_LN_EPS) * ln_w + ln_b
    o_ref[...] = ((y + bonus) * _sigmoid(g)).astype(o_ref.dtype)


def rwkv7_mix(p, mu, w0, w_up, a0, a_up, k_k, k_a, r_k, ln_w, ln_b, tb=1024):
    s = p.shape[0]
    tb = min(tb, s)
    assert s % tb == 0
    npair = A_H // 2
    nblk = A_W // LANES
    zeros = jnp.zeros((A_W,), F32)
    cp = jnp.stack([mu[0:A_W], mu[A_W:2 * A_W], mu[2 * A_W:3 * A_W], mu[3 * A_W:4 * A_W],
                    w0, a0, k_k, k_a, r_k, ln_w, ln_b, zeros, zeros, zeros, zeros, zeros])
    tail = 2 * LANES - 2 * A_LORA
    mulo = jnp.concatenate([mu[4 * A_W:], jnp.zeros((tail,), F32)]).reshape(1, 2 * LANES)
    wup = jnp.concatenate([w_up, jnp.zeros((2 * LANES - A_LORA, A_W), F32)], axis=0).astype(BF16)
    aup = jnp.concatenate([jnp.zeros((A_LORA, A_W), F32), a_up, jnp.zeros((tail, A_W), F32)], axis=0).astype(BF16)
    lora_blk = EV_LORA // (2 * LANES)

    def col(j):
        return pl.BlockSpec((tb, LANES), lambda hp, i, j=j: (i, j * nblk + hp))

    vmem = 2 * (4 * tb * LANES * 4 + tb * 2 * LANES * 4 + tb * LANES * 2) + (24 << 20)
    return pl.pallas_call(
        functools.partial(_rwkv_kernel, tb=tb),
        grid=(npair, s // tb),
        in_specs=[col(0), col(1), col(2), col(3),
                  pl.BlockSpec((tb, 2 * LANES), lambda hp, i: (i, lora_blk)),
                  pl.BlockSpec((16, LANES), lambda hp, i: (0, hp)),
                  pl.BlockSpec((1, 2 * LANES), lambda hp, i: (0, 0)),
                  pl.BlockSpec((2 * LANES, LANES), lambda hp, i: (0, hp)),
                  pl.BlockSpec((2 * LANES, LANES), lambda hp, i: (0, hp))],
        out_specs=pl.BlockSpec((tb, LANES), lambda hp, i: (i, hp)),
        out_shape=jax.ShapeDtypeStruct((s, A_W), BF16),
        scratch_shapes=[pltpu.VMEM((LANES, LANES), F32),
                        pltpu.VMEM((SUBLANES, LANES), F32), pltpu.VMEM((SUBLANES, LANES), F32),
                        pltpu.VMEM((SUBLANES, LANES), F32), pltpu.VMEM((SUBLANES, LANES), F32),
                        pltpu.VMEM((SUBLANES, 2 * LANES), F32)],
        compiler_params=_cparams(("parallel", "arbitrary"), vmem),
        name="rwkv7",
    )(p, p, p, p, p, cp, mulo, wup, aup)


def _mlstm_kernel(ib_ref, fb_ref, q_ref, k_ref, v_ref, og_ref, gt_ref, ng_ref, o_ref,
                  c_ref, n_ref, m_ref, *, tb):
    L = CHUNK
    h = pl.program_id(0)
    i = pl.program_id(1)

    @pl.when(i == 0)
    def _():
        c_ref[...] = jnp.zeros_like(c_ref)
        n_ref[...] = jnp.zeros_like(n_ref)
        m_ref[...] = jnp.zeros_like(m_ref)

    i_b = ib_ref[h]
    f_b = fb_ref[h]
    row = lax.broadcasted_iota(jnp.int32, (L, L), 0)
    col = lax.broadcasted_iota(jnp.int32, (L, L), 1)
    causal = row >= col
    eye = row == col

    def to_col(x_row):
        return jnp.sum(jnp.where(eye, jnp.broadcast_to(x_row, (L, L)), 0.0), axis=1, keepdims=True)

    c_st = c_ref[...]
    n_st = n_ref[...]
    m_st = m_ref[0:1, 0:1]
    outs = []
    for c in range(tb // L):
        sl = slice(c * L, (c + 1) * L)
        gi = gt_ref[c, pl.ds(h, 1), :]
        gf = gt_ref[c, pl.ds(B_H + h, 1), :]
        li_row = GATE_CAP * jnp.tanh((gi + i_b) / GATE_CAP)
        lf_row = -_softplus(-(GATE_CAP * jnp.tanh((gf + f_b) / GATE_CAP)))
        li_col = to_col(li_row)
        lf_col = to_col(lf_row)
        cum_col = jnp.sum(jnp.where(causal, jnp.broadcast_to(lf_row, (L, L)), 0.0), axis=1, keepdims=True)
        cum_row = jnp.sum(jnp.where(row <= col, jnp.broadcast_to(lf_col, (L, L)), 0.0), axis=0, keepdims=True)
        total = jnp.sum(lf_row, axis=1, keepdims=True)

        q = q_ref[sl, :].astype(BF16)
        kf = k_ref[sl, :] * (B_HEAD ** -0.5)
        kb = kf.astype(BF16)
        vf = v_ref[sl, :]
        vb = vf.astype(BF16)

        d_log = jnp.where(causal, cum_col - cum_row + li_row, -jnp.inf)
        m_inter = cum_col + m_st
        m_row = jnp.maximum(jnp.max(d_log, axis=1, keepdims=True), m_inter)
        s_qk = _dot_nt(q, kb) * jnp.exp(d_log - m_row)
        w_inter = jnp.exp(m_inter - m_row)
        num = _dot(s_qk.astype(BF16), vb) + w_inter * _dot(q, c_st.astype(BF16))
        qn = jnp.sum(q_ref[sl, :] * n_st, axis=1, keepdims=True)
        den = jnp.sum(s_qk, axis=1, keepdims=True) + w_inter * qn
        outs.append(num / jnp.maximum(jnp.abs(den), jnp.exp(-m_row)))

        log_w_col = total - cum_col + li_col
        m_new = jnp.maximum(total + m_st, jnp.max(log_w_col, axis=0, keepdims=True))
        w_k = jnp.exp(log_w_col - m_new)
        carry = jnp.exp(total + m_st - m_new)
        kw = kf * w_k
        c_st = carry * c_st + _dot_tn(kw.astype(BF16), vb)
        n_st = carry * n_st + jnp.sum(kw, axis=0, keepdims=True)
        m_st = m_new
    c_ref[...] = c_st
    n_ref[...] = n_st
    m_ref[...] = jnp.broadcast_to(m_st, m_ref.shape)

    hh = jnp.concatenate(outs, axis=0) if len(outs) > 1 else outs[0]
    ms = jnp.mean(hh * hh, axis=1, keepdims=True)
    y = hh * lax.rsqrt(ms + NORM_EPS) * ng_ref[...]
    o_ref[...] = (y * _sigmoid(og_ref[...])).astype(o_ref.dtype)


def mlstm_mix(p, gates_t, i_b, f_b, norm_g, tb=256):
    s = p.shape[0]
    base = 0
    nblk = B_W // B_HEAD

    def col(j):
        return pl.BlockSpec((tb, B_HEAD), lambda h, i, j=j: (i, base + j * nblk + h))

    smem = pl.BlockSpec(memory_space=pltpu.SMEM)
    vmem = 2 * (4 * tb * B_HEAD * 4 + tb * B_HEAD * 2) + (24 << 20)
    return pl.pallas_call(
        functools.partial(_mlstm_kernel, tb=tb),
        grid=(B_H, s // tb),
        in_specs=[smem, smem, col(0), col(1), col(2), col(3),
                  pl.BlockSpec((tb // CHUNK, 2 * B_H, CHUNK), lambda h, i: (i, 0, 0)),
                  pl.BlockSpec((1, B_HEAD), lambda h, i: (0, h))],
        out_specs=pl.BlockSpec((tb, B_HEAD), lambda h, i: (i, h)),
        out_shape=jax.ShapeDtypeStruct((s, B_W), BF16),
        scratch_shapes=[pltpu.VMEM((B_HEAD, B_HEAD), F32), pltpu.VMEM((1, B_HEAD), F32),
                        pltpu.VMEM((1, LANES), F32)],
        compiler_params=_cparams(("parallel", "arbitrary"), vmem),
        name="mlstm",
    )(i_b, f_b, p, p, p, p, gates_t, norm_g.reshape(1, B_W))


def _retention_kernel(lg_ref, q_ref, k_ref, v_ref, g_ref, cc_ref, ss_ref, ng_ref, o_ref, r_ref, *, tb):
    L = CHUNK
    h = pl.program_id(0)
    i = pl.program_id(1)

    @pl.when(i == 0)
    def _():
        r_ref[...] = jnp.zeros_like(r_ref)

    lg = lg_ref[h]
    row = lax.broadcasted_iota(jnp.int32, (L, L), 0)
    col = lax.broadcasted_iota(jnp.int32, (L, L), 1)
    intra = jnp.exp(lg * jnp.abs(row - col).astype(F32))
    pos = lax.broadcasted_iota(jnp.int32, (L, 1), 0).astype(F32)
    q_decay = jnp.exp(lg * (pos + 1.0))
    k_decay = jnp.exp(lg * (L - 1.0 - pos))
    chunk_decay = jnp.exp(jnp.full((1, 1), lg * L, F32))

    cc = cc_ref[...]
    ss = ss_ref[...]

    def rope(t):
        return t * cc + pltpu.roll(t, C_KHEAD // 2, 1) * ss

    q = rope(q_ref[...])
    k = rope(k_ref[...]) * (C_KHEAD ** -0.5)
    r_st = r_ref[...]
    outs = []
    for c in range(tb // L):
        sl = slice(c * L, (c + 1) * L)
        qb = q[sl].astype(BF16)
        kc = k[sl]
        vb = v_ref[sl, :].astype(BF16)
        s_qk = _dot_nt(qb, kc.astype(BF16)) * intra
        o_c = _dot(s_qk.astype(BF16), vb) + q_decay * _dot(qb, r_st.astype(BF16))
        r_st = chunk_decay * r_st + _dot_tn((kc * k_decay).astype(BF16), vb)
        outs.append(o_c)
    r_ref[...] = r_st
    o = jnp.concatenate(outs, axis=0) if len(outs) > 1 else outs[0]
    ms = jnp.mean(o * o, axis=1, keepdims=True)
    y = o * lax.rsqrt(ms + NORM_EPS) * ng_ref[...]
    g = g_ref[...]
    o_ref[...] = (y * (g * _sigmoid(g))).astype(o_ref.dtype)


def retention_mix(p, cc, ss, log_gamma, norm_g, tb=256):
    s = p.shape[0]
    kb = C_KW // C_KHEAD
    vb = (2 * C_KW) // C_VHEAD
    gb = (2 * C_KW + C_VW) // C_VHEAD
    smem = pl.BlockSpec(memory_space=pltpu.SMEM)
    vmem = 2 * (2 * tb * C_KHEAD * 4 + 2 * tb * C_VHEAD * 4 + 2 * tb * C_KHEAD * 4 + tb * C_VHEAD * 2) + (24 << 20)
    return pl.pallas_call(
        functools.partial(_retention_kernel, tb=tb),
        grid=(C_H, s // tb),
        in_specs=[smem,
                  pl.BlockSpec((tb, C_KHEAD), lambda h, i: (i, h)),
                  pl.BlockSpec((tb, C_KHEAD), lambda h, i: (i, kb + h)),
                  pl.BlockSpec((tb, C_VHEAD), lambda h, i: (i, vb + h)),
                  pl.BlockSpec((tb, C_VHEAD), lambda h, i: (i, gb + h)),
                  pl.BlockSpec((tb, C_KHEAD), lambda h, i: (i, 0)),
                  pl.BlockSpec((tb, C_KHEAD), lambda h, i: (i, 0)),
                  pl.BlockSpec((1, C_VHEAD), lambda h, i: (0, h))],
        out_specs=pl.BlockSpec((tb, C_VHEAD), lambda h, i: (i, h)),
        out_shape=jax.ShapeDtypeStruct((s, C_VW), BF16),
        scratch_shapes=[pltpu.VMEM((C_KHEAD, C_VHEAD), F32)],
        compiler_params=_cparams(("parallel", "arbitrary"), vmem),
        name="retention",
    )(log_gamma, p, p, p, p, cc, ss, norm_g.reshape(1, C_VW))


def _rglru_kernel(x_ref, gate_ref, cw_ref, cp_ref, wr_ref, wi_ref, o_ref, px_ref, h_ref, *, tb):
    i = pl.program_id(1)

    @pl.when(i == 0)
    def _():
        px_ref[...] = jnp.zeros_like(px_ref)
        h_ref[...] = jnp.zeros_like(h_ref)

    cw = cw_ref[...]
    cp = cp_ref[...]
    conv_b, b_r, b_i, lam = cp[0:1], cp[1:2], cp[2:3], cp[3:4]
    x = x_ref[...]
    ext = jnp.concatenate([px_ref[...], x], axis=0)
    xc = conv_b + x * cw[CONV_W - 1:CONV_W]
    for d in range(1, CONV_W):
        xc = xc + pltpu.roll(ext, d, 0)[SUBLANES:] * cw[CONV_W - 1 - d:CONV_W - d]
    px_ref[...] = x[tb - SUBLANES:tb]

    xcb = xc.astype(BF16)
    r = _sigmoid(_dot(xcb, wr_ref[0]) + b_r)
    ig = _sigmoid(_dot(xcb, wi_ref[0]) + b_i)
    log_a = -LRU_C * r * _softplus(-lam)
    a = jnp.exp(log_a)
    u = jnp.sqrt(-jnp.tanh(log_a) * (a * a + 1.0)) * (ig * xc)

    rows = lax.broadcasted_iota(jnp.int32, (tb, 1), 0)
    d = 1
    while d < tb:
        keep = rows >= d
        a_sh = jnp.where(keep, pltpu.roll(a, d, 0), 1.0)
        u_sh = jnp.where(keep, pltpu.roll(u, d, 0), 0.0)
        u = a * u_sh + u
        a = a * a_sh
        d *= 2
    hh = a * h_ref[0:1, :] + u
    h_ref[...] = jnp.broadcast_to(hh[tb - 1:tb], h_ref.shape)

    gt = gate_ref[...]
    gelu = 0.5 * gt * (1.0 + jnp.tanh(0.7978845608028654 * (gt + 0.044715 * (gt * gt * gt))))
    o_ref[...] = (hh * gelu).astype(o_ref.dtype)


def rglru_mix(p, conv_w, conv_b, w_r, b_r, w_i, b_i, lam, tb=256):
    s = p.shape[0]
    xb0 = (2 * C_KW + 2 * C_VW) // D_BW
    gt0 = xb0 + D_W // D_BW
    zeros = jnp.zeros((D_W,), F32)
    cp = jnp.stack([conv_b, b_r, b_i, lam, zeros, zeros, zeros, zeros])
    vmem = 2 * (2 * tb * D_BW * 4 + tb * D_BW * 2 + 2 * D_BW * D_BW * 2) + (24 << 20)
    return pl.pallas_call(
        functools.partial(_rglru_kernel, tb=tb),
        grid=(D_BLOCKS, s // tb),
        in_specs=[pl.BlockSpec((tb, D_BW), lambda n, i: (i, xb0 + n)),
                  pl.BlockSpec((tb, D_BW), lambda n, i: (i, gt0 + n)),
                  pl.BlockSpec((CONV_W, D_BW), lambda n, i: (0, n)),
                  pl.BlockSpec((SUBLANES, D_BW), lambda n, i: (0, n)),
                  pl.BlockSpec((1, D_BW, D_BW), lambda n, i: (n, 0, 0)),
                  pl.BlockSpec((1, D_BW, D_BW), lambda n, i: (n, 0, 0))],
        out_specs=pl.BlockSpec((tb, D_BW), lambda n, i: (i, n)),
        out_shape=jax.ShapeDtypeStruct((s, D_W), BF16),
        scratch_shapes=[pltpu.VMEM((SUBLANES, D_BW), F32), pltpu.VMEM((SUBLANES, D_BW), F32)],
        compiler_params=_cparams(("parallel", "arbitrary"), vmem),
        name="rglru",
    )(p, p, conv_w, cp, w_r.astype(BF16), w_i.astype(BF16))


def _xattn_kernel(q_ref, k_ref, v_ref, o_ref):
    for hd in range(XA_H):
        sl = slice(hd * XA_HEAD, (hd + 1) * XA_HEAD)
        sc = _dot_nt(q_ref[:, sl], k_ref[:, sl]) * (XA_HEAD ** -0.5)
        sc = sc - jnp.max(sc, axis=1, keepdims=True)
        e = jnp.exp(sc)
        pr = e / jnp.sum(e, axis=1, keepdims=True)
        o_ref[:, sl] = _dot(pr.astype(BF16), v_ref[:, sl]).astype(o_ref.dtype)


def xattn_core(q, kmem, vmem_, tm=512):
    s = q.shape[0]
    nm = kmem.shape[0]
    vmem = 2 * (2 * tm * XA_W * 2 + 2 * nm * XA_W * 2) + (16 << 20)
    return pl.pallas_call(
        _xattn_kernel,
        grid=(s // tm,),
        in_specs=[pl.BlockSpec((tm, XA_W), lambda i: (i, 0)),
                  pl.BlockSpec((nm, XA_W), lambda i: (0, 0)),
                  pl.BlockSpec((nm, XA_W), lambda i: (0, 0))],
        out_specs=pl.BlockSpec((tm, XA_W), lambda i: (i, 0)),
        out_shape=jax.ShapeDtypeStruct((s, XA_W), BF16),
        compiler_params=_cparams(("parallel",), vmem),
        name="xattn",
    )(q, kmem, vmem_)


def kernel(x, mem, mem_norm_g, norm_mix_g, norm_xattn_g, norm_mlp_g, xattn_wq, xattn_wk, xattn_wv, xattn_wo, mlp_w1, mlp_w2, even_w_in, even_w_out, rwkv_mu, rwkv_w0, rwkv_w_up, rwkv_a0, rwkv_a_up, rwkv_k_k, rwkv_k_a, rwkv_r_k, rwkv_ln_w, rwkv_ln_b, mlstm_i_b, mlstm_f_b, mlstm_norm_g, odd_w_in, odd_w_out, ret_norm_g, lru_conv_w, lru_conv_b, lru_w_r, lru_b_r, lru_w_i, lru_b_i, lru_lambda, final_norm_g):
    b, s, d = x.shape
    assert b == 1 and d == D_MODEL and s % 256 == 0
    h = x.reshape(s, d)
    mem_n = rmsnorm(mem.reshape(-1, d), mem_norm_g, BF16)
    even_w_b = jnp.pad(even_w_in[:, :, A_COLS:], ((0, 0), (0, 0), (0, EV_B_COLS - B_COLS)))

    positions = jnp.arange(s, dtype=F32)
    inv_freq = ROPE_BASE ** (-jnp.arange(0, C_KHEAD, 2, dtype=F32) / C_KHEAD)
    ang = positions[:, None] * inv_freq[None, :]
    cos, sin = jnp.cos(ang), jnp.sin(ang)
    rope_cc = jnp.concatenate([cos, cos], axis=1)
    rope_ss = jnp.concatenate([-sin, sin], axis=1)
    log_gamma = jnp.log1p(-jnp.exp2(-5.0 - jnp.arange(C_H, dtype=F32)))

    for layer in range(DEPTH):
        j = layer // 2
        xn = rmsnorm(h, norm_mix_g[layer], BF16)
        if layer % 2 == 0:
            p_a = matmul(xn, even_w_in, j, F32, n=EV_A_COLS)
            p_b = matmul(xn, even_w_b, j, F32)
            y_a = rwkv7_mix(p_a, rwkv_mu[j], rwkv_w0[j], rwkv_w_up[j], rwkv_a0[j], rwkv_a_up[j],
                            rwkv_k_k[j], rwkv_k_a[j], rwkv_r_k[j], rwkv_ln_w[j], rwkv_ln_b[j])
            gates = p_b[:, EV_GATES:EV_GATES + 2 * B_H]
            gates_t = gates.reshape(s // CHUNK, CHUNK, 2 * B_H).transpose(0, 2, 1)
            y_b = mlstm_mix(p_b, gates_t, mlstm_i_b[j], mlstm_f_b[j], mlstm_norm_g[j])
            y = jnp.concatenate([y_a, y_b], axis=1)
            w_out = even_w_out
        else:
            p = matmul(xn, odd_w_in, j, F32)
            y_c = retention_mix(p, rope_cc, rope_ss, log_gamma, ret_norm_g[j])
            y_d = rglru_mix(p, lru_conv_w[j], lru_conv_b[j], lru_w_r[j], lru_b_r[j],
                            lru_w_i[j], lru_b_i[j], lru_lambda[j])
            y = jnp.concatenate([y_c, y_d], axis=1)
            w_out = odd_w_out
        h = matmul(y, w_out, j, F32, epilogue="residual", residual=h)

        xn = rmsnorm(h, norm_xattn_g[layer], BF16)
        q = matmul(xn, xattn_wq, layer, BF16)
        kmem = matmul(mem_n, xattn_wk, layer, BF16)
        vmem_ = matmul(mem_n, xattn_wv, layer, BF16)
        o = xattn_core(q, kmem, vmem_)
        h = matmul(o, xattn_wo, layer, F32, epilogue="residual", residual=h)

        xn = rmsnorm(h, norm_mlp_g[layer], BF16)
        u = matmul(xn, mlp_w1, layer, BF16, epilogue="relu2")
        h = matmul(u, mlp_w2, layer, F32, epilogue="residual", residual=h)

    return rmsnorm(h, final_norm_g, F32).reshape(b, s, d)
```

```python
import functools

import jax
import jax.numpy as jnp
from jax import lax
from jax.experimental import pallas as pl
from jax.experimental.pallas import tpu as pltpu

D_MODEL = 4096
DEPTH = 4
CHUNK = 64
NORM_EPS = 1e-6
MIX_W = D_MODEL
A_W = MIX_W // 2
A_HEAD = 64
A_H = A_W // A_HEAD
A_LORA = 96
A_LN_EPS = 64e-5
B_W = MIX_W // 2
B_H = 8
B_HEAD = B_W // B_H
GATE_CAP = 15.0
C_VW = MIX_W // 2
C_H = 8
C_VHEAD = C_VW // C_H
C_KHEAD = C_VHEAD // 2
C_KW = C_H * C_KHEAD
ROPE_BASE = 10000.0
D_W = MIX_W // 2
D_BLOCKS = 16
D_BW = D_W // D_BLOCKS
CONV_W = 4
LRU_C = 8.0
D_FF = 4 * D_MODEL
XA_H = 4
XA_HEAD = 256
XA_W = XA_H * XA_HEAD
A_COLS = 4 * A_W + 2 * A_LORA
B_COLS = 4 * B_W + 2 * B_H

LANES = 128
SUBLANES = 8
VMEM_BYTES_V7X = 64 * 1024 * 1024

EV_LORA = 4 * A_W
EV_A_COLS = EV_LORA + 2 * LANES
EV_GATES = 4 * B_W
EV_B_COLS = EV_GATES + 2 * LANES
assert A_COLS <= EV_A_COLS and B_COLS <= EV_B_COLS

BF16 = jnp.bfloat16
F32 = jnp.float32
EXP_M_HALF = 0.6065306597126334


def _cparams(dims, vmem_bytes):
    limit = min(int(vmem_bytes), VMEM_BYTES_V7X - 4 * 1024 * 1024)
    return pltpu.CompilerParams(dimension_semantics=dims, vmem_limit_bytes=limit)


def _dot(a, b):
    return jnp.dot(a, b, preferred_element_type=F32)


def _dot_nt(a, b):
    return lax.dot_general(a, b, (((1,), (1,)), ((), ())), preferred_element_type=F32)


def _dot_tn(a, b):
    return lax.dot_general(a, b, (((0,), (0,)), ((), ())), preferred_element_type=F32)


def _sigmoid(x):
    return 1.0 / (1.0 + jnp.exp(-x))


def _softplus(x):
    return jnp.maximum(x, 0.0) + jnp.log1p(jnp.exp(-jnp.abs(x)))


def _rmsnorm_kernel(x_ref, g_ref, o_ref):
    x = x_ref[...]
    ms = jnp.mean(x * x, axis=-1, keepdims=True)
    o_ref[...] = (x * lax.rsqrt(ms + NORM_EPS) * g_ref[...]).astype(o_ref.dtype)


def rmsnorm(x, g, out_dtype, tm=256):
    m, d = x.shape
    tm = min(tm, m)
    vmem = 2 * tm * d * (4 + jnp.dtype(out_dtype).itemsize) + 4 * d * 4 + (4 << 20)
    return pl.pallas_call(
        _rmsnorm_kernel,
        grid=(m // tm,),
        in_specs=[pl.BlockSpec((tm, d), lambda i: (i, 0)), pl.BlockSpec((1, d), lambda i: (0, 0))],
        out_specs=pl.BlockSpec((tm, d), lambda i: (i, 0)),
        out_shape=jax.ShapeDtypeStruct((m, d), out_dtype),
        compiler_params=_cparams(("parallel",), vmem),
        name="rmsnorm",
    )(x, g.reshape(1, d))


def _mm_kernel(*refs, nk, epilogue, w_transposed, has_scale, has_gain, d_norm):
    refs = list(refs)
    x_ref, w_ref = refs[:2]
    pos = 2
    r_ref = ssq_in_ref = gain_ref = None
    if epilogue == "residual":
        r_ref = refs[pos]
        pos += 1
    if has_scale:
        ssq_in_ref = refs[pos]
        pos += 1
    if has_gain:
        gain_ref = refs[pos]
        pos += 1
    o_ref = refs[pos]
    pos += 1
    ob_ref = ssq_out_ref = None
    if has_gain:
        ob_ref, ssq_out_ref = refs[pos], refs[pos + 1]
        pos += 2
    rest = refs[pos:]
    j = pl.program_id(1)

    def finish(acc):
        if has_scale:
            acc = acc * lax.rsqrt(ssq_in_ref[:, 0:1] * (1.0 / d_norm) + NORM_EPS)
        if epilogue == "residual":
            acc = r_ref[...] + acc
        elif epilogue == "relu2":
            acc = jnp.square(jnp.maximum(acc, 0.0))
        o_ref[...] = acc.astype(o_ref.dtype)
        if has_gain:
            ob_ref[...] = (acc * gain_ref[...]).astype(ob_ref.dtype)
            part_ssq = jnp.broadcast_to(jnp.sum(acc * acc, axis=1, keepdims=True), ssq_out_ref.shape)

            @pl.when(j == 0)
            def _():
                ssq_out_ref[...] = part_ssq

            @pl.when(j > 0)
            def _():
                ssq_out_ref[...] += part_ssq

    wb = w_ref[...].astype(BF16)
    part = _dot_nt(x_ref[...], wb) if w_transposed else _dot(x_ref[...], wb)
    if nk == 1:
        finish(part)
        return
    acc_ref = rest[0]
    k = pl.program_id(2)

    @pl.when(k == 0)
    def _():
        acc_ref[...] = part

    @pl.when(jnp.logical_and(k > 0, k < nk - 1))
    def _():
        acc_ref[...] += part

    @pl.when(k == nk - 1)
    def _():
        finish(acc_ref[...] + part)


def matmul(x, w, layer, out_dtype, n=None, w_transposed=False, epilogue="none", residual=None,
           row_ssq=None, next_gain=None, tm=2048, tn=None, tk=None):
    m, kdim = x.shape
    n_all = w.shape[1] if w_transposed else w.shape[2]
    n = n_all if n is None else n
    if epilogue == "residual":
        tm = tm // 2
    tm = min(tm, m)
    if tk is None:
        tk = min(kdim, 4096)
    nk = kdim // tk
    if tn is None:
        tn = 256 if nk == 1 else 512
    tn = min(tn, n)
    assert m % tm == 0 and n % tn == 0 and kdim % tk == 0
    if w_transposed:
        w_spec = pl.BlockSpec((pl.Squeezed(), tn, tk), lambda i, j, k: (layer, j, k))
    else:
        w_spec = pl.BlockSpec((pl.Squeezed(), tk, tn), lambda i, j, k: (layer, k, j))
    in_specs = [pl.BlockSpec((tm, tk), lambda i, j, k: (i, k)), w_spec]
    args = [x, w]
    osz = jnp.dtype(out_dtype).itemsize
    wsz = jnp.dtype(w.dtype).itemsize
    vmem = 2 * (tm * tk * 2 + tk * tn * wsz + tm * tn * osz) + tk * tn * 2 + 2 * tm * tn * 4 + (8 << 20)
    if epilogue == "residual":
        in_specs.append(pl.BlockSpec((tm, tn), lambda i, j, k: (i, j)))
        args.append(residual)
        vmem += 2 * tm * tn * 4
    if row_ssq is not None:
        in_specs.append(pl.BlockSpec((tm, LANES), lambda i, j, k: (i, 0)))
        args.append(row_ssq)
        vmem += 2 * tm * LANES * 4
    out_specs = pl.BlockSpec((tm, tn), lambda i, j, k: (i, j))
    out_shape = jax.ShapeDtypeStruct((m, n), out_dtype)
    if next_gain is not None:
        in_specs.append(pl.BlockSpec((1, tn), lambda i, j, k: (0, j)))
        args.append(next_gain.reshape(1, n))
        out_specs = [out_specs, pl.BlockSpec((tm, tn), lambda i, j, k: (i, j)),
                     pl.BlockSpec((tm, LANES), lambda i, j, k: (i, 0))]
        out_shape = [out_shape, jax.ShapeDtypeStruct((m, n), BF16), jax.ShapeDtypeStruct((m, LANES), F32)]
        vmem += 2 * (tm * tn * 2 + tm * LANES * 4)
    scratch = [pltpu.VMEM((tm, tn), F32)] if nk > 1 else []
    return pl.pallas_call(
        functools.partial(_mm_kernel, nk=nk, epilogue=epilogue, w_transposed=w_transposed,
                          has_scale=row_ssq is not None, has_gain=next_gain is not None, d_norm=D_MODEL),
        grid=(m // tm, n // tn, nk),
        in_specs=in_specs,
        out_specs=out_specs,
        out_shape=out_shape,
        scratch_shapes=scratch,
        compiler_params=_cparams(("parallel", "arbitrary", "arbitrary"), vmem),
        name="matmul_" + epilogue,
    )(*args)


def _shift_rows(x, prev_ref, d):
    ext = jnp.concatenate([prev_ref[...], x], axis=0)
    return pltpu.roll(ext, d, 0)[SUBLANES:]


def _pair_sum(x, first):
    s1 = jnp.sum(jnp.where(first, x, 0.0), axis=1, keepdims=True)
    s2 = jnp.sum(jnp.where(first, 0.0, x), axis=1, keepdims=True)
    return jnp.where(first, s1, s2)


def _rwkv_kernel(r_ref, k_ref, v_ref, g_ref, lo_ref, cp_ref, mulo_ref, wup_ref, aup_ref, o_ref,
                 state_ref, pr_ref, pk_ref, pv_ref, pg_ref, plo_ref, *, tb):
    L = CHUNK
    i = pl.program_id(1)

    @pl.when(i == 0)
    def _():
        state_ref[...] = jnp.zeros_like(state_ref)
        pr_ref[...] = jnp.zeros_like(pr_ref)
        pk_ref[...] = jnp.zeros_like(pk_ref)
        pv_ref[...] = jnp.zeros_like(pv_ref)
        pg_ref[...] = jnp.zeros_like(pg_ref)
        plo_ref[...] = jnp.zeros_like(plo_ref)

    cp = cp_ref[...]
    mu_r, mu_k, mu_v, mu_g = cp[0:1], cp[1:2], cp[2:3], cp[3:4]
    w0, a0, k_k, k_a, r_k, ln_w, ln_b = (cp[4:5], cp[5:6], cp[6:7], cp[7:8], cp[8:9], cp[9:10], cp[10:11])

    def shift_lerp(x_ref, prev_ref, mu):
        x = x_ref[...]
        sh = _shift_rows(x, prev_ref, 1)
        prev_ref[...] = x[tb - SUBLANES:tb]
        return x + (sh - x) * mu

    r = shift_lerp(r_ref, pr_ref, mu_r)
    k = shift_lerp(k_ref, pk_ref, mu_k)
    v = shift_lerp(v_ref, pv_ref, mu_v)
    g = shift_lerp(g_ref, pg_ref, mu_g)
    lo = shift_lerp(lo_ref, plo_ref, mulo_ref[...])

    lane = lax.broadcasted_iota(jnp.int32, (1, LANES), 1)
    first = lane < A_HEAD
    m1 = first.astype(F32)
    m2 = 1.0 - m1

    w = w0 + _dot(jnp.tanh(lo).astype(BF16), wup_ref[...])
    logw = -_sigmoid(w) * EXP_M_HALF
    a = _sigmoid(a0 + _dot(lo.astype(BF16), aup_ref[...]))
    kk = k * k_k
    k = k * (1.0 + (a - 1.0) * k_a)
    kk = kk * lax.rsqrt(_pair_sum(kk * kk, first) + 1e-12)
    alpha = -kk
    beta = kk * a
    bonus = _pair_sum(r * k * r_k, first) * v

    nc = tb // L
    row = lax.broadcasted_iota(jnp.int32, (L, L), 0)
    col = lax.broadcasted_iota(jnp.int32, (L, L), 1)
    tril_strict = row > col
    row2 = lax.broadcasted_iota(jnp.int32, (L, LANES), 0)
    col2 = lax.broadcasted_iota(jnp.int32, (L, LANES), 1) % L
    tril_incl2 = row2 >= col2
    bd_r = lax.broadcasted_iota(jnp.int32, (LANES, LANES), 0) // A_HEAD
    bd_c = lax.broadcasted_iota(jnp.int32, (LANES, LANES), 1) // A_HEAD
    blockdiag = bd_r == bd_c

    pos = lax.broadcasted_iota(jnp.int32, (tb, 1), 0) % L
    cum = logw
    step = 1
    while step < L:
        cum = cum + jnp.where(pos >= step, pltpu.roll(cum, step, 0), 0.0)
        step *= 2
    e_pos = jnp.exp(cum)
    e_neg = jnp.exp(-cum)
    ab = alpha * jnp.exp(cum - logw)
    rb = r * e_pos
    kt = (k * e_neg).astype(BF16)
    bt = (beta * e_neg).astype(BF16)
    vv = v.astype(BF16)
    heads = ((m1, ab * m1, (rb * m1).astype(BF16)), (m2, ab * m2, (rb * m2).astype(BF16)))
    zero_top = jnp.zeros((L, LANES), BF16)

    sls = [slice(c * L, (c + 1) * L) for c in range(nc)]
    kb = [jnp.concatenate([kt[sl], bt[sl]], axis=0) for sl in sls]
    xa = []
    for c, sl in enumerate(sls):
        x = jnp.concatenate([heads[0][1][sl].astype(BF16), heads[1][1][sl].astype(BF16),
                             heads[0][2][sl], heads[1][2][sl]], axis=0)
        xa.append(_dot_nt(x, kb[c]))
    prob = [(c, h) for c in range(nc) for h in range(2)]
    n_pow, ys, a_r = [], [], []
    for c, h in prob:
        an = xa[c][h * L:(h + 1) * L]
        a_ak = jnp.where(tril_strict, an[:, :L], 0.0).astype(BF16)
        n_pow.append(jnp.where(tril_strict, an[:, L:], 0.0).astype(BF16))
        a_r.append(jnp.where(tril_incl2, xa[c][(2 + h) * L:(3 + h) * L], 0.0).astype(BF16))
        ys.append(jnp.concatenate([_dot(a_ak, vv[sls[c]]), heads[h][1][sls[c]]], axis=1))
    for lvl in range(6):
        ys = [y + _dot(n, y.astype(BF16)) for n, y in zip(n_pow, ys)]
        if lvl < 5:
            n_pow = [_dot(n, n).astype(BF16) for n in n_pow]
    q_c, o0_c, g_c, pb_c = [], [], [], []
    for c, sl in enumerate(sls):
        top = jnp.concatenate([vv[sl], zero_top], axis=1)
        y2 = []
        o0 = jnp.zeros((L, LANES), F32)
        qq = rb[sl]
        for h in range(2):
            mh = heads[h][0]
            y = ys[2 * c + h]
            y = jnp.concatenate([y[:, :LANES] * mh, y[:, LANES:]], axis=1)
            y2.append(y)
            out = _dot(a_r[2 * c + h], jnp.concatenate([top, y.astype(BF16)], axis=0))
            o0 = o0 + out[:, :LANES] * mh
            qq = qq + out[:, LANES:]
        ysum = (y2[0] + y2[1]).astype(BF16)
        gp = _dot_tn(jnp.concatenate([top, ysum], axis=0), kb[c])
        g_c.append(jnp.where(blockdiag, gp[:LANES], 0.0))
        pb_c.append(jnp.where(blockdiag, gp[LANES:], 0.0).astype(BF16))
        q_c.append(qq.astype(BF16))
        o0_c.append(o0)

    state = state_ref[...]
    outs = []
    for c in range(nc):
        st_b = state.astype(BF16)
        outs.append(o0_c[c] + _dot_nt(q_c[c], st_b))
        state = (state + _dot(st_b, pb_c[c]) + g_c[c]) * e_pos[(c + 1) * L - 1:(c + 1) * L]
    state_ref[...] = state
    o = jnp.concatenate(outs, axis=0) if len(outs) > 1 else outs[0]

    mean = _pair_sum(o, first) * (1.0 / A_HEAD)
    d = o - mean
    var = _pair_sum(d * d, first) * (1.0 / A_HEAD)
    y = d * lax.rsqrt(var + A_LN_EPS) * ln_w + ln_b
    o_ref[...] = ((y + bonus) * _sigmoid(g)).astype(o_ref.dtype)


def rwkv7_mix(p, mu, w0, w_up, a0, a_up, k_k, k_a, r_k, ln_w, ln_b, tb=1024):
    s = p.shape[0]
    tb = min(tb, s)
    assert s % tb == 0
    npair = A_H // 2
    nblk = A_W // LANES
    zeros = jnp.zeros((A_W,), F32)
    cp = jnp.stack([mu[0:A_W], mu[A_W:2 * A_W], mu[2 * A_W:3 * A_W], mu[3 * A_W:4 * A_W],
                    w0, a0, k_k, k_a, r_k, ln_w, ln_b, zeros, zeros, zeros, zeros, zeros])
    tail = 2 * LANES - 2 * A_LORA
    mulo = jnp.concatenate([mu[4 * A_W:], jnp.zeros((tail,), F32)]).reshape(1, 2 * LANES)
    wup = jnp.concatenate([w_up, jnp.zeros((2 * LANES - A_LORA, A_W), F32)], axis=0).astype(BF16)
    aup = jnp.concatenate([jnp.zeros((A_LORA, A_W), F32), a_up, jnp.zeros((tail, A_W), F32)], axis=0).astype(BF16)
    lora_blk = EV_LORA // (2 * LANES)

    def col(j):
        return pl.BlockSpec((tb, LANES), lambda hp, i, j=j: (i, j * nblk + hp))

    vmem = 2 * (4 * tb * LANES * 4 + tb * 2 * LANES * 4 + tb * LANES * 2) + (24 << 20)
    return pl.pallas_call(
        functools.partial(_rwkv_kernel, tb=tb),
        grid=(npair, s // tb),
        in_specs=[col(0), col(1), col(2), col(3),
                  pl.BlockSpec((tb, 2 * LANES), lambda hp, i: (i, lora_blk)),
                  pl.BlockSpec((16, LANES), lambda hp, i: (0, hp)),
                  pl.BlockSpec((1, 2 * LANES), lambda hp, i: (0, 0)),
                  pl.BlockSpec((2 * LANES, LANES), lambda hp, i: (0, hp)),
                  pl.BlockSpec((2 * LANES, LANES), lambda hp, i: (0, hp))],
        out_specs=pl.BlockSpec((tb, LANES), lambda hp, i: (i, hp)),
        out_shape=jax.ShapeDtypeStruct((s, MIX_W), BF16),
        scratch_shapes=[pltpu.VMEM((LANES, LANES), F32),
                        pltpu.VMEM((SUBLANES, LANES), F32), pltpu.VMEM((SUBLANES, LANES), F32),
                        pltpu.VMEM((SUBLANES, LANES), F32), pltpu.VMEM((SUBLANES, LANES), F32),
                        pltpu.VMEM((SUBLANES, 2 * LANES), F32)],
        compiler_params=_cparams(("parallel", "arbitrary"), vmem),
        name="rwkv7",
    )(p, p, p, p, p, cp, mulo, wup, aup)


def _mlstm_kernel(ib_ref, fb_ref, q_ref, k_ref, v_ref, og_ref, gt_ref, ng_ref, ybuf_ref, o_ref,
                  c_ref, n_ref, m_ref, *, tb):
    del ybuf_ref
    L = CHUNK
    h = pl.program_id(0)
    i = pl.program_id(1)

    @pl.when(i == 0)
    def _():
        c_ref[...] = jnp.zeros_like(c_ref)
        n_ref[...] = jnp.zeros_like(n_ref)
        m_ref[...] = jnp.zeros_like(m_ref)

    i_b = ib_ref[h]
    f_b = fb_ref[h]
    row = lax.broadcasted_iota(jnp.int32, (L, L), 0)
    col = lax.broadcasted_iota(jnp.int32, (L, L), 1)
    causal = row >= col
    eye = row == col

    def to_col(x_row):
        return jnp.sum(jnp.where(eye, jnp.broadcast_to(x_row, (L, L)), 0.0), axis=1, keepdims=True)

    pre = []
    for c in range(tb // L):
        sl = slice(c * L, (c + 1) * L)
        gi = gt_ref[c, pl.ds(h, 1), :]
        gf = gt_ref[c, pl.ds(B_H + h, 1), :]
        li_row = GATE_CAP * jnp.tanh((gi + i_b) / GATE_CAP)
        lf_row = -_softplus(-(GATE_CAP * jnp.tanh((gf + f_b) / GATE_CAP)))
        li_col = to_col(li_row)
        lf_col = to_col(lf_row)
        cum_col = jnp.sum(jnp.where(causal, jnp.broadcast_to(lf_row, (L, L)), 0.0), axis=1, keepdims=True)
        cum_row = jnp.sum(jnp.where(row <= col, jnp.broadcast_to(lf_col, (L, L)), 0.0), axis=0, keepdims=True)
        total = jnp.sum(lf_row, axis=1, keepdims=True)

        qf = q_ref[sl, :]
        q = qf.astype(BF16)
        kf = k_ref[sl, :] * (B_HEAD ** -0.5)
        vb = v_ref[sl, :].astype(BF16)

        d_log = jnp.where(causal, cum_col - cum_row + li_row, -jnp.inf)
        m_loc = jnp.max(d_log, axis=1, keepdims=True)
        s_loc = _dot_nt(q, kf.astype(BF16)) * jnp.exp(d_log - m_loc)
        sv = _dot(s_loc.astype(BF16), vb)
        s_sum = jnp.sum(s_loc, axis=1, keepdims=True)

        log_w_col = total - cum_col + li_col
        m_k = jnp.max(log_w_col, axis=0, keepdims=True)
        kw = kf * jnp.exp(log_w_col - m_k)
        kv = _dot_tn(kw.astype(BF16), vb)
        k_sum = jnp.sum(kw, axis=0, keepdims=True)
        pre.append((qf, q, cum_col, total, m_loc, sv, s_sum, m_k, kv, k_sum))

    c_st = c_ref[...]
    n_st = n_ref[...]
    m_st = m_ref[0:1, 0:1]
    outs = []
    for qf, q, cum_col, total, m_loc, sv, s_sum, m_k, kv, k_sum in pre:
        m_inter = cum_col + m_st
        m_row = jnp.maximum(m_loc, m_inter)
        w_loc = jnp.exp(m_loc - m_row)
        w_inter = jnp.exp(m_inter - m_row)
        num = w_loc * sv + w_inter * _dot(q, c_st.astype(BF16))
        den = w_loc * s_sum + w_inter * jnp.sum(qf * n_st, axis=1, keepdims=True)
        outs.append(num / jnp.maximum(jnp.abs(den), jnp.exp(-m_row)))

        m_new = jnp.maximum(total + m_st, m_k)
        carry = jnp.exp(total + m_st - m_new)
        w_new = jnp.exp(m_k - m_new)
        c_st = carry * c_st + w_new * kv
        n_st = carry * n_st + w_new * k_sum
        m_st = m_new
    c_ref[...] = c_st
    n_ref[...] = n_st
    m_ref[...] = jnp.broadcast_to(m_st, m_ref.shape)

    hh = jnp.concatenate(outs, axis=0) if len(outs) > 1 else outs[0]
    ms = jnp.mean(hh * hh, axis=1, keepdims=True)
    y = hh * lax.rsqrt(ms + NORM_EPS) * ng_ref[...]
    o_ref[...] = (y * _sigmoid(og_ref[...])).astype(o_ref.dtype)


def mlstm_mix(p, gates_t, i_b, f_b, norm_g, y_buf, tb=512):
    s = p.shape[0]
    tb = min(tb, s)
    assert s % tb == 0
    nblk = B_W // B_HEAD
    out0 = A_W // B_HEAD

    def col(j):
        return pl.BlockSpec((tb, B_HEAD), lambda h, i, j=j: (i, j * nblk + h))

    smem = pl.BlockSpec(memory_space=pltpu.SMEM)
    vmem = 2 * (4 * tb * B_HEAD * 4 + tb * B_HEAD * 2) + (24 << 20)
    return pl.pallas_call(
        functools.partial(_mlstm_kernel, tb=tb),
        grid=(B_H, s // tb),
        in_specs=[smem, smem, col(0), col(1), col(2), col(3),
                  pl.BlockSpec((tb // CHUNK, 2 * B_H, CHUNK), lambda h, i: (i, 0, 0)),
                  pl.BlockSpec((1, B_HEAD), lambda h, i: (0, h)),
                  pl.BlockSpec(memory_space=pl.ANY)],
        out_specs=pl.BlockSpec((tb, B_HEAD), lambda h, i: (i, out0 + h)),
        out_shape=jax.ShapeDtypeStruct(y_buf.shape, BF16),
        input_output_aliases={8: 0},
        scratch_shapes=[pltpu.VMEM((B_HEAD, B_HEAD), F32), pltpu.VMEM((1, B_HEAD), F32),
                        pltpu.VMEM((1, LANES), F32)],
        compiler_params=_cparams(("parallel", "arbitrary"), vmem),
        name="mlstm",
    )(i_b, f_b, p, p, p, p, gates_t, norm_g.reshape(1, B_W), y_buf)


def _retention_kernel(lg_ref, q_ref, k_ref, v_ref, g_ref, cc_ref, ss_ref, ng_ref, o_ref, r_ref, *, tb):
    L = CHUNK
    h = pl.program_id(0)
    i = pl.program_id(1)

    @pl.when(i == 0)
    def _():
        r_ref[...] = jnp.zeros_like(r_ref)

    lg = lg_ref[h]
    row = lax.broadcasted_iota(jnp.int32, (L, L), 0)
    col = lax.broadcasted_iota(jnp.int32, (L, L), 1)
    intra = jnp.exp(lg * jnp.abs(row - col).astype(F32))
    pos = lax.broadcasted_iota(jnp.int32, (L, 1), 0).astype(F32)
    q_decay = jnp.exp(lg * (pos + 1.0))
    k_decay = jnp.exp(lg * (L - 1.0 - pos))
    chunk_decay = jnp.exp(jnp.full((1, 1), lg * L, F32))

    cc = cc_ref[...]
    ss = ss_ref[...]

    def rope(t):
        return t * cc + pltpu.roll(t, C_KHEAD // 2, 1) * ss

    q = rope(q_ref[...]).astype(BF16)
    k = rope(k_ref[...]) * (C_KHEAD ** -0.5)
    vb = v_ref[...].astype(BF16)
    sls = [slice(c * L, (c + 1) * L) for c in range(tb // L)]
    sv = [_dot((_dot_nt(q[sl], k[sl].astype(BF16)) * intra).astype(BF16), vb[sl]) for sl in sls]
    kv = [_dot_tn((k[sl] * k_decay).astype(BF16), vb[sl]) for sl in sls]
    r_st = r_ref[...]
    outs = []
    for c, sl in enumerate(sls):
        outs.append(sv[c] + q_decay * _dot(q[sl], r_st.astype(BF16)))
        r_st = chunk_decay * r_st + kv[c]
    r_ref[...] = r_st
    o = jnp.concatenate(outs, axis=0) if len(outs) > 1 else outs[0]
    ms = jnp.mean(o * o, axis=1, keepdims=True)
    y = o * lax.rsqrt(ms + NORM_EPS) * ng_ref[...]
    g = g_ref[...]
    o_ref[...] = (y * (g * _sigmoid(g))).astype(o_ref.dtype)


def retention_mix(p, cc, ss, log_gamma, norm_g, tb=512):
    s = p.shape[0]
    tb = min(tb, s)
    assert s % tb == 0
    kb = C_KW // C_KHEAD
    vb = (2 * C_KW) // C_VHEAD
    gb = (2 * C_KW + C_VW) // C_VHEAD
    smem = pl.BlockSpec(memory_space=pltpu.SMEM)
    vmem = 2 * (2 * tb * C_KHEAD * 4 + 2 * tb * C_VHEAD * 4 + 2 * tb * C_KHEAD * 4 + tb * C_VHEAD * 2) + (24 << 20)
    return pl.pallas_call(
        functools.partial(_retention_kernel, tb=tb),
        grid=(C_H, s // tb),
        in_specs=[smem,
                  pl.BlockSpec((tb, C_KHEAD), lambda h, i: (i, h)),
                  pl.BlockSpec((tb, C_KHEAD), lambda h, i: (i, kb + h)),
                  pl.BlockSpec((tb, C_VHEAD), lambda h, i: (i, vb + h)),
                  pl.BlockSpec((tb, C_VHEAD), lambda h, i: (i, gb + h)),
                  pl.BlockSpec((tb, C_KHEAD), lambda h, i: (i, 0)),
                  pl.BlockSpec((tb, C_KHEAD), lambda h, i: (i, 0)),
                  pl.BlockSpec((1, C_VHEAD), lambda h, i: (0, h))],
        out_specs=pl.BlockSpec((tb, C_VHEAD), lambda h, i: (i, h)),
        out_shape=jax.ShapeDtypeStruct((s, MIX_W), BF16),
        scratch_shapes=[pltpu.VMEM((C_KHEAD, C_VHEAD), F32)],
        compiler_params=_cparams(("parallel", "arbitrary"), vmem),
        name="retention",
    )(log_gamma, p, p, p, p, cc, ss, norm_g.reshape(1, C_VW))


def _rglru_kernel(x_ref, gate_ref, cw_ref, cp_ref, wr_ref, wi_ref, ybuf_ref, o_ref, px_ref, h_ref, *, tb, ts):
    del ybuf_ref
    i = pl.program_id(1)

    @pl.when(i == 0)
    def _():
        px_ref[...] = jnp.zeros_like(px_ref)
        h_ref[...] = jnp.zeros_like(h_ref)

    cw = cw_ref[...]
    cp = cp_ref[...]
    conv_b, b_r, b_i, lam = cp[0:1], cp[1:2], cp[2:3], cp[3:4]
    x = x_ref[...]
    ext = jnp.concatenate([px_ref[...], x], axis=0)
    xc = conv_b + x * cw[CONV_W - 1:CONV_W]
    for d in range(1, CONV_W):
        xc = xc + pltpu.roll(ext, d, 0)[SUBLANES:] * cw[CONV_W - 1 - d:CONV_W - d]
    px_ref[...] = x[tb - SUBLANES:tb]

    xcb = xc.astype(BF16)
    r = _sigmoid(_dot(xcb, wr_ref[0]) + b_r)
    ig = _sigmoid(_dot(xcb, wi_ref[0]) + b_i)
    log_a = -LRU_C * r * _softplus(-lam)
    a = jnp.exp(log_a)
    u = jnp.sqrt(-jnp.tanh(log_a) * (a * a + 1.0)) * (ig * xc)

    rows = lax.broadcasted_iota(jnp.int32, (tb, 1), 0) % ts
    d = 1
    while d < ts:
        keep = rows >= d
        a_sh = jnp.where(keep, pltpu.roll(a, d, 0), 1.0)
        u_sh = jnp.where(keep, pltpu.roll(u, d, 0), 0.0)
        u = a * u_sh + u
        a = a * a_sh
        d *= 2
    h_prev = h_ref[0:1, :]
    parts = []
    for c in range(tb // ts):
        h_c = a[c * ts:(c + 1) * ts] * h_prev + u[c * ts:(c + 1) * ts]
        h_prev = h_c[ts - 1:ts]
        parts.append(h_c)
    hh = jnp.concatenate(parts, axis=0) if len(parts) > 1 else parts[0]
    h_ref[...] = jnp.broadcast_to(h_prev, h_ref.shape)

    gt = gate_ref[...]
    gelu = 0.5 * gt * (1.0 + jnp.tanh(0.7978845608028654 * (gt + 0.044715 * (gt * gt * gt))))
    o_ref[...] = (hh * gelu).astype(o_ref.dtype)


def rglru_mix(p, conv_w, conv_b, w_r, b_r, w_i, b_i, lam, y_buf, tb=1024, ts=256):
    s = p.shape[0]
    tb = min(tb, s)
    ts = min(ts, tb)
    assert s % tb == 0 and tb % ts == 0
    xb0 = (2 * C_KW + 2 * C_VW) // D_BW
    gt0 = xb0 + D_W // D_BW
    out0 = C_VW // D_BW
    zeros = jnp.zeros((D_W,), F32)
    cp = jnp.stack([conv_b, b_r, b_i, lam, zeros, zeros, zeros, zeros])
    vmem = 2 * (2 * tb * D_BW * 4 + tb * D_BW * 2 + 2 * D_BW * D_BW * 2) + (24 << 20)
    return pl.pallas_call(
        functools.partial(_rglru_kernel, tb=tb, ts=ts),
        grid=(D_BLOCKS, s // tb),
        in_specs=[pl.BlockSpec((tb, D_BW), lambda n, i: (i, xb0 + n)),
                  pl.BlockSpec((tb, D_BW), lambda n, i: (i, gt0 + n)),
                  pl.BlockSpec((CONV_W, D_BW), lambda n, i: (0, n)),
                  pl.BlockSpec((SUBLANES, D_BW), lambda n, i: (0, n)),
                  pl.BlockSpec((1, D_BW, D_BW), lambda n, i: (n, 0, 0)),
                  pl.BlockSpec((1, D_BW, D_BW), lambda n, i: (n, 0, 0)),
                  pl.BlockSpec(memory_space=pl.ANY)],
        out_specs=pl.BlockSpec((tb, D_BW), lambda n, i: (i, out0 + n)),
        out_shape=jax.ShapeDtypeStruct(y_buf.shape, BF16),
        input_output_aliases={6: 0},
        scratch_shapes=[pltpu.VMEM((SUBLANES, D_BW), F32), pltpu.VMEM((SUBLANES, D_BW), F32)],
        compiler_params=_cparams(("parallel", "arbitrary"), vmem),
        name="rglru",
    )(p, p, conv_w, cp, w_r.astype(BF16), w_i.astype(BF16), y_buf)


def _xattn_kernel(q_ref, k_ref, v_ref, o_ref):
    for hd in range(XA_H):
        sl = slice(hd * XA_HEAD, (hd + 1) * XA_HEAD)
        sc = _dot_nt(q_ref[:, sl], k_ref[:, sl]) * (XA_HEAD ** -0.5)
        sc = sc - jnp.max(sc, axis=1, keepdims=True)
        e = jnp.exp(sc)
        pr = e / jnp.sum(e, axis=1, keepdims=True)
        o_ref[:, sl] = _dot(pr.astype(BF16), v_ref[:, sl]).astype(o_ref.dtype)


def xattn_core(q, kmem, vmem_, tm=512):
    s = q.shape[0]
    nm = kmem.shape[0]
    vmem = 2 * (2 * tm * XA_W * 2 + 2 * nm * XA_W * 2) + (16 << 20)
    return pl.pallas_call(
        _xattn_kernel,
        grid=(s // tm,),
        in_specs=[pl.BlockSpec((tm, XA_W), lambda i: (i, 0)),
                  pl.BlockSpec((nm, XA_W), lambda i: (0, 0)),
                  pl.BlockSpec((nm, XA_W), lambda i: (0, 0))],
        out_specs=pl.BlockSpec((tm, XA_W), lambda i: (i, 0)),
        out_shape=jax.ShapeDtypeStruct((s, XA_W), BF16),
        compiler_params=_cparams(("parallel",), vmem),
        name="xattn",
    )(q, kmem, vmem_)


def kernel(x, mem, mem_norm_g, norm_mix_g, norm_xattn_g, norm_mlp_g, xattn_wq, xattn_wk, xattn_wv, xattn_wo, mlp_w1, mlp_w2, even_w_in, even_w_out, rwkv_mu, rwkv_w0, rwkv_w_up, rwkv_a0, rwkv_a_up, rwkv_k_k, rwkv_k_a, rwkv_r_k, rwkv_ln_w, rwkv_ln_b, mlstm_i_b, mlstm_f_b, mlstm_norm_g, odd_w_in, odd_w_out, ret_norm_g, lru_conv_w, lru_conv_b, lru_w_r, lru_b_r, lru_w_i, lru_b_i, lru_lambda, final_norm_g):
    b, s, d = x.shape
    assert b == 1 and d == D_MODEL and s % 256 == 0
    h = x.reshape(s, d)
    mem_n = rmsnorm(mem.reshape(-1, d), mem_norm_g, BF16)
    even_wt = jnp.swapaxes(even_w_in, 1, 2)
    even_wt_b = jnp.pad(even_wt[:, A_COLS:, :], ((0, 0), (0, EV_B_COLS - B_COLS), (0, 0)))

    positions = jnp.arange(s, dtype=F32)
    inv_freq = ROPE_BASE ** (-jnp.arange(0, C_KHEAD, 2, dtype=F32) / C_KHEAD)
    ang = positions[:, None] * inv_freq[None, :]
    cos, sin = jnp.cos(ang), jnp.sin(ang)
    rope_cc = jnp.concatenate([cos, cos], axis=1)
    rope_ss = jnp.concatenate([-sin, sin], axis=1)
    log_gamma = jnp.log1p(-jnp.exp2(-5.0 - jnp.arange(C_H, dtype=F32)))

    hb, ssq = rmsnorm(h, norm_mix_g[0], BF16), None
    for layer in range(DEPTH):
        j = layer // 2
        if layer % 2 == 0:
            p_a = matmul(hb, even_wt, j, F32, n=EV_A_COLS, w_transposed=True, row_ssq=ssq)
            p_b = matmul(hb, even_wt_b, j, F32, w_transposed=True, row_ssq=ssq)
            y = rwkv7_mix(p_a, rwkv_mu[j], rwkv_w0[j], rwkv_w_up[j], rwkv_a0[j], rwkv_a_up[j],
                          rwkv_k_k[j], rwkv_k_a[j], rwkv_r_k[j], rwkv_ln_w[j], rwkv_ln_b[j])
            gates = p_b[:, EV_GATES:EV_GATES + 2 * B_H]
            gates_t = gates.reshape(s // CHUNK, CHUNK, 2 * B_H).transpose(0, 2, 1)
            y = mlstm_mix(p_b, gates_t, mlstm_i_b[j], mlstm_f_b[j], mlstm_norm_g[j], y)
            w_out = even_w_out
        else:
            p = matmul(hb, odd_w_in, j, F32, row_ssq=ssq)
            y = retention_mix(p, rope_cc, rope_ss, log_gamma, ret_norm_g[j])
            y = rglru_mix(p, lru_conv_w[j], lru_conv_b[j], lru_w_r[j], lru_b_r[j],
                          lru_w_i[j], lru_b_i[j], lru_lambda[j], y)
            w_out = odd_w_out
        h, hb, ssq = matmul(y, w_out, j, F32, epilogue="residual", residual=h, next_gain=norm_xattn_g[layer])

        q = matmul(hb, xattn_wq, layer, BF16, row_ssq=ssq)
        kmem = matmul(mem_n, xattn_wk, layer, BF16)
        vmem_ = matmul(mem_n, xattn_wv, layer, BF16)
        o = xattn_core(q, kmem, vmem_)
        h, hb, ssq = matmul(o, xattn_wo, layer, F32, epilogue="residual", residual=h, next_gain=norm_mlp_g[layer])

        u = matmul(hb, mlp_w1, layer, BF16, epilogue="relu2", row_ssq=ssq)
        if layer + 1 < DEPTH:
            h, hb, ssq = matmul(u, mlp_w2, layer, F32, epilogue="residual", residual=h,
                                next_gain=norm_mix_g[layer + 1])
        else:
            h = matmul(u, mlp_w2, layer, F32, epilogue="residual", residual=h)

    return rmsnorm(h, final_norm_g, F32).reshape(b, s, d)
```

```python
import functools

import jax
import jax.numpy as jnp
from jax import lax
from jax.experimental import pallas as pl
from jax.experimental.pallas import tpu as pltpu

D_MODEL = 4096
DEPTH = 4
CHUNK = 64
NORM_EPS = 1e-6
MIX_W = D_MODEL
A_W = MIX_W // 2
A_HEAD = 64
A_H = A_W // A_HEAD
A_LORA = 96
A_LN_EPS = 64e-5
B_W = MIX_W // 2
B_H = 8
B_HEAD = B_W // B_H
GATE_CAP = 15.0
C_VW = MIX_W // 2
C_H = 8
C_VHEAD = C_VW // C_H
C_KHEAD = C_VHEAD // 2
C_KW = C_H * C_KHEAD
ROPE_BASE = 10000.0
D_W = MIX_W // 2
D_BLOCKS = 16
D_BW = D_W // D_BLOCKS
CONV_W = 4
LRU_C = 8.0
D_FF = 4 * D_MODEL
XA_H = 4
XA_HEAD = 256
XA_W = XA_H * XA_HEAD
A_COLS = 4 * A_W + 2 * A_LORA
B_COLS = 4 * B_W + 2 * B_H

LANES = 128
SUBLANES = 8
VMEM_BYTES_V7X = 64 * 1024 * 1024

EV_LORA = 4 * A_W
EV_A_COLS = EV_LORA + 4 * LANES
EV_GATES = 4 * B_W
assert A_COLS <= EV_LORA + 2 * LANES

BF16 = jnp.bfloat16
F32 = jnp.float32
EXP_M_HALF = 0.6065306597126334


def _cparams(dims, vmem_bytes):
    limit = min(int(vmem_bytes), VMEM_BYTES_V7X - 4 * 1024 * 1024)
    return pltpu.CompilerParams(dimension_semantics=dims, vmem_limit_bytes=limit)


def _dot(a, b):
    return jnp.dot(a, b, preferred_element_type=F32)


def _dot_nt(a, b):
    return lax.dot_general(a, b, (((1,), (1,)), ((), ())), preferred_element_type=F32)


def _dot_tn(a, b):
    return lax.dot_general(a, b, (((0,), (0,)), ((), ())), preferred_element_type=F32)


def _sigmoid(x):
    return 1.0 / (1.0 + jnp.exp(-x))


def _softplus(x):
    return jnp.maximum(x, 0.0) + jnp.log1p(jnp.exp(-jnp.abs(x)))


def _rmsnorm_kernel(x_ref, g_ref, o_ref):
    x = x_ref[...]
    ms = jnp.mean(x * x, axis=-1, keepdims=True)
    o_ref[...] = (x * lax.rsqrt(ms + NORM_EPS) * g_ref[...]).astype(o_ref.dtype)


def rmsnorm(x, g, out_dtype, tm=256):
    m, d = x.shape
    tm = min(tm, m)
    vmem = 2 * tm * d * (4 + jnp.dtype(out_dtype).itemsize) + 4 * d * 4 + (4 << 20)
    return pl.pallas_call(
        _rmsnorm_kernel,
        grid=(m // tm,),
        in_specs=[pl.BlockSpec((tm, d), lambda i: (i, 0)), pl.BlockSpec((1, d), lambda i: (0, 0))],
        out_specs=pl.BlockSpec((tm, d), lambda i: (i, 0)),
        out_shape=jax.ShapeDtypeStruct((m, d), out_dtype),
        compiler_params=_cparams(("parallel",), vmem),
        name="rmsnorm",
    )(x, g.reshape(1, d))


def _mm_kernel(*refs, nk, epilogue, w_transposed):
    if epilogue == "residual":
        x_ref, w_ref, r_ref, o_ref = refs[:4]
        rest = refs[4:]
    else:
        x_ref, w_ref, o_ref = refs[:3]
        r_ref = None
        rest = refs[3:]

    def finish(acc):
        if epilogue == "residual":
            acc = r_ref[...] + acc
        elif epilogue == "relu2":
            acc = jnp.square(jnp.maximum(acc, 0.0))
        o_ref[...] = acc.astype(o_ref.dtype)

    if w_transposed:
        part = _dot_nt(x_ref[...], w_ref[0].astype(BF16))
    else:
        part = _dot(x_ref[...], w_ref[...].astype(BF16))
    if nk == 1:
        finish(part)
        return
    acc_ref = rest[0]
    k = pl.program_id(2)

    @pl.when(k == 0)
    def _():
        acc_ref[...] = part

    @pl.when(jnp.logical_and(k > 0, k < nk - 1))
    def _():
        acc_ref[...] += part

    @pl.when(k == nk - 1)
    def _():
        finish(acc_ref[...] + part)


def matmul(x, w, layer, out_dtype, n=None, w_transposed=False, row0=0, epilogue="none", residual=None,
           tm=2048, tn=None, tk=None):
    m, kdim = x.shape
    n_all = w.shape[1] if w_transposed else w.shape[2]
    n = n_all if n is None else n
    tm = min(tm, m)
    if tk is None:
        tk = kdim if kdim <= 4096 else 2048
    nk = kdim // tk
    single_x = nk == 1 and epilogue != "residual" and m > tm
    if tn is None:
        tn = 256 if (nk == 1 and not single_x) else 512
    tn = min(tn, n)
    assert m % tm == 0 and n % tn == 0 and kdim % tk == 0 and row0 + n <= n_all
    if w_transposed:
        w_spec = pl.BlockSpec((pl.Element(1), pl.Element(tn), pl.Element(tk)),
                              lambda i, j, k: (layer, pl.multiple_of(row0 + j * tn, SUBLANES),
                                               pl.multiple_of(k * tk, LANES)))
    else:
        assert row0 == 0
        w_spec = pl.BlockSpec((pl.Squeezed(), tk, tn), lambda i, j, k: (layer, k, j))
    if single_x:
        x_spec = pl.BlockSpec((tm, tk), lambda i, j, k: (i, k), pipeline_mode=pl.Buffered(1))
    else:
        x_spec = pl.BlockSpec((tm, tk), lambda i, j, k: (i, k))
    in_specs = [x_spec, w_spec]
    args = [x, w]
    osz = jnp.dtype(out_dtype).itemsize
    wsz = jnp.dtype(w.dtype).itemsize
    vmem = ((1 if single_x else 2) * tm * tk * 2 + 2 * (tk * tn * wsz + tm * tn * osz)
            + tk * tn * 2 + 2 * tm * tn * 4 + (8 << 20))
    if epilogue == "residual":
        in_specs.append(pl.BlockSpec((tm, tn), lambda i, j, k: (i, j)))
        args.append(residual)
        vmem += 2 * tm * tn * 4
    scratch = [pltpu.VMEM((tm, tn), F32)] if nk > 1 else []
    return pl.pallas_call(
        functools.partial(_mm_kernel, nk=nk, epilogue=epilogue, w_transposed=w_transposed),
        grid=(m // tm, n // tn, nk),
        in_specs=in_specs,
        out_specs=pl.BlockSpec((tm, tn), lambda i, j, k: (i, j)),
        out_shape=jax.ShapeDtypeStruct((m, n), out_dtype),
        scratch_shapes=scratch,
        compiler_params=_cparams(("parallel", "arbitrary", "arbitrary"), vmem),
        name="matmul_" + epilogue,
    )(*args)


def _shift_rows(x, prev_ref, d):
    ext = jnp.concatenate([prev_ref[...], x], axis=0)
    return pltpu.roll(ext, d, 0)[SUBLANES:]


def _pair_sum(x, first):
    s1 = jnp.sum(jnp.where(first, x, 0.0), axis=1, keepdims=True)
    s2 = jnp.sum(jnp.where(first, 0.0, x), axis=1, keepdims=True)
    return jnp.where(first, s1, s2)


def _rwkv_kernel(r_ref, k_ref, v_ref, g_ref, lo_ref, cp_ref, mulo_ref, wup_ref, aup_ref, o_ref,
                 state_ref, pr_ref, pk_ref, pv_ref, pg_ref, plo_ref, *, tb):
    L = CHUNK
    i = pl.program_id(1)

    @pl.when(i == 0)
    def _():
        state_ref[...] = jnp.zeros_like(state_ref)
        pr_ref[...] = jnp.zeros_like(pr_ref)
        pk_ref[...] = jnp.zeros_like(pk_ref)
        pv_ref[...] = jnp.zeros_like(pv_ref)
        pg_ref[...] = jnp.zeros_like(pg_ref)
        plo_ref[...] = jnp.zeros_like(plo_ref)

    cp = cp_ref[...]
    mu_r, mu_k, mu_v, mu_g = cp[0:1], cp[1:2], cp[2:3], cp[3:4]
    w0, a0, k_k, k_a, r_k, ln_w, ln_b = (cp[4:5], cp[5:6], cp[6:7], cp[7:8], cp[8:9], cp[9:10], cp[10:11])

    def shift_lerp(x_ref, prev_ref, mu):
        x = x_ref[...]
        sh = _shift_rows(x, prev_ref, 1)
        prev_ref[...] = x[tb - SUBLANES:tb]
        return x + (sh - x) * mu

    r = shift_lerp(r_ref, pr_ref, mu_r)
    k = shift_lerp(k_ref, pk_ref, mu_k)
    v = shift_lerp(v_ref, pv_ref, mu_v)
    g = shift_lerp(g_ref, pg_ref, mu_g)
    lo = shift_lerp(lo_ref, plo_ref, mulo_ref[...])

    lane = lax.broadcasted_iota(jnp.int32, (1, LANES), 1)
    first = lane < A_HEAD
    m1 = first.astype(F32)
    m2 = 1.0 - m1

    w = w0 + _dot(jnp.tanh(lo).astype(BF16), wup_ref[...])
    logw = -_sigmoid(w) * EXP_M_HALF
    a = _sigmoid(a0 + _dot(lo.astype(BF16), aup_ref[...]))
    kk = k * k_k
    k = k * (1.0 + (a - 1.0) * k_a)
    kk = kk * lax.rsqrt(_pair_sum(kk * kk, first) + 1e-12)
    alpha = -kk
    beta = kk * a
    bonus = _pair_sum(r * k * r_k, first) * v

    nc = tb // L
    row = lax.broadcasted_iota(jnp.int32, (L, L), 0)
    col = lax.broadcasted_iota(jnp.int32, (L, L), 1)
    tril_strict = row > col
    row2 = lax.broadcasted_iota(jnp.int32, (L, LANES), 0)
    col2 = lax.broadcasted_iota(jnp.int32, (L, LANES), 1) % L
    tril_incl2 = row2 >= col2
    bd_r = lax.broadcasted_iota(jnp.int32, (LANES, LANES), 0) // A_HEAD
    bd_c = lax.broadcasted_iota(jnp.int32, (LANES, LANES), 1) // A_HEAD
    blockdiag = bd_r == bd_c

    pos = lax.broadcasted_iota(jnp.int32, (tb, 1), 0) % L
    cum = logw
    step = 1
    while step < L:
        cum = cum + jnp.where(pos >= step, pltpu.roll(cum, step, 0), 0.0)
        step *= 2
    e_pos = jnp.exp(cum)
    e_neg = jnp.exp(-cum)
    ab = alpha * jnp.exp(cum - logw)
    rb = r * e_pos
    kt = (k * e_neg).astype(BF16)
    bt = (beta * e_neg).astype(BF16)
    vv = v.astype(BF16)
    heads = ((m1, ab * m1, (rb * m1).astype(BF16)), (m2, ab * m2, (rb * m2).astype(BF16)))
    zero_top = jnp.zeros((L, LANES), BF16)

    sls = [slice(c * L, (c + 1) * L) for c in range(nc)]
    kb = [jnp.concatenate([kt[sl], bt[sl]], axis=0) for sl in sls]
    xa = []
    for c, sl in enumerate(sls):
        x = jnp.concatenate([heads[0][1][sl].astype(BF16), heads[1][1][sl].astype(BF16),
                             heads[0][2][sl], heads[1][2][sl]], axis=0)
        xa.append(_dot_nt(x, kb[c]))
    prob = [(c, h) for c in range(nc) for h in range(2)]
    n_pow, ys, a_r = [], [], []
    for c, h in prob:
        an = xa[c][h * L:(h + 1) * L]
        a_ak = jnp.where(tril_strict, an[:, :L], 0.0).astype(BF16)
        n_pow.append(jnp.where(tril_strict, an[:, L:], 0.0).astype(BF16))
        a_r.append(jnp.where(tril_incl2, xa[c][(2 + h) * L:(3 + h) * L], 0.0).astype(BF16))
        ys.append(jnp.concatenate([_dot(a_ak, vv[sls[c]]), heads[h][1][sls[c]]], axis=1))
    for lvl in range(6):
        ys = [y + _dot(n, y.astype(BF16)) for n, y in zip(n_pow, ys)]
        if lvl < 5:
            n_pow = [_dot(n, n).astype(BF16) for n in n_pow]
    q_c, o0_c, g_c, pb_c = [], [], [], []
    for c, sl in enumerate(sls):
        top = jnp.concatenate([vv[sl], zero_top], axis=1)
        y2 = []
        o0 = jnp.zeros((L, LANES), F32)
        qq = rb[sl]
        for h in range(2):
            mh = heads[h][0]
            y = ys[2 * c + h]
            y = jnp.concatenate([y[:, :LANES] * mh, y[:, LANES:]], axis=1)
            y2.append(y)
            out = _dot(a_r[2 * c + h], jnp.concatenate([top, y.astype(BF16)], axis=0))
            o0 = o0 + out[:, :LANES] * mh
            qq = qq + out[:, LANES:]
        ysum = (y2[0] + y2[1]).astype(BF16)
        gp = _dot_tn(jnp.concatenate([top, ysum], axis=0), kb[c])
        g_c.append(jnp.where(blockdiag, gp[:LANES], 0.0))
        pb_c.append(jnp.where(blockdiag, gp[LANES:], 0.0).astype(BF16))
        q_c.append(qq.astype(BF16))
        o0_c.append(o0)

    state = state_ref[...]
    outs = []
    for c in range(nc):
        st_b = state.astype(BF16)
        outs.append(o0_c[c] + _dot_nt(q_c[c], st_b))
        state = (state + _dot(st_b, pb_c[c]) + g_c[c]) * e_pos[(c + 1) * L - 1:(c + 1) * L]
    state_ref[...] = state
    o = jnp.concatenate(outs, axis=0) if len(outs) > 1 else outs[0]

    mean = _pair_sum(o, first) * (1.0 / A_HEAD)
    d = o - mean
    var = _pair_sum(d * d, first) * (1.0 / A_HEAD)
    y = d * lax.rsqrt(var + A_LN_EPS) * ln_w + ln_b
    o_ref[...] = ((y + bonus) * _sigmoid(g)).astype(o_ref.dtype)


def rwkv7_mix(p, mu, w0, w_up, a0, a_up, k_k, k_a, r_k, ln_w, ln_b, tb=1024):
    s = p.shape[0]
    tb = min(tb, s)
    assert s % tb == 0
    npair = A_H // 2
    nblk = A_W // LANES
    zeros = jnp.zeros((A_W,), F32)
    cp = jnp.stack([mu[0:A_W], mu[A_W:2 * A_W], mu[2 * A_W:3 * A_W], mu[3 * A_W:4 * A_W],
                    w0, a0, k_k, k_a, r_k, ln_w, ln_b, zeros, zeros, zeros, zeros, zeros])
    tail = 2 * LANES - 2 * A_LORA
    mulo = jnp.concatenate([mu[4 * A_W:], jnp.zeros((tail,), F32)]).reshape(1, 2 * LANES)
    wup = jnp.concatenate([w_up, jnp.zeros((2 * LANES - A_LORA, A_W), F32)], axis=0).astype(BF16)
    aup = jnp.concatenate([jnp.zeros((A_LORA, A_W), F32), a_up, jnp.zeros((tail, A_W), F32)], axis=0).astype(BF16)
    lora_blk = EV_LORA // (2 * LANES)

    def col(j):
        return pl.BlockSpec((tb, LANES), lambda hp, i, j=j: (i, j * nblk + hp))

    vmem = 2 * (4 * tb * LANES * 4 + tb * 2 * LANES * 4 + tb * LANES * 2) + (24 << 20)
    return pl.pallas_call(
        functools.partial(_rwkv_kernel, tb=tb),
        grid=(npair, s // tb),
        in_specs=[col(0), col(1), col(2), col(3),
                  pl.BlockSpec((tb, 2 * LANES), lambda hp, i: (i, lora_blk)),
                  pl.BlockSpec((16, LANES), lambda hp, i: (0, hp)),
                  pl.BlockSpec((1, 2 * LANES), lambda hp, i: (0, 0)),
                  pl.BlockSpec((2 * LANES, LANES), lambda hp, i: (0, hp)),
                  pl.BlockSpec((2 * LANES, LANES), lambda hp, i: (0, hp))],
        out_specs=pl.BlockSpec((tb, LANES), lambda hp, i: (i, hp)),
        out_shape=jax.ShapeDtypeStruct((s, MIX_W), BF16),
        scratch_shapes=[pltpu.VMEM((LANES, LANES), F32),
                        pltpu.VMEM((SUBLANES, LANES), F32), pltpu.VMEM((SUBLANES, LANES), F32),
                        pltpu.VMEM((SUBLANES, LANES), F32), pltpu.VMEM((SUBLANES, LANES), F32),
                        pltpu.VMEM((SUBLANES, 2 * LANES), F32)],
        compiler_params=_cparams(("parallel", "arbitrary"), vmem),
        name="rwkv7",
    )(p, p, p, p, p, cp, mulo, wup, aup)


def _mlstm_kernel(ib_ref, fb_ref, q_ref, k_ref, v_ref, og_ref, gt_ref, ng_ref, ybuf_ref, o_ref,
                  c_ref, n_ref, m_ref, *, tb):
    del ybuf_ref
    L = CHUNK
    h = pl.program_id(0)
    i = pl.program_id(1)

    @pl.when(i == 0)
    def _():
        c_ref[...] = jnp.zeros_like(c_ref)
        n_ref[...] = jnp.zeros_like(n_ref)
        m_ref[...] = jnp.zeros_like(m_ref)

    i_b = ib_ref[h]
    f_b = fb_ref[h]
    row = lax.broadcasted_iota(jnp.int32, (L, L), 0)
    col = lax.broadcasted_iota(jnp.int32, (L, L), 1)
    causal = row >= col
    eye = row == col

    def to_col(x_row):
        return jnp.sum(jnp.where(eye, jnp.broadcast_to(x_row, (L, L)), 0.0), axis=1, keepdims=True)

    pre = []
    for c in range(tb // L):
        sl = slice(c * L, (c + 1) * L)
        gi = gt_ref[c, pl.ds(h, 1), :]
        gf = gt_ref[c, pl.ds(B_H + h, 1), :]
        li_row = GATE_CAP * jnp.tanh((gi + i_b) / GATE_CAP)
        lf_row = -_softplus(-(GATE_CAP * jnp.tanh((gf + f_b) / GATE_CAP)))
        li_col = to_col(li_row)
        lf_col = to_col(lf_row)
        cum_col = jnp.sum(jnp.where(causal, jnp.broadcast_to(lf_row, (L, L)), 0.0), axis=1, keepdims=True)
        cum_row = jnp.sum(jnp.where(row <= col, jnp.broadcast_to(lf_col, (L, L)), 0.0), axis=0, keepdims=True)
        total = jnp.sum(lf_row, axis=1, keepdims=True)

        qf = q_ref[sl, :]
        q = qf.astype(BF16)
        kf = k_ref[sl, :] * (B_HEAD ** -0.5)
        vb = v_ref[sl, :].astype(BF16)

        d_log = jnp.where(causal, cum_col - cum_row + li_row, -jnp.inf)
        m_loc = jnp.max(d_log, axis=1, keepdims=True)
        s_loc = _dot_nt(q, kf.astype(BF16)) * jnp.exp(d_log - m_loc)
        sv = _dot(s_loc.astype(BF16), vb)
        s_sum = jnp.sum(s_loc, axis=1, keepdims=True)

        log_w_col = total - cum_col + li_col
        m_k = jnp.max(log_w_col, axis=0, keepdims=True)
        kw = kf * jnp.exp(log_w_col - m_k)
        kv = _dot_tn(kw.astype(BF16), vb)
        k_sum = jnp.sum(kw, axis=0, keepdims=True)
        pre.append((qf, q, cum_col, total, m_loc, sv, s_sum, m_k, kv, k_sum))

    c_st = c_ref[...]
    n_st = n_ref[...]
    m_st = m_ref[0:1, 0:1]
    outs = []
    for qf, q, cum_col, total, m_loc, sv, s_sum, m_k, kv, k_sum in pre:
        m_inter = cum_col + m_st
        m_row = jnp.maximum(m_loc, m_inter)
        w_loc = jnp.exp(m_loc - m_row)
        w_inter = jnp.exp(m_inter - m_row)
        num = w_loc * sv + w_inter * _dot(q, c_st.astype(BF16))
        den = w_loc * s_sum + w_inter * jnp.sum(qf * n_st, axis=1, keepdims=True)
        outs.append(num / jnp.maximum(jnp.abs(den), jnp.exp(-m_row)))

        m_new = jnp.maximum(total + m_st, m_k)
        carry = jnp.exp(total + m_st - m_new)
        w_new = jnp.exp(m_k - m_new)
        c_st = carry * c_st + w_new * kv
        n_st = carry * n_st + w_new * k_sum
        m_st = m_new
    c_ref[...] = c_st
    n_ref[...] = n_st
    m_ref[...] = jnp.broadcast_to(m_st, m_ref.shape)

    hh = jnp.concatenate(outs, axis=0) if len(outs) > 1 else outs[0]
    ms = jnp.mean(hh * hh, axis=1, keepdims=True)
    y = hh * lax.rsqrt(ms + NORM_EPS) * ng_ref[...]
    o_ref[...] = (y * _sigmoid(og_ref[...])).astype(o_ref.dtype)


def mlstm_mix(p, gates_t, i_b, f_b, norm_g, y_buf, tb=512):
    s = p.shape[0]
    tb = min(tb, s)
    assert s % tb == 0
    nblk = B_W // B_HEAD
    out0 = A_W // B_HEAD

    def col(j):
        return pl.BlockSpec((tb, B_HEAD), lambda h, i, j=j: (i, j * nblk + h))

    smem = pl.BlockSpec(memory_space=pltpu.SMEM)
    vmem = 2 * (4 * tb * B_HEAD * 4 + tb * B_HEAD * 2) + (24 << 20)
    return pl.pallas_call(
        functools.partial(_mlstm_kernel, tb=tb),
        grid=(B_H, s // tb),
        in_specs=[smem, smem, col(0), col(1), col(2), col(3),
                  pl.BlockSpec((tb // CHUNK, 2 * B_H, CHUNK), lambda h, i: (i, 0, 0)),
                  pl.BlockSpec((1, B_HEAD), lambda h, i: (0, h)),
                  pl.BlockSpec(memory_space=pl.ANY)],
        out_specs=pl.BlockSpec((tb, B_HEAD), lambda h, i: (i, out0 + h)),
        out_shape=jax.ShapeDtypeStruct(y_buf.shape, BF16),
        input_output_aliases={8: 0},
        scratch_shapes=[pltpu.VMEM((B_HEAD, B_HEAD), F32), pltpu.VMEM((1, B_HEAD), F32),
                        pltpu.VMEM((1, LANES), F32)],
        compiler_params=_cparams(("parallel", "arbitrary"), vmem),
        name="mlstm",
    )(i_b, f_b, p, p, p, p, gates_t, norm_g.reshape(1, B_W), y_buf)


def _retention_kernel(lg_ref, q_ref, k_ref, v_ref, g_ref, cc_ref, ss_ref, ng_ref, o_ref, r_ref, *, tb):
    L = CHUNK
    h = pl.program_id(0)
    i = pl.program_id(1)

    @pl.when(i == 0)
    def _():
        r_ref[...] = jnp.zeros_like(r_ref)

    lg = lg_ref[h]
    row = lax.broadcasted_iota(jnp.int32, (L, L), 0)
    col = lax.broadcasted_iota(jnp.int32, (L, L), 1)
    intra = jnp.exp(lg * jnp.abs(row - col).astype(F32))
    pos = lax.broadcasted_iota(jnp.int32, (L, 1), 0).astype(F32)
    q_decay = jnp.exp(lg * (pos + 1.0))
    k_decay = jnp.exp(lg * (L - 1.0 - pos))
    chunk_decay = jnp.exp(jnp.full((1, 1), lg * L, F32))

    cc = cc_ref[...]
    ss = ss_ref[...]

    def rope(t):
        return t * cc + pltpu.roll(t, C_KHEAD // 2, 1) * ss

    q = rope(q_ref[...]).astype(BF16)
    k = rope(k_ref[...]) * (C_KHEAD ** -0.5)
    vb = v_ref[...].astype(BF16)
    sls = [slice(c * L, (c + 1) * L) for c in range(tb // L)]
    sv = [_dot((_dot_nt(q[sl], k[sl].astype(BF16)) * intra).astype(BF16), vb[sl]) for sl in sls]
    kv = [_dot_tn((k[sl] * k_decay).astype(BF16), vb[sl]) for sl in sls]
    r_st = r_ref[...]
    outs = []
    for c, sl in enumerate(sls):
        outs.append(sv[c] + q_decay * _dot(q[sl], r_st.astype(BF16)))
        r_st = chunk_decay * r_st + kv[c]
    r_ref[...] = r_st
    o = jnp.concatenate(outs, axis=0) if len(outs) > 1 else outs[0]
    ms = jnp.mean(o * o, axis=1, keepdims=True)
    y = o * lax.rsqrt(ms + NORM_EPS) * ng_ref[...]
    g = g_ref[...]
    o_ref[...] = (y * (g * _sigmoid(g))).astype(o_ref.dtype)


def retention_mix(p, cc, ss, log_gamma, norm_g, tb=512):
    s = p.shape[0]
    tb = min(tb, s)
    assert s % tb == 0
    kb = C_KW // C_KHEAD
    vb = (2 * C_KW) // C_VHEAD
    gb = (2 * C_KW + C_VW) // C_VHEAD
    smem = pl.BlockSpec(memory_space=pltpu.SMEM)
    vmem = 2 * (2 * tb * C_KHEAD * 4 + 2 * tb * C_VHEAD * 4 + 2 * tb * C_KHEAD * 4 + tb * C_VHEAD * 2) + (24 << 20)
    return pl.pallas_call(
        functools.partial(_retention_kernel, tb=tb),
        grid=(C_H, s // tb),
        in_specs=[smem,
                  pl.BlockSpec((tb, C_KHEAD), lambda h, i: (i, h)),
                  pl.BlockSpec((tb, C_KHEAD), lambda h, i: (i, kb + h)),
                  pl.BlockSpec((tb, C_VHEAD), lambda h, i: (i, vb + h)),
                  pl.BlockSpec((tb, C_VHEAD), lambda h, i: (i, gb + h)),
                  pl.BlockSpec((tb, C_KHEAD), lambda h, i: (i, 0)),
                  pl.BlockSpec((tb, C_KHEAD), lambda h, i: (i, 0)),
                  pl.BlockSpec((1, C_VHEAD), lambda h, i: (0, h))],
        out_specs=pl.BlockSpec((tb, C_VHEAD), lambda h, i: (i, h)),
        out_shape=jax.ShapeDtypeStruct((s, MIX_W), BF16),
        scratch_shapes=[pltpu.VMEM((C_KHEAD, C_VHEAD), F32)],
        compiler_params=_cparams(("parallel", "arbitrary"), vmem),
        name="retention",
    )(log_gamma, p, p, p, p, cc, ss, norm_g.reshape(1, C_VW))


def _rglru_kernel(x_ref, gate_ref, cw_ref, cp_ref, wr_ref, wi_ref, ybuf_ref, o_ref, px_ref, h_ref, *, tb, ts):
    del ybuf_ref
    i = pl.program_id(1)

    @pl.when(i == 0)
    def _():
        px_ref[...] = jnp.zeros_like(px_ref)
        h_ref[...] = jnp.zeros_like(h_ref)

    cw = cw_ref[...]
    cp = cp_ref[...]
    conv_b, b_r, b_i, lam = cp[0:1], cp[1:2], cp[2:3], cp[3:4]
    x = x_ref[...]
    ext = jnp.concatenate([px_ref[...], x], axis=0)
    xc = conv_b + x * cw[CONV_W - 1:CONV_W]
    for d in range(1, CONV_W):
        xc = xc + pltpu.roll(ext, d, 0)[SUBLANES:] * cw[CONV_W - 1 - d:CONV_W - d]
    px_ref[...] = x[tb - SUBLANES:tb]

    xcb = xc.astype(BF16)
    r = _sigmoid(_dot(xcb, wr_ref[0]) + b_r)
    ig = _sigmoid(_dot(xcb, wi_ref[0]) + b_i)
    log_a = -LRU_C * r * _softplus(-lam)
    a = jnp.exp(log_a)
    u = jnp.sqrt(-jnp.tanh(log_a) * (a * a + 1.0)) * (ig * xc)

    rows = lax.broadcasted_iota(jnp.int32, (tb, 1), 0) % ts
    d = 1
    while d < ts:
        keep = rows >= d
        a_sh = jnp.where(keep, pltpu.roll(a, d, 0), 1.0)
        u_sh = jnp.where(keep, pltpu.roll(u, d, 0), 0.0)
        u = a * u_sh + u
        a = a * a_sh
        d *= 2
    h_prev = h_ref[0:1, :]
    parts = []
    for c in range(tb // ts):
        h_c = a[c * ts:(c + 1) * ts] * h_prev + u[c * ts:(c + 1) * ts]
        h_prev = h_c[ts - 1:ts]
        parts.append(h_c)
    hh = jnp.concatenate(parts, axis=0) if len(parts) > 1 else parts[0]
    h_ref[...] = jnp.broadcast_to(h_prev, h_ref.shape)

    gt = gate_ref[...]
    gelu = 0.5 * gt * (1.0 + jnp.tanh(0.7978845608028654 * (gt + 0.044715 * (gt * gt * gt))))
    o_ref[...] = (hh * gelu).astype(o_ref.dtype)


def rglru_mix(p, conv_w, conv_b, w_r, b_r, w_i, b_i, lam, y_buf, tb=1024, ts=256):
    s = p.shape[0]
    tb = min(tb, s)
    ts = min(ts, tb)
    assert s % tb == 0 and tb % ts == 0
    xb0 = (2 * C_KW + 2 * C_VW) // D_BW
    gt0 = xb0 + D_W // D_BW
    out0 = C_VW // D_BW
    zeros = jnp.zeros((D_W,), F32)
    cp = jnp.stack([conv_b, b_r, b_i, lam, zeros, zeros, zeros, zeros])
    vmem = 2 * (2 * tb * D_BW * 4 + tb * D_BW * 2 + 2 * D_BW * D_BW * 2) + (24 << 20)
    return pl.pallas_call(
        functools.partial(_rglru_kernel, tb=tb, ts=ts),
        grid=(D_BLOCKS, s // tb),
        in_specs=[pl.BlockSpec((tb, D_BW), lambda n, i: (i, xb0 + n)),
                  pl.BlockSpec((tb, D_BW), lambda n, i: (i, gt0 + n)),
                  pl.BlockSpec((CONV_W, D_BW), lambda n, i: (0, n)),
                  pl.BlockSpec((SUBLANES, D_BW), lambda n, i: (0, n)),
                  pl.BlockSpec((1, D_BW, D_BW), lambda n, i: (n, 0, 0)),
                  pl.BlockSpec((1, D_BW, D_BW), lambda n, i: (n, 0, 0)),
                  pl.BlockSpec(memory_space=pl.ANY)],
        out_specs=pl.BlockSpec((tb, D_BW), lambda n, i: (i, out0 + n)),
        out_shape=jax.ShapeDtypeStruct(y_buf.shape, BF16),
        input_output_aliases={6: 0},
        scratch_shapes=[pltpu.VMEM((SUBLANES, D_BW), F32), pltpu.VMEM((SUBLANES, D_BW), F32)],
        compiler_params=_cparams(("parallel", "arbitrary"), vmem),
        name="rglru",
    )(p, p, conv_w, cp, w_r.astype(BF16), w_i.astype(BF16), y_buf)


def _xattn_kernel(q_ref, k_ref, v_ref, o_ref):
    for hd in range(XA_H):
        sl = slice(hd * XA_HEAD, (hd + 1) * XA_HEAD)
        sc = _dot_nt(q_ref[:, sl], k_ref[:, sl]) * (XA_HEAD ** -0.5)
        sc = sc - jnp.max(sc, axis=1, keepdims=True)
        e = jnp.exp(sc)
        pr = e / jnp.sum(e, axis=1, keepdims=True)
        o_ref[:, sl] = _dot(pr.astype(BF16), v_ref[:, sl]).astype(o_ref.dtype)


def xattn_core(q, kmem, vmem_, tm=512):
    s = q.shape[0]
    nm = kmem.shape[0]
    vmem = 2 * (2 * tm * XA_W * 2 + 2 * nm * XA_W * 2) + (16 << 20)
    return pl.pallas_call(
        _xattn_kernel,
        grid=(s // tm,),
        in_specs=[pl.BlockSpec((tm, XA_W), lambda i: (i, 0)),
                  pl.BlockSpec((nm, XA_W), lambda i: (0, 0)),
                  pl.BlockSpec((nm, XA_W), lambda i: (0, 0))],
        out_specs=pl.BlockSpec((tm, XA_W), lambda i: (i, 0)),
        out_shape=jax.ShapeDtypeStruct((s, XA_W), BF16),
        compiler_params=_cparams(("parallel",), vmem),
        name="xattn",
    )(q, kmem, vmem_)


def kernel(x, mem, mem_norm_g, norm_mix_g, norm_xattn_g, norm_mlp_g, xattn_wq, xattn_wk, xattn_wv, xattn_wo, mlp_w1, mlp_w2, even_w_in, even_w_out, rwkv_mu, rwkv_w0, rwkv_w_up, rwkv_a0, rwkv_a_up, rwkv_k_k, rwkv_k_a, rwkv_r_k, rwkv_ln_w, rwkv_ln_b, mlstm_i_b, mlstm_f_b, mlstm_norm_g, odd_w_in, odd_w_out, ret_norm_g, lru_conv_w, lru_conv_b, lru_w_r, lru_b_r, lru_w_i, lru_b_i, lru_lambda, final_norm_g):
    b, s, d = x.shape
    assert b == 1 and d == D_MODEL and s % 256 == 0
    h = x.reshape(s, d)
    mem_n = rmsnorm(mem.reshape(-1, d), mem_norm_g, BF16)
    even_wt = jnp.swapaxes(even_w_in, 1, 2)
    even_wt_g = jnp.pad(even_wt[:, A_COLS + EV_GATES:, :], ((0, 0), (0, LANES - 2 * B_H), (0, 0)))

    positions = jnp.arange(s, dtype=F32)
    inv_freq = ROPE_BASE ** (-jnp.arange(0, C_KHEAD, 2, dtype=F32) / C_KHEAD)
    ang = positions[:, None] * inv_freq[None, :]
    cos, sin = jnp.cos(ang), jnp.sin(ang)
    rope_cc = jnp.concatenate([cos, cos], axis=1)
    rope_ss = jnp.concatenate([-sin, sin], axis=1)
    log_gamma = jnp.log1p(-jnp.exp2(-5.0 - jnp.arange(C_H, dtype=F32)))

    for layer in range(DEPTH):
        j = layer // 2
        xn = rmsnorm(h, norm_mix_g[layer], BF16)
        if layer % 2 == 0:
            p_a = matmul(xn, even_wt, j, F32, n=EV_A_COLS, w_transposed=True)
            p_b = matmul(xn, even_wt, j, F32, n=4 * B_W, w_transposed=True, row0=A_COLS)
            p_g = matmul(xn, even_wt_g, j, F32, w_transposed=True)
            y = rwkv7_mix(p_a, rwkv_mu[j], rwkv_w0[j], rwkv_w_up[j], rwkv_a0[j], rwkv_a_up[j],
                          rwkv_k_k[j], rwkv_k_a[j], rwkv_r_k[j], rwkv_ln_w[j], rwkv_ln_b[j])
            gates_t = p_g[:, :2 * B_H].reshape(s // CHUNK, CHUNK, 2 * B_H).transpose(0, 2, 1)
            y = mlstm_mix(p_b, gates_t, mlstm_i_b[j], mlstm_f_b[j], mlstm_norm_g[j], y)
            w_out = even_w_out
        else:
            p = matmul(xn, odd_w_in, j, F32)
            y = retention_mix(p, rope_cc, rope_ss, log_gamma, ret_norm_g[j])
            y = rglru_mix(p, lru_conv_w[j], lru_conv_b[j], lru_w_r[j], lru_b_r[j],
                          lru_w_i[j], lru_b_i[j], lru_lambda[j], y)
            w_out = odd_w_out
        h = matmul(y, w_out, j, F32, epilogue="residual", residual=h)

        xn = rmsnorm(h, norm_xattn_g[layer], BF16)
        q = matmul(xn, xattn_wq, layer, BF16)
        kmem = matmul(mem_n, xattn_wk, layer, BF16)
        vmem_ = matmul(mem_n, xattn_wv, layer, BF16)
        o = xattn_core(q, kmem, vmem_)
        h = matmul(o, xattn_wo, layer, F32, epilogue="residual", residual=h)

        xn = rmsnorm(h, norm_mlp_g[layer], BF16)
        u = matmul(xn, mlp_w1, layer, BF16, epilogue="relu2")
        h = matmul(u, mlp_w2, layer, F32, epilogue="residual", residual=h)

    return rmsnorm(h, final_norm_g, F32).reshape(b, s, d)
```

```python
import functools

import jax
import jax.numpy as jnp
from jax import lax
from jax.experimental import pallas as pl
from jax.experimental.pallas import tpu as pltpu

D_MODEL = 4096
DEPTH = 4
CHUNK = 64
NORM_EPS = 1e-6
MIX_W = D_MODEL
A_W = MIX_W // 2
A_HEAD = 64
A_H = A_W // A_HEAD
A_LORA = 96
A_LN_EPS = 64e-5
B_W = MIX_W // 2
B_H = 8
B_HEAD = B_W // B_H
GATE_CAP = 15.0
C_VW = MIX_W // 2
C_H = 8
C_VHEAD = C_VW // C_H
C_KHEAD = C_VHEAD // 2
C_KW = C_H * C_KHEAD
ROPE_BASE = 10000.0
D_W = MIX_W // 2
D_BLOCKS = 16
D_BW = D_W // D_BLOCKS
CONV_W = 4
LRU_C = 8.0
D_FF = 4 * D_MODEL
XA_H = 4
XA_HEAD = 256
XA_W = XA_H * XA_HEAD
A_COLS = 4 * A_W + 2 * A_LORA
B_COLS = 4 * B_W + 2 * B_H

LANES = 128
SUBLANES = 8
VMEM_BYTES_V7X = 64 * 1024 * 1024

EV_LORA = 4 * A_W
EV_A_COLS = EV_LORA + 4 * LANES
EV_GATES = 4 * B_W
assert A_COLS <= EV_LORA + 2 * LANES

BF16 = jnp.bfloat16
F32 = jnp.float32
EXP_M_HALF = 0.6065306597126334


def _cparams(dims, vmem_bytes):
    limit = min(int(vmem_bytes), VMEM_BYTES_V7X - 4 * 1024 * 1024)
    return pltpu.CompilerParams(dimension_semantics=dims, vmem_limit_bytes=limit)


def _dot(a, b):
    return jnp.dot(a, b, preferred_element_type=F32)


def _dot_nt(a, b):
    return lax.dot_general(a, b, (((1,), (1,)), ((), ())), preferred_element_type=F32)


def _dot_tn(a, b):
    return lax.dot_general(a, b, (((0,), (0,)), ((), ())), preferred_element_type=F32)


def _sigmoid(x):
    return 1.0 / (1.0 + jnp.exp(-x))


def _softplus(x):
    return jnp.maximum(x, 0.0) + jnp.log1p(jnp.exp(-jnp.abs(x)))


def _rmsnorm_kernel(x_ref, g_ref, o_ref):
    x = x_ref[...]
    ms = jnp.mean(x * x, axis=-1, keepdims=True)
    o_ref[...] = (x * lax.rsqrt(ms + NORM_EPS) * g_ref[...]).astype(o_ref.dtype)


def rmsnorm(x, g, out_dtype, tm=256):
    m, d = x.shape
    tm = min(tm, m)
    vmem = 2 * tm * d * (4 + jnp.dtype(out_dtype).itemsize) + 4 * d * 4 + (4 << 20)
    return pl.pallas_call(
        _rmsnorm_kernel,
        grid=(m // tm,),
        in_specs=[pl.BlockSpec((tm, d), lambda i: (i, 0)), pl.BlockSpec((1, d), lambda i: (0, 0))],
        out_specs=pl.BlockSpec((tm, d), lambda i: (i, 0)),
        out_shape=jax.ShapeDtypeStruct((m, d), out_dtype),
        compiler_params=_cparams(("parallel",), vmem),
        name="rmsnorm",
    )(x, g.reshape(1, d))


def _mm_kernel(*refs, nk, epilogue, w_transposed):
    if epilogue == "residual":
        x_ref, w_ref, r_ref, o_ref = refs
    else:
        x_ref, w_ref, o_ref = refs
        r_ref = None

    if nk > 1:
        @pl.when(pl.program_id(2) == 0)
        def _():
            o_ref[...] = r_ref[...]

    if w_transposed:
        part = _dot_nt(x_ref[...], w_ref[0].astype(BF16))
    else:
        part = _dot(x_ref[...], w_ref[...].astype(BF16))
    if nk > 1:
        o_ref[...] += part
    elif epilogue == "residual":
        o_ref[...] = r_ref[...] + part
    elif epilogue == "relu2":
        o_ref[...] = jnp.square(jnp.maximum(part, 0.0)).astype(o_ref.dtype)
    else:
        o_ref[...] = part.astype(o_ref.dtype)


def matmul(x, w, layer, out_dtype, n=None, w_transposed=False, row0=0, epilogue="none", residual=None,
           tm=2048, tn=None, tk=None):
    m, kdim = x.shape
    n_all = w.shape[1] if w_transposed else w.shape[2]
    n = n_all if n is None else n
    tm = min(tm, m)
    if tk is None:
        tk = kdim if kdim <= 4096 else 2048
    nk = kdim // tk
    single_x = nk == 1 and epilogue != "residual" and m > tm
    if tn is None:
        tn = 256 if (nk == 1 and not single_x) else 512
    tn = min(tn, n)
    assert m % tm == 0 and n % tn == 0 and kdim % tk == 0 and row0 + n <= n_all
    if w_transposed:
        w_spec = pl.BlockSpec((pl.Element(1), pl.Element(tn), pl.Element(tk)),
                              lambda i, j, k: (layer, pl.multiple_of(row0 + j * tn, SUBLANES),
                                               pl.multiple_of(k * tk, LANES)))
    else:
        assert row0 == 0
        w_spec = pl.BlockSpec((pl.Squeezed(), tk, tn), lambda i, j, k: (layer, k, j))
    if single_x:
        x_spec = pl.BlockSpec((tm, tk), lambda i, j, k: (i, k), pipeline_mode=pl.Buffered(1))
    else:
        x_spec = pl.BlockSpec((tm, tk), lambda i, j, k: (i, k))
    in_specs = [x_spec, w_spec]
    args = [x, w]
    osz = jnp.dtype(out_dtype).itemsize
    wsz = jnp.dtype(w.dtype).itemsize
    vmem = ((1 if single_x else 2) * tm * tk * 2 + 2 * (tk * tn * wsz + tm * tn * osz)
            + tk * tn * 2 + 2 * tm * tn * 4 + (8 << 20))
    if epilogue == "residual":
        in_specs.append(pl.BlockSpec((tm, tn), lambda i, j, k: (i, j)))
        args.append(residual)
        vmem += 2 * tm * tn * 4
    assert nk == 1 or (epilogue == "residual" and out_dtype == F32)
    return pl.pallas_call(
        functools.partial(_mm_kernel, nk=nk, epilogue=epilogue, w_transposed=w_transposed),
        grid=(m // tm, n // tn, nk),
        in_specs=in_specs,
        out_specs=pl.BlockSpec((tm, tn), lambda i, j, k: (i, j)),
        out_shape=jax.ShapeDtypeStruct((m, n), out_dtype),
        compiler_params=_cparams(("parallel", "arbitrary", "arbitrary"), vmem),
        name="matmul_" + epilogue,
    )(*args)


def _shift_rows(x, prev_ref, d):
    ext = jnp.concatenate([prev_ref[...], x], axis=0)
    return pltpu.roll(ext, d, 0)[SUBLANES:]


def _pair_sum(x, first):
    s1 = jnp.sum(jnp.where(first, x, 0.0), axis=1, keepdims=True)
    s2 = jnp.sum(jnp.where(first, 0.0, x), axis=1, keepdims=True)
    return jnp.where(first, s1, s2)


def _rwkv_kernel(r_ref, k_ref, v_ref, g_ref, lo_ref, cp_ref, mulo_ref, wup_ref, aup_ref, o_ref,
                 state_ref, pr_ref, pk_ref, pv_ref, pg_ref, plo_ref, *, tb):
    L = CHUNK
    i = pl.program_id(1)

    @pl.when(i == 0)
    def _():
        state_ref[...] = jnp.zeros_like(state_ref)
        pr_ref[...] = jnp.zeros_like(pr_ref)
        pk_ref[...] = jnp.zeros_like(pk_ref)
        pv_ref[...] = jnp.zeros_like(pv_ref)
        pg_ref[...] = jnp.zeros_like(pg_ref)
        plo_ref[...] = jnp.zeros_like(plo_ref)

    cp = cp_ref[...]
    mu_r, mu_k, mu_v, mu_g = cp[0:1], cp[1:2], cp[2:3], cp[3:4]
    w0, a0, k_k, k_a, r_k, ln_w, ln_b = (cp[4:5], cp[5:6], cp[6:7], cp[7:8], cp[8:9], cp[9:10], cp[10:11])

    def shift_lerp(x_ref, prev_ref, mu):
        x = x_ref[...]
        sh = _shift_rows(x, prev_ref, 1)
        prev_ref[...] = x[tb - SUBLANES:tb]
        return x + (sh - x) * mu

    r = shift_lerp(r_ref, pr_ref, mu_r)
    k = shift_lerp(k_ref, pk_ref, mu_k)
    v = shift_lerp(v_ref, pv_ref, mu_v)
    g = shift_lerp(g_ref, pg_ref, mu_g)
    lo = shift_lerp(lo_ref, plo_ref, mulo_ref[...])

    lane = lax.broadcasted_iota(jnp.int32, (1, LANES), 1)
    first = lane < A_HEAD
    m1 = first.astype(F32)
    m2 = 1.0 - m1

    w = w0 + _dot(jnp.tanh(lo).astype(BF16), wup_ref[...])
    logw = -_sigmoid(w) * EXP_M_HALF
    a = _sigmoid(a0 + _dot(lo.astype(BF16), aup_ref[...]))
    kk = k * k_k
    k = k * (1.0 + (a - 1.0) * k_a)
    kk = kk * lax.rsqrt(_pair_sum(kk * kk, first) + 1e-12)
    alpha = -kk
    beta = kk * a
    bonus = _pair_sum(r * k * r_k, first) * v

    nc = tb // L
    row = lax.broadcasted_iota(jnp.int32, (L, L), 0)
    col = lax.broadcasted_iota(jnp.int32, (L, L), 1)
    tril_strict = row > col
    row2 = lax.broadcasted_iota(jnp.int32, (L, LANES), 0)
    col2 = lax.broadcasted_iota(jnp.int32, (L, LANES), 1) % L
    tril_incl2 = row2 >= col2
    bd_r = lax.broadcasted_iota(jnp.int32, (LANES, LANES), 0) // A_HEAD
    bd_c = lax.broadcasted_iota(jnp.int32, (LANES, LANES), 1) // A_HEAD
    blockdiag = bd_r == bd_c

    pos = lax.broadcasted_iota(jnp.int32, (tb, 1), 0) % L
    cum = logw
    step = 1
    while step < L:
        cum = cum + jnp.where(pos >= step, pltpu.roll(cum, step, 0), 0.0)
        step *= 2
    e_pos = jnp.exp(cum)
    e_neg = jnp.exp(-cum)
    ab = alpha * jnp.exp(cum - logw)
    rb = r * e_pos
    kt = (k * e_neg).astype(BF16)
    bt = (beta * e_neg).astype(BF16)
    vv = v.astype(BF16)
    heads = ((m1, ab * m1, (rb * m1).astype(BF16)), (m2, ab * m2, (rb * m2).astype(BF16)))
    zero_top = jnp.zeros((L, LANES), BF16)

    sls = [slice(c * L, (c + 1) * L) for c in range(nc)]
    kb = [jnp.concatenate([kt[sl], bt[sl]], axis=0) for sl in sls]
    xa = []
    for c, sl in enumerate(sls):
        x = jnp.concatenate([heads[0][1][sl].astype(BF16), heads[1][1][sl].astype(BF16),
                             heads[0][2][sl], heads[1][2][sl]], axis=0)
        xa.append(_dot_nt(x, kb[c]))
    prob = [(c, h) for c in range(nc) for h in range(2)]
    n_pow, ys, a_r = [], [], []
    for c, h in prob:
        an = xa[c][h * L:(h + 1) * L]
        a_ak = jnp.where(tril_strict, an[:, :L], 0.0).astype(BF16)
        n_pow.append(jnp.where(tril_strict, an[:, L:], 0.0).astype(BF16))
        a_r.append(jnp.where(tril_incl2, xa[c][(2 + h) * L:(3 + h) * L], 0.0).astype(BF16))
        ys.append(_dot(a_ak, vv[sls[c]]) * heads[h][0] + pltpu.roll(heads[h][1][sls[c]], A_HEAD, 1))
    for lvl in range(6):
        ys = [y + _dot(n, y.astype(BF16)) for n, y in zip(n_pow, ys)]
        if lvl < 5:
            n_pow = [_dot(n, n).astype(BF16) for n in n_pow]
    q_c, o0_c, g_c, pb_c = [], [], [], []
    for c, sl in enumerate(sls):
        top = jnp.concatenate([vv[sl], zero_top], axis=1)
        y2 = []
        o0 = jnp.zeros((L, LANES), F32)
        qq = rb[sl]
        for h in range(2):
            mh = heads[h][0]
            y = ys[2 * c + h]
            y = jnp.concatenate([y * mh, pltpu.roll(y * (1.0 - mh), A_HEAD, 1)], axis=1)
            y2.append(y)
            out = _dot(a_r[2 * c + h], jnp.concatenate([top, y.astype(BF16)], axis=0))
            o0 = o0 + out[:, :LANES] * mh
            qq = qq + out[:, LANES:]
        ysum = (y2[0] + y2[1]).astype(BF16)
        gp = _dot_tn(jnp.concatenate([top, ysum], axis=0), kb[c])
        g_c.append(jnp.where(blockdiag, gp[:LANES], 0.0))
        pb_c.append(jnp.where(blockdiag, gp[LANES:], 0.0).astype(BF16))
        q_c.append(qq.astype(BF16))
        o0_c.append(o0)

    state = state_ref[...]
    outs = []
    for c in range(nc):
        st_b = state.astype(BF16)
        outs.append(o0_c[c] + _dot_nt(q_c[c], st_b))
        state = (state + _dot(st_b, pb_c[c]) + g_c[c]) * e_pos[(c + 1) * L - 1:(c + 1) * L]
    state_ref[...] = state
    o = jnp.concatenate(outs, axis=0) if len(outs) > 1 else outs[0]

    mean = _pair_sum(o, first) * (1.0 / A_HEAD)
    d = o - mean
    var = _pair_sum(d * d, first) * (1.0 / A_HEAD)
    y = d * lax.rsqrt(var + A_LN_EPS) * ln_w + ln_b
    o_ref[...] = ((y + bonus) * _sigmoid(g)).astype(o_ref.dtype)


def rwkv7_mix(p, mu, w0, w_up, a0, a_up, k_k, k_a, r_k, ln_w, ln_b, tb=1024):
    s = p.shape[0]
    tb = min(tb, s)
    assert s % tb == 0
    npair = A_H // 2
    nblk = A_W // LANES
    zeros = jnp.zeros((A_W,), F32)
    cp = jnp.stack([mu[0:A_W], mu[A_W:2 * A_W], mu[2 * A_W:3 * A_W], mu[3 * A_W:4 * A_W],
                    w0, a0, k_k, k_a, r_k, ln_w, ln_b, zeros, zeros, zeros, zeros, zeros])
    tail = 2 * LANES - 2 * A_LORA
    mulo = jnp.concatenate([mu[4 * A_W:], jnp.zeros((tail,), F32)]).reshape(1, 2 * LANES)
    wup = jnp.concatenate([w_up, jnp.zeros((2 * LANES - A_LORA, A_W), F32)], axis=0).astype(BF16)
    aup = jnp.concatenate([jnp.zeros((A_LORA, A_W), F32), a_up, jnp.zeros((tail, A_W), F32)], axis=0).astype(BF16)
    lora_blk = EV_LORA // (2 * LANES)

    def col(j):
        return pl.BlockSpec((tb, LANES), lambda hp, i, j=j: (i, j * nblk + hp))

    vmem = 2 * (4 * tb * LANES * 4 + tb * 2 * LANES * 4 + tb * LANES * 2) + (24 << 20)
    return pl.pallas_call(
        functools.partial(_rwkv_kernel, tb=tb),
        grid=(npair, s // tb),
        in_specs=[col(0), col(1), col(2), col(3),
                  pl.BlockSpec((tb, 2 * LANES), lambda hp, i: (i, lora_blk)),
                  pl.BlockSpec((16, LANES), lambda hp, i: (0, hp)),
                  pl.BlockSpec((1, 2 * LANES), lambda hp, i: (0, 0)),
                  pl.BlockSpec((2 * LANES, LANES), lambda hp, i: (0, hp)),
                  pl.BlockSpec((2 * LANES, LANES), lambda hp, i: (0, hp))],
        out_specs=pl.BlockSpec((tb, LANES), lambda hp, i: (i, hp)),
        out_shape=jax.ShapeDtypeStruct((s, MIX_W), BF16),
        scratch_shapes=[pltpu.VMEM((LANES, LANES), F32),
                        pltpu.VMEM((SUBLANES, LANES), F32), pltpu.VMEM((SUBLANES, LANES), F32),
                        pltpu.VMEM((SUBLANES, LANES), F32), pltpu.VMEM((SUBLANES, LANES), F32),
                        pltpu.VMEM((SUBLANES, 2 * LANES), F32)],
        compiler_params=_cparams(("parallel", "arbitrary"), vmem),
        name="rwkv7",
    )(p, p, p, p, p, cp, mulo, wup, aup)


def _mlstm_kernel(ib_ref, fb_ref, q_ref, k_ref, v_ref, og_ref, gt_ref, ng_ref, ybuf_ref, o_ref,
                  c_ref, n_ref, m_ref, *, tb):
    del ybuf_ref
    L = CHUNK
    h = pl.program_id(0)
    i = pl.program_id(1)

    @pl.when(i == 0)
    def _():
        c_ref[...] = jnp.zeros_like(c_ref)
        n_ref[...] = jnp.zeros_like(n_ref)
        m_ref[...] = jnp.zeros_like(m_ref)

    i_b = ib_ref[h]
    f_b = fb_ref[h]
    row = lax.broadcasted_iota(jnp.int32, (L, L), 0)
    col = lax.broadcasted_iota(jnp.int32, (L, L), 1)
    causal = row >= col
    eye = row == col

    def to_col(x_row):
        return jnp.sum(jnp.where(eye, jnp.broadcast_to(x_row, (L, L)), 0.0), axis=1, keepdims=True)

    pre = []
    for c in range(tb // L):
        sl = slice(c * L, (c + 1) * L)
        gi = gt_ref[c, pl.ds(h, 1), :]
        gf = gt_ref[c, pl.ds(B_H + h, 1), :]
        li_row = GATE_CAP * jnp.tanh((gi + i_b) / GATE_CAP)
        lf_row = -_softplus(-(GATE_CAP * jnp.tanh((gf + f_b) / GATE_CAP)))
        li_col = to_col(li_row)
        lf_col = to_col(lf_row)
        cum_col = jnp.sum(jnp.where(causal, jnp.broadcast_to(lf_row, (L, L)), 0.0), axis=1, keepdims=True)
        cum_row = jnp.sum(jnp.where(row <= col, jnp.broadcast_to(lf_col, (L, L)), 0.0), axis=0, keepdims=True)
        total = jnp.sum(lf_row, axis=1, keepdims=True)

        qf = q_ref[sl, :]
        q = qf.astype(BF16)
        kf = k_ref[sl, :] * (B_HEAD ** -0.5)
        vb = v_ref[sl, :].astype(BF16)

        d_log = jnp.where(causal, cum_col - cum_row + li_row, -jnp.inf)
        m_loc = jnp.max(d_log, axis=1, keepdims=True)
        s_loc = _dot_nt(q, kf.astype(BF16)) * jnp.exp(d_log - m_loc)
        sv = _dot(s_loc.astype(BF16), vb)
        s_sum = jnp.sum(s_loc, axis=1, keepdims=True)

        log_w_col = total - cum_col + li_col
        m_k = jnp.max(log_w_col, axis=0, keepdims=True)
        kw = kf * jnp.exp(log_w_col - m_k)
        kv = _dot_tn(kw.astype(BF16), vb)
        k_sum = jnp.sum(kw, axis=0, keepdims=True)
        pre.append((qf, q, cum_col, total, m_loc, sv, s_sum, m_k, kv, k_sum))

    c_st = c_ref[...]
    n_st = n_ref[...]
    m_st = m_ref[0:1, 0:1]
    outs = []
    for qf, q, cum_col, total, m_loc, sv, s_sum, m_k, kv, k_sum in pre:
        m_inter = cum_col + m_st
        m_row = jnp.maximum(m_loc, m_inter)
        w_loc = jnp.exp(m_loc - m_row)
        w_inter = jnp.exp(m_inter - m_row)
        num = w_loc * sv + w_inter * _dot(q, c_st.astype(BF16))
        den = w_loc * s_sum + w_inter * jnp.sum(qf * n_st, axis=1, keepdims=True)
        outs.append(num / jnp.maximum(jnp.abs(den), jnp.exp(-m_row)))

        m_new = jnp.maximum(total + m_st, m_k)
        carry = jnp.exp(total + m_st - m_new)
        w_new = jnp.exp(m_k - m_new)
        c_st = carry * c_st + w_new * kv
        n_st = carry * n_st + w_new * k_sum
        m_st = m_new
    c_ref[...] = c_st
    n_ref[...] = n_st
    m_ref[...] = jnp.broadcast_to(m_st, m_ref.shape)

    hh = jnp.concatenate(outs, axis=0) if len(outs) > 1 else outs[0]
    ms = jnp.mean(hh * hh, axis=1, keepdims=True)
    y = hh * lax.rsqrt(ms + NORM_EPS) * ng_ref[...]
    o_ref[...] = (y * _sigmoid(og_ref[...])).astype(o_ref.dtype)


def mlstm_mix(p, gates_t, i_b, f_b, norm_g, y_buf, tb=512):
    s = p.shape[0]
    tb = min(tb, s)
    assert s % tb == 0
    nblk = B_W // B_HEAD
    out0 = A_W // B_HEAD

    def col(j):
        return pl.BlockSpec((tb, B_HEAD), lambda h, i, j=j: (i, j * nblk + h))

    smem = pl.BlockSpec(memory_space=pltpu.SMEM)
    vmem = 2 * (4 * tb * B_HEAD * 4 + tb * B_HEAD * 2) + (24 << 20)
    return pl.pallas_call(
        functools.partial(_mlstm_kernel, tb=tb),
        grid=(B_H, s // tb),
        in_specs=[smem, smem, col(0), col(1), col(2), col(3),
                  pl.BlockSpec((tb // CHUNK, 2 * B_H, CHUNK), lambda h, i: (i, 0, 0)),
                  pl.BlockSpec((1, B_HEAD), lambda h, i: (0, h)),
                  pl.BlockSpec(memory_space=pl.ANY)],
        out_specs=pl.BlockSpec((tb, B_HEAD), lambda h, i: (i, out0 + h)),
        out_shape=jax.ShapeDtypeStruct(y_buf.shape, BF16),
        input_output_aliases={8: 0},
        scratch_shapes=[pltpu.VMEM((B_HEAD, B_HEAD), F32), pltpu.VMEM((1, B_HEAD), F32),
                        pltpu.VMEM((1, LANES), F32)],
        compiler_params=_cparams(("parallel", "arbitrary"), vmem),
        name="mlstm",
    )(i_b, f_b, p, p, p, p, gates_t, norm_g.reshape(1, B_W), y_buf)


def _retention_kernel(lg_ref, q_ref, k_ref, v_ref, g_ref, cc_ref, ss_ref, ng_ref, o_ref, r_ref, *, tb):
    L = CHUNK
    h = pl.program_id(0)
    i = pl.program_id(1)

    @pl.when(i == 0)
    def _():
        r_ref[...] = jnp.zeros_like(r_ref)

    lg = lg_ref[h]
    row = lax.broadcasted_iota(jnp.int32, (L, L), 0)
    col = lax.broadcasted_iota(jnp.int32, (L, L), 1)
    intra = jnp.exp(lg * jnp.abs(row - col).astype(F32))
    pos = lax.broadcasted_iota(jnp.int32, (L, 1), 0).astype(F32)
    q_decay = jnp.exp(lg * (pos + 1.0))
    k_decay = jnp.exp(lg * (L - 1.0 - pos))
    chunk_decay = jnp.exp(jnp.full((1, 1), lg * L, F32))

    cc = cc_ref[...]
    ss = ss_ref[...]

    def rope(t):
        return t * cc + pltpu.roll(t, C_KHEAD // 2, 1) * ss

    q = rope(q_ref[...]).astype(BF16)
    k = rope(k_ref[...]) * (C_KHEAD ** -0.5)
    vb = v_ref[...].astype(BF16)
    sls = [slice(c * L, (c + 1) * L) for c in range(tb // L)]
    sv = [_dot((_dot_nt(q[sl], k[sl].astype(BF16)) * intra).astype(BF16), vb[sl]) for sl in sls]
    kv = [_dot_tn((k[sl] * k_decay).astype(BF16), vb[sl]) for sl in sls]
    r_st = r_ref[...]
    outs = []
    for c, sl in enumerate(sls):
        outs.append(sv[c] + q_decay * _dot(q[sl], r_st.astype(BF16)))
        r_st = chunk_decay * r_st + kv[c]
    r_ref[...] = r_st
    o = jnp.concatenate(outs, axis=0) if len(outs) > 1 else outs[0]
    ms = jnp.mean(o * o, axis=1, keepdims=True)
    y = o * lax.rsqrt(ms + NORM_EPS) * ng_ref[...]
    g = g_ref[...]
    o_ref[...] = (y * (g * _sigmoid(g))).astype(o_ref.dtype)


def retention_mix(p, cc, ss, log_gamma, norm_g, tb=512):
    s = p.shape[0]
    tb = min(tb, s)
    assert s % tb == 0
    kb = C_KW // C_KHEAD
    vb = (2 * C_KW) // C_VHEAD
    gb = (2 * C_KW + C_VW) // C_VHEAD
    smem = pl.BlockSpec(memory_space=pltpu.SMEM)
    vmem = 2 * (2 * tb * C_KHEAD * 4 + 2 * tb * C_VHEAD * 4 + 2 * tb * C_KHEAD * 4 + tb * C_VHEAD * 2) + (24 << 20)
    return pl.pallas_call(
        functools.partial(_retention_kernel, tb=tb),
        grid=(C_H, s // tb),
        in_specs=[smem,
                  pl.BlockSpec((tb, C_KHEAD), lambda h, i: (i, h)),
                  pl.BlockSpec((tb, C_KHEAD), lambda h, i: (i, kb + h)),
                  pl.BlockSpec((tb, C_VHEAD), lambda h, i: (i, vb + h)),
                  pl.BlockSpec((tb, C_VHEAD), lambda h, i: (i, gb + h)),
                  pl.BlockSpec((tb, C_KHEAD), lambda h, i: (i, 0)),
                  pl.BlockSpec((tb, C_KHEAD), lambda h, i: (i, 0)),
                  pl.BlockSpec((1, C_VHEAD), lambda h, i: (0, h))],
        out_specs=pl.BlockSpec((tb, C_VHEAD), lambda h, i: (i, h)),
        out_shape=jax.ShapeDtypeStruct((s, MIX_W), BF16),
        scratch_shapes=[pltpu.VMEM((C_KHEAD, C_VHEAD), F32)],
        compiler_params=_cparams(("parallel", "arbitrary"), vmem),
        name="retention",
    )(log_gamma, p, p, p, p, cc, ss, norm_g.reshape(1, C_VW))


def _rglru_kernel(x_ref, gate_ref, cw_ref, cp_ref, wr_ref, wi_ref, ybuf_ref, o_ref, px_ref, h_ref, *, tb, ts):
    del ybuf_ref
    i = pl.program_id(1)

    @pl.when(i == 0)
    def _():
        px_ref[...] = jnp.zeros_like(px_ref)
        h_ref[...] = jnp.zeros_like(h_ref)

    cw = cw_ref[...]
    cp = cp_ref[...]
    conv_b, b_r, b_i, lam = cp[0:1], cp[1:2], cp[2:3], cp[3:4]
    x = x_ref[...]
    ext = jnp.concatenate([px_ref[...], x], axis=0)
    xc = conv_b + x * cw[CONV_W - 1:CONV_W]
    for d in range(1, CONV_W):
        xc = xc + pltpu.roll(ext, d, 0)[SUBLANES:] * cw[CONV_W - 1 - d:CONV_W - d]
    px_ref[...] = x[tb - SUBLANES:tb]

    xcb = xc.astype(BF16)
    r = _sigmoid(_dot(xcb, wr_ref[0]) + b_r)
    ig = _sigmoid(_dot(xcb, wi_ref[0]) + b_i)
    log_a = -LRU_C * r * _softplus(-lam)
    a = jnp.exp(log_a)
    u = jnp.sqrt(-jnp.tanh(log_a) * (a * a + 1.0)) * (ig * xc)

    rows = lax.broadcasted_iota(jnp.int32, (tb, 1), 0) % ts
    d = 1
    while d < ts:
        keep = rows >= d
        a_sh = jnp.where(keep, pltpu.roll(a, d, 0), 1.0)
        u_sh = jnp.where(keep, pltpu.roll(u, d, 0), 0.0)
        u = a * u_sh + u
        a = a * a_sh
        d *= 2
    h_prev = h_ref[0:1, :]
    parts = []
    for c in range(tb // ts):
        h_c = a[c * ts:(c + 1) * ts] * h_prev + u[c * ts:(c + 1) * ts]
        h_prev = h_c[ts - 1:ts]
        parts.append(h_c)
    hh = jnp.concatenate(parts, axis=0) if len(parts) > 1 else parts[0]
    h_ref[...] = jnp.broadcast_to(h_prev, h_ref.shape)

    gt = gate_ref[...]
    gelu = 0.5 * gt * (1.0 + jnp.tanh(0.7978845608028654 * (gt + 0.044715 * (gt * gt * gt))))
    o_ref[...] = (hh * gelu).astype(o_ref.dtype)


def rglru_mix(p, conv_w, conv_b, w_r, b_r, w_i, b_i, lam, y_buf, tb=1024, ts=256):
    s = p.shape[0]
    tb = min(tb, s)
    ts = min(ts, tb)
    assert s % tb == 0 and tb % ts == 0
    xb0 = (2 * C_KW + 2 * C_VW) // D_BW
    gt0 = xb0 + D_W // D_BW
    out0 = C_VW // D_BW
    zeros = jnp.zeros((D_W,), F32)
    cp = jnp.stack([conv_b, b_r, b_i, lam, zeros, zeros, zeros, zeros])
    vmem = 2 * (2 * tb * D_BW * 4 + tb * D_BW * 2 + 2 * D_BW * D_BW * 2) + (24 << 20)
    return pl.pallas_call(
        functools.partial(_rglru_kernel, tb=tb, ts=ts),
        grid=(D_BLOCKS, s // tb),
        in_specs=[pl.BlockSpec((tb, D_BW), lambda n, i: (i, xb0 + n)),
                  pl.BlockSpec((tb, D_BW), lambda n, i: (i, gt0 + n)),
                  pl.BlockSpec((CONV_W, D_BW), lambda n, i: (0, n)),
                  pl.BlockSpec((SUBLANES, D_BW), lambda n, i: (0, n)),
                  pl.BlockSpec((1, D_BW, D_BW), lambda n, i: (n, 0, 0)),
                  pl.BlockSpec((1, D_BW, D_BW), lambda n, i: (n, 0, 0)),
                  pl.BlockSpec(memory_space=pl.ANY)],
        out_specs=pl.BlockSpec((tb, D_BW), lambda n, i: (i, out0 + n)),
        out_shape=jax.ShapeDtypeStruct(y_buf.shape, BF16),
        input_output_aliases={6: 0},
        scratch_shapes=[pltpu.VMEM((SUBLANES, D_BW), F32), pltpu.VMEM((SUBLANES, D_BW), F32)],
        compiler_params=_cparams(("parallel", "arbitrary"), vmem),
        name="rglru",
    )(p, p, conv_w, cp, w_r.astype(BF16), w_i.astype(BF16), y_buf)


def _xattn_block_kernel(h_ref, g_ref, wq_ref, k_ref, v_ref, wo_ref, o_ref, att_ref):
    x = h_ref[...]
    ms = jnp.mean(x * x, axis=-1, keepdims=True)
    xn = (x * lax.rsqrt(ms + NORM_EPS) * g_ref[...]).astype(BF16)
    q = _dot(xn, wq_ref[...]).astype(BF16)
    for hd in range(XA_H):
        sl = slice(hd * XA_HEAD, (hd + 1) * XA_HEAD)
        sc = _dot_nt(q[:, sl], k_ref[:, sl]) * (XA_HEAD ** -0.5)
        sc = sc - jnp.max(sc, axis=1, keepdims=True)
        e = jnp.exp(sc)
        pr = e / jnp.sum(e, axis=1, keepdims=True)
        att_ref[:, sl] = _dot(pr.astype(BF16), v_ref[:, sl]).astype(BF16)
    o_ref[...] = x + _dot(att_ref[...], wo_ref[...])


def xattn_block(h, g, wq, kmem, vmem_, wo, tm=256):
    s, d = h.shape
    nm = kmem.shape[0]
    tm = min(tm, s)
    const = dict(pipeline_mode=pl.Buffered(1))
    vmem = (4 * tm * d * 4 + 2 * d * XA_W * 2 + 2 * nm * XA_W * 2
            + tm * d * (2 + 4 + 4) + tm * XA_W * (4 + 2 + 2) + (8 << 20))
    return pl.pallas_call(
        _xattn_block_kernel,
        grid=(s // tm,),
        in_specs=[pl.BlockSpec((tm, d), lambda i: (i, 0)),
                  pl.BlockSpec((1, d), lambda i: (0, 0)),
                  pl.BlockSpec((d, XA_W), lambda i: (0, 0), **const),
                  pl.BlockSpec((nm, XA_W), lambda i: (0, 0), **const),
                  pl.BlockSpec((nm, XA_W), lambda i: (0, 0), **const),
                  pl.BlockSpec((XA_W, d), lambda i: (0, 0), **const)],
        out_specs=pl.BlockSpec((tm, d), lambda i: (i, 0)),
        out_shape=jax.ShapeDtypeStruct((s, d), F32),
        scratch_shapes=[pltpu.VMEM((tm, XA_W), BF16)],
        compiler_params=_cparams(("parallel",), vmem),
        name="xattn_block",
    )(h, g.reshape(1, d), wq, kmem, vmem_, wo)


def kernel(x, mem, mem_norm_g, norm_mix_g, norm_xattn_g, norm_mlp_g, xattn_wq, xattn_wk, xattn_wv, xattn_wo, mlp_w1, mlp_w2, even_w_in, even_w_out, rwkv_mu, rwkv_w0, rwkv_w_up, rwkv_a0, rwkv_a_up, rwkv_k_k, rwkv_k_a, rwkv_r_k, rwkv_ln_w, rwkv_ln_b, mlstm_i_b, mlstm_f_b, mlstm_norm_g, odd_w_in, odd_w_out, ret_norm_g, lru_conv_w, lru_conv_b, lru_w_r, lru_b_r, lru_w_i, lru_b_i, lru_lambda, final_norm_g):
    b, s, d = x.shape
    assert b == 1 and d == D_MODEL and s % 256 == 0
    h = x.reshape(s, d)
    mem_n = rmsnorm(mem.reshape(-1, d), mem_norm_g, BF16)
    wq_bf = xattn_wq.astype(BF16)
    wo_bf = xattn_wo.astype(BF16)
    even_wt = jnp.swapaxes(even_w_in, 1, 2)
    even_wt_g = jnp.pad(even_wt[:, A_COLS + EV_GATES:, :], ((0, 0), (0, LANES - 2 * B_H), (0, 0)))

    positions = jnp.arange(s, dtype=F32)
    inv_freq = ROPE_BASE ** (-jnp.arange(0, C_KHEAD, 2, dtype=F32) / C_KHEAD)
    ang = positions[:, None] * inv_freq[None, :]
    cos, sin = jnp.cos(ang), jnp.sin(ang)
    rope_cc = jnp.concatenate([cos, cos], axis=1)
    rope_ss = jnp.concatenate([-sin, sin], axis=1)
    log_gamma = jnp.log1p(-jnp.exp2(-5.0 - jnp.arange(C_H, dtype=F32)))

    for layer in range(DEPTH):
        j = layer // 2
        xn = rmsnorm(h, norm_mix_g[layer], BF16)
        if layer % 2 == 0:
            p_a = matmul(xn, even_wt, j, F32, n=EV_A_COLS, w_transposed=True)
            p_b = matmul(xn, even_wt, j, F32, n=4 * B_W, w_transposed=True, row0=A_COLS)
            p_g = matmul(xn, even_wt_g, j, F32, w_transposed=True)
            y = rwkv7_mix(p_a, rwkv_mu[j], rwkv_w0[j], rwkv_w_up[j], rwkv_a0[j], rwkv_a_up[j],
                          rwkv_k_k[j], rwkv_k_a[j], rwkv_r_k[j], rwkv_ln_w[j], rwkv_ln_b[j])
            gates_t = p_g[:, :2 * B_H].reshape(s // CHUNK, CHUNK, 2 * B_H).transpose(0, 2, 1)
            y = mlstm_mix(p_b, gates_t, mlstm_i_b[j], mlstm_f_b[j], mlstm_norm_g[j], y)
            w_out = even_w_out
        else:
            p = matmul(xn, odd_w_in, j, F32)
            y = retention_mix(p, rope_cc, rope_ss, log_gamma, ret_norm_g[j])
            y = rglru_mix(p, lru_conv_w[j], lru_conv_b[j], lru_w_r[j], lru_b_r[j],
                          lru_w_i[j], lru_b_i[j], lru_lambda[j], y)
            w_out = odd_w_out
        h = matmul(y, w_out, j, F32, epilogue="residual", residual=h)

        kmem = matmul(mem_n, xattn_wk, layer, BF16)
        vmem_ = matmul(mem_n, xattn_wv, layer, BF16)
        h = xattn_block(h, norm_xattn_g[layer], wq_bf[layer], kmem, vmem_, wo_bf[layer])

        xn = rmsnorm(h, norm_mlp_g[layer], BF16)
        u = matmul(xn, mlp_w1, layer, BF16, epilogue="relu2")
        h = matmul(u, mlp_w2, layer, F32, epilogue="residual", residual=h)

    return rmsnorm(h, final_norm_g, F32).reshape(b, s, d)
```

```python
import functools

import jax
import jax.numpy as jnp
from jax import lax
from jax.experimental import pallas as pl
from jax.experimental.pallas import tpu as pltpu

D_MODEL = 4096
DEPTH = 4
CHUNK = 64
NORM_EPS = 1e-6
MIX_W = D_MODEL
A_W = MIX_W // 2
A_HEAD = 64
A_H = A_W // A_HEAD
A_LORA = 96
A_LN_EPS = 64e-5
B_W = MIX_W // 2
B_H = 8
B_HEAD = B_W // B_H
GATE_CAP = 15.0
C_VW = MIX_W // 2
C_H = 8
C_VHEAD = C_VW // C_H
C_KHEAD = C_VHEAD // 2
C_KW = C_H * C_KHEAD
ROPE_BASE = 10000.0
D_W = MIX_W // 2
D_BLOCKS = 16
D_BW = D_W // D_BLOCKS
CONV_W = 4
LRU_C = 8.0
D_FF = 4 * D_MODEL
XA_H = 4
XA_HEAD = 256
XA_W = XA_H * XA_HEAD
A_COLS = 4 * A_W + 2 * A_LORA
B_COLS = 4 * B_W + 2 * B_H

LANES = 128
SUBLANES = 8
VMEM_BYTES_V7X = 64 * 1024 * 1024

EV_LORA = 4 * A_W
EV_A_COLS = EV_LORA + 4 * LANES
EV_GATES = 4 * B_W
assert A_COLS <= EV_LORA + 2 * LANES

BF16 = jnp.bfloat16
F32 = jnp.float32
EXP_M_HALF = 0.6065306597126334


def _cparams(dims, vmem_bytes):
    limit = min(int(vmem_bytes), VMEM_BYTES_V7X - 4 * 1024 * 1024)
    return pltpu.CompilerParams(dimension_semantics=dims, vmem_limit_bytes=limit)


def _dot(a, b):
    return jnp.dot(a, b, preferred_element_type=F32)


def _dot_nt(a, b):
    return lax.dot_general(a, b, (((1,), (1,)), ((), ())), preferred_element_type=F32)


def _dot_tn(a, b):
    return lax.dot_general(a, b, (((0,), (0,)), ((), ())), preferred_element_type=F32)


def _sigmoid(x):
    return 0.5 * jnp.tanh(0.5 * x) + 0.5


def _softplus(x):
    return jnp.maximum(x, 0.0) + jnp.log1p(jnp.exp(-jnp.abs(x)))


def _rmsnorm_kernel(x_ref, g_ref, o_ref):
    x = x_ref[...]
    ms = jnp.mean(x * x, axis=-1, keepdims=True)
    o_ref[...] = (x * lax.rsqrt(ms + NORM_EPS) * g_ref[...]).astype(o_ref.dtype)


def rmsnorm(x, g, out_dtype, tm=256):
    m, d = x.shape
    tm = min(tm, m)
    vmem = 2 * tm * d * (4 + jnp.dtype(out_dtype).itemsize) + 4 * d * 4 + (4 << 20)
    return pl.pallas_call(
        _rmsnorm_kernel,
        grid=(m // tm,),
        in_specs=[pl.BlockSpec((tm, d), lambda i: (i, 0)), pl.BlockSpec((1, d), lambda i: (0, 0))],
        out_specs=pl.BlockSpec((tm, d), lambda i: (i, 0)),
        out_shape=jax.ShapeDtypeStruct((m, d), out_dtype),
        compiler_params=_cparams(("parallel",), vmem),
        name="rmsnorm",
    )(x, g.reshape(1, d))


def _mm_kernel(*refs, nk, epilogue, w_transposed):
    if epilogue == "residual":
        x_ref, w_ref, r_ref, o_ref = refs
    else:
        x_ref, w_ref, o_ref = refs
        r_ref = None

    if nk > 1:
        @pl.when(pl.program_id(2) == 0)
        def _():
            o_ref[...] = r_ref[...]

    if w_transposed:
        part = _dot_nt(x_ref[...], w_ref[0].astype(BF16))
    else:
        part = _dot(x_ref[...], w_ref[...].astype(BF16))
    if nk > 1:
        o_ref[...] += part
    elif epilogue == "residual":
        o_ref[...] = r_ref[...] + part
    elif epilogue == "relu2":
        o_ref[...] = jnp.square(jnp.maximum(part, 0.0)).astype(o_ref.dtype)
    else:
        o_ref[...] = part.astype(o_ref.dtype)


def matmul(x, w, layer, out_dtype, n=None, w_transposed=False, row0=0, epilogue="none", residual=None,
           tm=2048, tn=None, tk=None):
    m, kdim = x.shape
    n_all = w.shape[1] if w_transposed else w.shape[2]
    n = n_all if n is None else n
    tm = min(tm, m)
    if tk is None:
        tk = kdim if kdim <= 4096 else 2048
    nk = kdim // tk
    single_x = nk == 1 and m > tm
    if tn is None:
        tn = 256 if (nk == 1 and not single_x) else 512
    tn = min(tn, n)
    assert m % tm == 0 and n % tn == 0 and kdim % tk == 0 and row0 + n <= n_all
    if w_transposed:
        w_spec = pl.BlockSpec((pl.Element(1), pl.Element(tn), pl.Element(tk)),
                              lambda i, j, k: (layer, pl.multiple_of(row0 + j * tn, SUBLANES),
                                               pl.multiple_of(k * tk, LANES)))
    else:
        assert row0 == 0
        w_spec = pl.BlockSpec((pl.Squeezed(), tk, tn), lambda i, j, k: (layer, k, j))
    if single_x:
        x_spec = pl.BlockSpec((tm, tk), lambda i, j, k: (i, k), pipeline_mode=pl.Buffered(1))
    else:
        x_spec = pl.BlockSpec((tm, tk), lambda i, j, k: (i, k))
    in_specs = [x_spec, w_spec]
    args = [x, w]
    osz = jnp.dtype(out_dtype).itemsize
    wsz = jnp.dtype(w.dtype).itemsize
    vmem = ((1 if single_x else 2) * tm * tk * 2 + 2 * (tk * tn * wsz + tm * tn * osz)
            + tk * tn * 2 + 2 * tm * tn * 4 + (8 << 20))
    if epilogue == "residual":
        in_specs.append(pl.BlockSpec((tm, tn), lambda i, j, k: (i, j)))
        args.append(residual)
        vmem += 2 * tm * tn * 4
    assert nk == 1 or (epilogue == "residual" and out_dtype == F32)
    return pl.pallas_call(
        functools.partial(_mm_kernel, nk=nk, epilogue=epilogue, w_transposed=w_transposed),
        grid=(m // tm, n // tn, nk),
        in_specs=in_specs,
        out_specs=pl.BlockSpec((tm, tn), lambda i, j, k: (i, j)),
        out_shape=jax.ShapeDtypeStruct((m, n), out_dtype),
        compiler_params=_cparams(("parallel", "arbitrary", "arbitrary"), vmem),
        name="matmul_" + epilogue,
    )(*args)


def _shift_rows(x, prev_ref, d):
    ext = jnp.concatenate([prev_ref[...], x], axis=0)
    return pltpu.roll(ext, d, 0)[SUBLANES:]


def _pair_sum(x, first):
    s1 = jnp.sum(jnp.where(first, x, 0.0), axis=1, keepdims=True)
    s2 = jnp.sum(jnp.where(first, 0.0, x), axis=1, keepdims=True)
    return jnp.where(first, s1, s2)


def _rwkv_kernel(r_ref, k_ref, v_ref, g_ref, lo_ref, cp_ref, mulo_ref, wup_ref, aup_ref, o_ref,
                 state_ref, pr_ref, pk_ref, pv_ref, pg_ref, plo_ref, *, tb):
    L = CHUNK
    i = pl.program_id(1)

    @pl.when(i == 0)
    def _():
        state_ref[...] = jnp.zeros_like(state_ref)
        pr_ref[...] = jnp.zeros_like(pr_ref)
        pk_ref[...] = jnp.zeros_like(pk_ref)
        pv_ref[...] = jnp.zeros_like(pv_ref)
        pg_ref[...] = jnp.zeros_like(pg_ref)
        plo_ref[...] = jnp.zeros_like(plo_ref)

    cp = cp_ref[...]
    mu_r, mu_k, mu_v, mu_g = cp[0:1], cp[1:2], cp[2:3], cp[3:4]
    w0, a0, k_k, k_a, r_k, ln_w, ln_b = (cp[4:5], cp[5:6], cp[6:7], cp[7:8], cp[8:9], cp[9:10], cp[10:11])

    first_row = lax.broadcasted_iota(jnp.int32, (tb, 1), 0) == 0

    def shift_lerp(x_ref, prev_ref, mu):
        x = x_ref[...]
        sh = jnp.where(first_row, prev_ref[SUBLANES - 1:SUBLANES, :], pltpu.roll(x, 1, 0))
        prev_ref[...] = x[tb - SUBLANES:tb]
        return x + (sh - x) * mu

    r = shift_lerp(r_ref, pr_ref, mu_r)
    k = shift_lerp(k_ref, pk_ref, mu_k)
    v = shift_lerp(v_ref, pv_ref, mu_v)
    g = shift_lerp(g_ref, pg_ref, mu_g)
    lo = shift_lerp(lo_ref, plo_ref, mulo_ref[...])

    lane = lax.broadcasted_iota(jnp.int32, (1, LANES), 1)
    first = lane < A_HEAD
    m1 = first.astype(F32)
    m2 = 1.0 - m1

    w = w0 + _dot(jnp.tanh(lo).astype(BF16), wup_ref[...])
    logw = -_sigmoid(w) * EXP_M_HALF
    a = _sigmoid(a0 + _dot(lo.astype(BF16), aup_ref[...]))
    kk = k * k_k
    k = k * (1.0 + (a - 1.0) * k_a)
    kk = kk * lax.rsqrt(_pair_sum(kk * kk, first) + 1e-12)
    alpha = -kk
    beta = kk * a
    bonus = _pair_sum(r * k * r_k, first) * v

    nc = tb // L
    row = lax.broadcasted_iota(jnp.int32, (L, L), 0)
    col = lax.broadcasted_iota(jnp.int32, (L, L), 1)
    tril_strict = row > col
    row2 = lax.broadcasted_iota(jnp.int32, (L, LANES), 0)
    col2 = lax.broadcasted_iota(jnp.int32, (L, LANES), 1) % L
    tril_incl2 = row2 >= col2
    bd_r = lax.broadcasted_iota(jnp.int32, (LANES, LANES), 0) // A_HEAD
    bd_c = lax.broadcasted_iota(jnp.int32, (LANES, LANES), 1) // A_HEAD
    blockdiag = bd_r == bd_c

    pos = lax.broadcasted_iota(jnp.int32, (tb, 1), 0) % L
    cum = logw
    step = 1
    while step < L:
        cum = cum + jnp.where(pos >= step, pltpu.roll(cum, step, 0), 0.0)
        step *= 2
    e_pos = jnp.exp(cum)
    e_neg = jnp.exp(-cum)
    ab = alpha * jnp.exp(cum - logw)
    rb = r * e_pos
    kt = (k * e_neg).astype(BF16)
    bt = (beta * e_neg).astype(BF16)
    vv = v.astype(BF16)
    heads = ((m1, ab * m1, (rb * m1).astype(BF16)), (m2, ab * m2, (rb * m2).astype(BF16)))
    zero_top = jnp.zeros((L, LANES), BF16)

    sls = [slice(c * L, (c + 1) * L) for c in range(nc)]
    kb = [jnp.concatenate([kt[sl], bt[sl]], axis=0) for sl in sls]
    xa = []
    for c, sl in enumerate(sls):
        x = jnp.concatenate([heads[0][1][sl].astype(BF16), heads[1][1][sl].astype(BF16),
                             heads[0][2][sl], heads[1][2][sl]], axis=0)
        xa.append(_dot_nt(x, kb[c]))
    prob = [(c, h) for c in range(nc) for h in range(2)]
    n_pow, ys, a_r = [], [], []
    for c, h in prob:
        an = xa[c][h * L:(h + 1) * L]
        a_ak = jnp.where(tril_strict, an[:, :L], 0.0).astype(BF16)
        n_pow.append(jnp.where(tril_strict, an[:, L:], 0.0).astype(BF16))
        a_r.append(jnp.where(tril_incl2, xa[c][(2 + h) * L:(3 + h) * L], 0.0).astype(BF16))
        ys.append(_dot(a_ak, vv[sls[c]]) * heads[h][0] + pltpu.roll(heads[h][1][sls[c]], A_HEAD, 1))
    for lvl in range(6):
        ys = [y + _dot(n, y.astype(BF16)) for n, y in zip(n_pow, ys)]
        if lvl < 5:
            n_pow = [_dot(n, n).astype(BF16) for n in n_pow]
    q_c, o0_c, g_c, pb_c = [], [], [], []
    for c, sl in enumerate(sls):
        top = jnp.concatenate([vv[sl], zero_top], axis=1)
        y2 = []
        o0 = jnp.zeros((L, LANES), F32)
        qq = rb[sl]
        for h in range(2):
            mh = heads[h][0]
            y = ys[2 * c + h]
            y = jnp.concatenate([y * mh, pltpu.roll(y * (1.0 - mh), A_HEAD, 1)], axis=1)
            y2.append(y)
            out = _dot(a_r[2 * c + h], jnp.concatenate([top, y.astype(BF16)], axis=0))
            o0 = o0 + out[:, :LANES] * mh
            qq = qq + out[:, LANES:]
        ysum = (y2[0] + y2[1]).astype(BF16)
        gp = _dot_tn(jnp.concatenate([top, ysum], axis=0), kb[c])
        g_c.append(jnp.where(blockdiag, gp[:LANES], 0.0))
        pb_c.append(jnp.where(blockdiag, gp[LANES:], 0.0))
        q_c.append(qq.astype(BF16))
        o0_c.append(o0)

    wl = [e_pos[(c + 1) * L - 1:(c + 1) * L] for c in range(nc)]
    eye_f = (lax.broadcasted_iota(jnp.int32, (LANES, LANES), 0)
             == lax.broadcasted_iota(jnp.int32, (LANES, LANES), 1)).astype(F32)
    pairs = []
    for a in range(0, nc, 2):
        b = a + 1
        pb_b = pb_c[b].astype(BF16)
        e_a = (eye_f + pb_c[a]) * wl[a]
        g_a = g_c[a] * wl[a]
        t = _dot(jnp.concatenate([e_a, g_a], axis=0).astype(BF16), pb_b)
        c_ab = (pb_c[a] * wl[a] + t[:LANES]) * wl[b]
        g_ab = (g_a + t[LANES:] + g_c[b]) * wl[b]
        rhs = jnp.concatenate([c_ab.astype(BF16), pb_c[a].astype(BF16)], axis=1)
        pairs.append((rhs, g_ab, wl[a] * wl[b]))

    state = state_ref[...]
    outs = []
    for pi, (rhs, g_ab, w_ab) in enumerate(pairs):
        a, b = 2 * pi, 2 * pi + 1
        st_b = state.astype(BF16)
        r = _dot(st_b, rhs)
        s_mid = (state + r[:, LANES:] + g_c[a]) * wl[a]
        outs.append(o0_c[a] + _dot_nt(q_c[a], st_b))
        outs.append(o0_c[b] + _dot_nt(q_c[b], s_mid.astype(BF16)))
        state = state * w_ab + r[:, :LANES] + g_ab
    state_ref[...] = state
    o = jnp.concatenate(outs, axis=0)

    mean = _pair_sum(o, first) * (1.0 / A_HEAD)
    d = o - mean
    var = _pair_sum(d * d, first) * (1.0 / A_HEAD)
    y = d * lax.rsqrt(var + A_LN_EPS) * ln_w + ln_b
    o_ref[...] = ((y + bonus) * _sigmoid(g)).astype(o_ref.dtype)


def rwkv7_mix(p, mu, w0, w_up, a0, a_up, k_k, k_a, r_k, ln_w, ln_b, tb=1024):
    s = p.shape[0]
    tb = min(tb, s)
    assert s % tb == 0
    npair = A_H // 2
    nblk = A_W // LANES
    zeros = jnp.zeros((A_W,), F32)
    cp = jnp.stack([mu[0:A_W], mu[A_W:2 * A_W], mu[2 * A_W:3 * A_W], mu[3 * A_W:4 * A_W],
                    w0, a0, k_k, k_a, r_k, ln_w, ln_b, zeros, zeros, zeros, zeros, zeros])
    tail = 2 * LANES - 2 * A_LORA
    mulo = jnp.concatenate([mu[4 * A_W:], jnp.zeros((tail,), F32)]).reshape(1, 2 * LANES)
    wup = jnp.concatenate([w_up, jnp.zeros((2 * LANES - A_LORA, A_W), F32)], axis=0).astype(BF16)
    aup = jnp.concatenate([jnp.zeros((A_LORA, A_W), F32), a_up, jnp.zeros((tail, A_W), F32)], axis=0).astype(BF16)
    lora_blk = EV_LORA // (2 * LANES)

    def col(j):
        return pl.BlockSpec((tb, LANES), lambda hp, i, j=j: (i, j * nblk + hp))

    vmem = 2 * (4 * tb * LANES * 4 + tb * 2 * LANES * 4 + tb * LANES * 2) + (24 << 20)
    return pl.pallas_call(
        functools.partial(_rwkv_kernel, tb=tb),
        grid=(npair, s // tb),
        in_specs=[col(0), col(1), col(2), col(3),
                  pl.BlockSpec((tb, 2 * LANES), lambda hp, i: (i, lora_blk)),
                  pl.BlockSpec((16, LANES), lambda hp, i: (0, hp)),
                  pl.BlockSpec((1, 2 * LANES), lambda hp, i: (0, 0)),
                  pl.BlockSpec((2 * LANES, LANES), lambda hp, i: (0, hp)),
                  pl.BlockSpec((2 * LANES, LANES), lambda hp, i: (0, hp))],
        out_specs=pl.BlockSpec((tb, LANES), lambda hp, i: (i, hp)),
        out_shape=jax.ShapeDtypeStruct((s, MIX_W), BF16),
        scratch_shapes=[pltpu.VMEM((LANES, LANES), F32),
                        pltpu.VMEM((SUBLANES, LANES), F32), pltpu.VMEM((SUBLANES, LANES), F32),
                        pltpu.VMEM((SUBLANES, LANES), F32), pltpu.VMEM((SUBLANES, LANES), F32),
                        pltpu.VMEM((SUBLANES, 2 * LANES), F32)],
        compiler_params=_cparams(("parallel", "arbitrary"), vmem),
        name="rwkv7",
    )(p, p, p, p, p, cp, mulo, wup, aup)


def _mlstm_kernel(ib_ref, fb_ref, q_ref, k_ref, v_ref, og_ref, gt_ref, ng_ref, ybuf_ref, o_ref,
                  c_ref, n_ref, m_ref, *, tb):
    del ybuf_ref
    L = CHUNK
    h = pl.program_id(0)
    i = pl.program_id(1)

    @pl.when(i == 0)
    def _():
        c_ref[...] = jnp.zeros_like(c_ref)
        n_ref[...] = jnp.zeros_like(n_ref)
        m_ref[...] = jnp.zeros_like(m_ref)

    i_b = ib_ref[h]
    f_b = fb_ref[h]
    row = lax.broadcasted_iota(jnp.int32, (L, L), 0)
    col = lax.broadcasted_iota(jnp.int32, (L, L), 1)
    causal = row >= col
    eye = row == col

    def to_col(x_row):
        return jnp.sum(jnp.where(eye, jnp.broadcast_to(x_row, (L, L)), 0.0), axis=1, keepdims=True)

    pre = []
    for c in range(tb // L):
        sl = slice(c * L, (c + 1) * L)
        gi = gt_ref[c, pl.ds(h, 1), :]
        gf = gt_ref[c, pl.ds(B_H + h, 1), :]
        li_row = GATE_CAP * jnp.tanh((gi + i_b) / GATE_CAP)
        lf_row = -_softplus(-(GATE_CAP * jnp.tanh((gf + f_b) / GATE_CAP)))
        li_col = to_col(li_row)
        lf_col = to_col(lf_row)
        cum_col = jnp.sum(jnp.where(causal, jnp.broadcast_to(lf_row, (L, L)), 0.0), axis=1, keepdims=True)
        cum_row = jnp.sum(jnp.where(row <= col, jnp.broadcast_to(lf_col, (L, L)), 0.0), axis=0, keepdims=True)
        total = jnp.sum(lf_row, axis=1, keepdims=True)

        qf = q_ref[sl, :]
        q = qf.astype(BF16)
        kf = k_ref[sl, :] * (B_HEAD ** -0.5)
        vb = v_ref[sl, :].astype(BF16)

        d_log = jnp.where(causal, cum_col - cum_row + li_row, -jnp.inf)
        m_loc = jnp.max(d_log, axis=1, keepdims=True)
        s_loc = _dot_nt(q, kf.astype(BF16)) * jnp.exp(d_log - m_loc)
        sv = _dot(s_loc.astype(BF16), vb)
        s_sum = jnp.sum(s_loc, axis=1, keepdims=True)

        log_w_col = total - cum_col + li_col
        m_k = jnp.max(log_w_col, axis=0, keepdims=True)
        kw = kf * jnp.exp(log_w_col - m_k)
        kv = _dot_tn(kw.astype(BF16), vb)
        k_sum = jnp.sum(kw, axis=0, keepdims=True)
        wide = lambda t: jnp.broadcast_to(t, (t.shape[0], B_HEAD))
        pre.append((qf, q, wide(cum_col), wide(total), wide(m_loc), sv, wide(s_sum), wide(m_k), kv, k_sum))

    c_st = c_ref[...]
    n_st = n_ref[...]
    m_st = m_ref[...]
    outs = []
    for qf, q, cum_b, total_b, m_loc_b, sv, s_sum_b, m_k_b, kv, k_sum in pre:
        m_inter = cum_b + m_st
        m_row = jnp.maximum(m_loc_b, m_inter)
        w_loc = jnp.exp(m_loc_b - m_row)
        w_inter = jnp.exp(m_inter - m_row)
        num = w_loc * sv + w_inter * _dot(q, c_st.astype(BF16))
        den = w_loc * s_sum_b + w_inter * jnp.sum(qf * n_st, axis=1, keepdims=True)
        outs.append(num / jnp.maximum(jnp.abs(den), jnp.exp(-m_row)))

        m_new = jnp.maximum(total_b + m_st, m_k_b)
        carry = jnp.exp(total_b + m_st - m_new)
        w_new = jnp.exp(m_k_b - m_new)
        c_st = carry * c_st + w_new * kv
        n_st = carry * n_st + w_new * k_sum
        m_st = m_new
    c_ref[...] = c_st
    n_ref[...] = n_st
    m_ref[...] = m_st

    hh = jnp.concatenate(outs, axis=0) if len(outs) > 1 else outs[0]
    ms = jnp.mean(hh * hh, axis=1, keepdims=True)
    y = hh * lax.rsqrt(ms + NORM_EPS) * ng_ref[...]
    o_ref[...] = (y * _sigmoid(og_ref[...])).astype(o_ref.dtype)


def mlstm_mix(p, gates_t, i_b, f_b, norm_g, y_buf, tb=1024):
    s = p.shape[0]
    tb = min(tb, s)
    assert s % tb == 0
    nblk = B_W // B_HEAD
    out0 = A_W // B_HEAD

    def col(j):
        return pl.BlockSpec((tb, B_HEAD), lambda h, i, j=j: (i, j * nblk + h))

    smem = pl.BlockSpec(memory_space=pltpu.SMEM)
    vmem = 2 * (4 * tb * B_HEAD * 4 + tb * B_HEAD * 2) + (24 << 20)
    return pl.pallas_call(
        functools.partial(_mlstm_kernel, tb=tb),
        grid=(B_H, s // tb),
        in_specs=[smem, smem, col(0), col(1), col(2), col(3),
                  pl.BlockSpec((tb // CHUNK, 2 * B_H, CHUNK), lambda h, i: (i, 0, 0)),
                  pl.BlockSpec((1, B_HEAD), lambda h, i: (0, h)),
                  pl.BlockSpec(memory_space=pl.ANY)],
        out_specs=pl.BlockSpec((tb, B_HEAD), lambda h, i: (i, out0 + h)),
        out_shape=jax.ShapeDtypeStruct(y_buf.shape, BF16),
        input_output_aliases={8: 0},
        scratch_shapes=[pltpu.VMEM((B_HEAD, B_HEAD), F32), pltpu.VMEM((1, B_HEAD), F32),
                        pltpu.VMEM((1, B_HEAD), F32)],
        compiler_params=_cparams(("parallel", "arbitrary"), vmem),
        name="mlstm",
    )(i_b, f_b, p, p, p, p, gates_t, norm_g.reshape(1, B_W), y_buf)


def _retention_kernel(lg_ref, q_ref, k_ref, v_ref, g_ref, cc_ref, ss_ref, ng_ref, o_ref, r_ref, *, tb):
    L = CHUNK
    h = pl.program_id(0)
    i = pl.program_id(1)

    @pl.when(i == 0)
    def _():
        r_ref[...] = jnp.zeros_like(r_ref)

    lg = lg_ref[h]
    row = lax.broadcasted_iota(jnp.int32, (L, L), 0)
    col = lax.broadcasted_iota(jnp.int32, (L, L), 1)
    intra = jnp.exp(lg * jnp.abs(row - col).astype(F32))
    pos = lax.broadcasted_iota(jnp.int32, (L, 1), 0).astype(F32)
    q_decay = jnp.exp(lg * (pos + 1.0))
    k_decay = jnp.exp(lg * (L - 1.0 - pos))
    chunk_decay = jnp.exp(jnp.full((1, C_VHEAD), lg * L, F32))

    cc = cc_ref[...]
    ss = ss_ref[...]

    def rope(t):
        return t * cc + pltpu.roll(t, C_KHEAD // 2, 1) * ss

    q = rope(q_ref[...]).astype(BF16)
    k = rope(k_ref[...]) * (C_KHEAD ** -0.5)
    vb = v_ref[...].astype(BF16)
    sls = [slice(c * L, (c + 1) * L) for c in range(tb // L)]
    sv = [_dot((_dot_nt(q[sl], k[sl].astype(BF16)) * intra).astype(BF16), vb[sl]) for sl in sls]
    kv = [_dot_tn((k[sl] * k_decay).astype(BF16), vb[sl]) for sl in sls]
    r_st = r_ref[...]
    outs = []
    for c, sl in enumerate(sls):
        outs.append(sv[c] + q_decay * _dot(q[sl], r_st.astype(BF16)))
        r_st = chunk_decay * r_st + kv[c]
    r_ref[...] = r_st
    o = jnp.concatenate(outs, axis=0) if len(outs) > 1 else outs[0]
    ms = jnp.mean(o * o, axis=1, keepdims=True)
    y = o * lax.rsqrt(ms + NORM_EPS) * ng_ref[...]
    g = g_ref[...]
    o_ref[...] = (y * (g * _sigmoid(g))).astype(o_ref.dtype)


def retention_mix(p, cc, ss, log_gamma, norm_g, tb=1024):
    s = p.shape[0]
    tb = min(tb, s)
    assert s % tb == 0
    kb = C_KW // C_KHEAD
    vb = (2 * C_KW) // C_VHEAD
    gb = (2 * C_KW + C_VW) // C_VHEAD
    smem = pl.BlockSpec(memory_space=pltpu.SMEM)
    vmem = 2 * (2 * tb * C_KHEAD * 4 + 2 * tb * C_VHEAD * 4 + 2 * tb * C_KHEAD * 4 + tb * C_VHEAD * 2) + (24 << 20)
    return pl.pallas_call(
        functools.partial(_retention_kernel, tb=tb),
        grid=(C_H, s // tb),
        in_specs=[smem,
                  pl.BlockSpec((tb, C_KHEAD), lambda h, i: (i, h)),
                  pl.BlockSpec((tb, C_KHEAD), lambda h, i: (i, kb + h)),
                  pl.BlockSpec((tb, C_VHEAD), lambda h, i: (i, vb + h)),
                  pl.BlockSpec((tb, C_VHEAD), lambda h, i: (i, gb + h)),
                  pl.BlockSpec((tb, C_KHEAD), lambda h, i: (i, 0)),
                  pl.BlockSpec((tb, C_KHEAD), lambda h, i: (i, 0)),
                  pl.BlockSpec((1, C_VHEAD), lambda h, i: (0, h))],
        out_specs=pl.BlockSpec((tb, C_VHEAD), lambda h, i: (i, h)),
        out_shape=jax.ShapeDtypeStruct((s, MIX_W), BF16),
        scratch_shapes=[pltpu.VMEM((C_KHEAD, C_VHEAD), F32)],
        compiler_params=_cparams(("parallel", "arbitrary"), vmem),
        name="retention",
    )(log_gamma, p, p, p, p, cc, ss, norm_g.reshape(1, C_VW))


def _rglru_kernel(x_ref, gate_ref, cw_ref, cp_ref, wr_ref, wi_ref, ybuf_ref, o_ref, px_ref, h_ref, *, tb, ts):
    del ybuf_ref
    i = pl.program_id(1)

    @pl.when(i == 0)
    def _():
        px_ref[...] = jnp.zeros_like(px_ref)
        h_ref[...] = jnp.zeros_like(h_ref)

    cw = cw_ref[...]
    cp = cp_ref[...]
    conv_b, b_r, b_i, lam = cp[0:1], cp[1:2], cp[2:3], cp[3:4]
    x = x_ref[...]
    ext = jnp.concatenate([px_ref[...], x], axis=0)
    xc = conv_b + x * cw[CONV_W - 1:CONV_W]
    for d in range(1, CONV_W):
        xc = xc + pltpu.roll(ext, d, 0)[SUBLANES:] * cw[CONV_W - 1 - d:CONV_W - d]
    px_ref[...] = x[tb - SUBLANES:tb]

    xcb = xc.astype(BF16)
    r = _sigmoid(_dot(xcb, wr_ref[0]) + b_r)
    ig = _sigmoid(_dot(xcb, wi_ref[0]) + b_i)
    log_a = -LRU_C * r * _softplus(-lam)
    a = jnp.exp(log_a)
    u = jnp.sqrt(-jnp.tanh(log_a) * (a * a + 1.0)) * (ig * xc)

    rows = lax.broadcasted_iota(jnp.int32, (tb, 1), 0) % ts
    d = 1
    while d < ts:
        keep = rows >= d
        a_sh = jnp.where(keep, pltpu.roll(a, d, 0), 1.0)
        u_sh = jnp.where(keep, pltpu.roll(u, d, 0), 0.0)
        u = a * u_sh + u
        a = a * a_sh
        d *= 2
    h_prev = h_ref[0:1, :]
    parts = []
    for c in range(tb // ts):
        h_c = a[c * ts:(c + 1) * ts] * h_prev + u[c * ts:(c + 1) * ts]
        h_prev = h_c[ts - 1:ts]
        parts.append(h_c)
    hh = jnp.concatenate(parts, axis=0) if len(parts) > 1 else parts[0]
    h_ref[...] = jnp.broadcast_to(h_prev, h_ref.shape)

    gt = gate_ref[...]
    gelu = 0.5 * gt * (1.0 + jnp.tanh(0.7978845608028654 * (gt + 0.044715 * (gt * gt * gt))))
    o_ref[...] = (hh * gelu).astype(o_ref.dtype)


def rglru_mix(p, conv_w, conv_b, w_r, b_r, w_i, b_i, lam, y_buf, tb=1024, ts=256):
    s = p.shape[0]
    tb = min(tb, s)
    ts = min(ts, tb)
    assert s % tb == 0 and tb % ts == 0
    xb0 = (2 * C_KW + 2 * C_VW) // D_BW
    gt0 = xb0 + D_W // D_BW
    out0 = C_VW // D_BW
    zeros = jnp.zeros((D_W,), F32)
    cp = jnp.stack([conv_b, b_r, b_i, lam, zeros, zeros, zeros, zeros])
    vmem = 2 * (2 * tb * D_BW * 4 + tb * D_BW * 2 + 2 * D_BW * D_BW * 2) + (24 << 20)
    return pl.pallas_call(
        functools.partial(_rglru_kernel, tb=tb, ts=ts),
        grid=(D_BLOCKS, s // tb),
        in_specs=[pl.BlockSpec((tb, D_BW), lambda n, i: (i, xb0 + n)),
                  pl.BlockSpec((tb, D_BW), lambda n, i: (i, gt0 + n)),
                  pl.BlockSpec((CONV_W, D_BW), lambda n, i: (0, n)),
                  pl.BlockSpec((SUBLANES, D_BW), lambda n, i: (0, n)),
                  pl.BlockSpec((1, D_BW, D_BW), lambda n, i: (n, 0, 0)),
                  pl.BlockSpec((1, D_BW, D_BW), lambda n, i: (n, 0, 0)),
                  pl.BlockSpec(memory_space=pl.ANY)],
        out_specs=pl.BlockSpec((tb, D_BW), lambda n, i: (i, out0 + n)),
        out_shape=jax.ShapeDtypeStruct(y_buf.shape, BF16),
        input_output_aliases={6: 0},
        scratch_shapes=[pltpu.VMEM((SUBLANES, D_BW), F32), pltpu.VMEM((SUBLANES, D_BW), F32)],
        compiler_params=_cparams(("parallel", "arbitrary"), vmem),
        name="rglru",
    )(p, p, conv_w, cp, w_r.astype(BF16), w_i.astype(BF16), y_buf)


def _xattn_block_kernel(h_ref, g_ref, g2_ref, wq_ref, k_ref, v_ref, wo_ref, o_ref, xn2_ref, att_ref):
    x = h_ref[...]
    ms = jnp.mean(x * x, axis=-1, keepdims=True)
    xn = (x * lax.rsqrt(ms + NORM_EPS) * g_ref[...]).astype(BF16)
    q = _dot(xn, wq_ref[...]).astype(BF16)
    for hd in range(XA_H):
        sl = slice(hd * XA_HEAD, (hd + 1) * XA_HEAD)
        sc = _dot_nt(q[:, sl], k_ref[:, sl]) * (XA_HEAD ** -0.5)
        sc = sc - jnp.max(sc, axis=1, keepdims=True)
        e = jnp.exp(sc)
        pr = e / jnp.sum(e, axis=1, keepdims=True)
        att_ref[:, sl] = _dot(pr.astype(BF16), v_ref[:, sl]).astype(BF16)
    h_new = x + _dot(att_ref[...], wo_ref[...])
    o_ref[...] = h_new
    ms2 = jnp.mean(h_new * h_new, axis=-1, keepdims=True)
    xn2_ref[...] = (h_new * lax.rsqrt(ms2 + NORM_EPS) * g2_ref[...]).astype(xn2_ref.dtype)


def xattn_block(h, g, g_next, wq, kmem, vmem_, wo, tm=256):
    s, d = h.shape
    nm = kmem.shape[0]
    tm = min(tm, s)
    const = dict(pipeline_mode=pl.Buffered(1))
    vmem = (4 * tm * d * 4 + 2 * tm * d * 2 + 2 * d * XA_W * 2 + 2 * nm * XA_W * 2
            + tm * d * (2 + 4 + 4) + tm * XA_W * (4 + 2 + 2) + (8 << 20))
    row = pl.BlockSpec((tm, d), lambda i: (i, 0))
    gain = pl.BlockSpec((1, d), lambda i: (0, 0))
    return pl.pallas_call(
        _xattn_block_kernel,
        grid=(s // tm,),
        in_specs=[row, gain, gain,
                  pl.BlockSpec((d, XA_W), lambda i: (0, 0), **const),
                  pl.BlockSpec((nm, XA_W), lambda i: (0, 0), **const),
                  pl.BlockSpec((nm, XA_W), lambda i: (0, 0), **const),
                  pl.BlockSpec((XA_W, d), lambda i: (0, 0), **const)],
        out_specs=[row, row],
        out_shape=[jax.ShapeDtypeStruct((s, d), F32), jax.ShapeDtypeStruct((s, d), BF16)],
        scratch_shapes=[pltpu.VMEM((tm, XA_W), BF16)],
        compiler_params=_cparams(("parallel",), vmem),
        name="xattn_block",
    )(h, g.reshape(1, d), g_next.reshape(1, d), wq, kmem, vmem_, wo)


def kernel(x, mem, mem_norm_g, norm_mix_g, norm_xattn_g, norm_mlp_g, xattn_wq, xattn_wk, xattn_wv, xattn_wo, mlp_w1, mlp_w2, even_w_in, even_w_out, rwkv_mu, rwkv_w0, rwkv_w_up, rwkv_a0, rwkv_a_up, rwkv_k_k, rwkv_k_a, rwkv_r_k, rwkv_ln_w, rwkv_ln_b, mlstm_i_b, mlstm_f_b, mlstm_norm_g, odd_w_in, odd_w_out, ret_norm_g, lru_conv_w, lru_conv_b, lru_w_r, lru_b_r, lru_w_i, lru_b_i, lru_lambda, final_norm_g):
    b, s, d = x.shape
    assert b == 1 and d == D_MODEL and s % 256 == 0
    h = x.reshape(s, d)
    mem_n = rmsnorm(mem.reshape(-1, d), mem_norm_g, BF16)
    wq_bf = xattn_wq.astype(BF16)
    wo_bf = xattn_wo.astype(BF16)
    even_wt = jnp.swapaxes(even_w_in, 1, 2)
    even_wt_g = jnp.pad(even_wt[:, A_COLS + EV_GATES:, :], ((0, 0), (0, LANES - 2 * B_H), (0, 0)))

    positions = jnp.arange(s, dtype=F32)
    inv_freq = ROPE_BASE ** (-jnp.arange(0, C_KHEAD, 2, dtype=F32) / C_KHEAD)
    ang = positions[:, None] * inv_freq[None, :]
    cos, sin = jnp.cos(ang), jnp.sin(ang)
    rope_cc = jnp.concatenate([cos, cos], axis=1)
    rope_ss = jnp.concatenate([-sin, sin], axis=1)
    log_gamma = jnp.log1p(-jnp.exp2(-5.0 - jnp.arange(C_H, dtype=F32)))

    for layer in range(DEPTH):
        j = layer // 2
        xn = rmsnorm(h, norm_mix_g[layer], BF16)
        if layer % 2 == 0:
            p_a = matmul(xn, even_wt, j, F32, n=EV_A_COLS, w_transposed=True)
            p_b = matmul(xn, even_wt, j, F32, n=4 * B_W, w_transposed=True, row0=A_COLS)
            p_g = matmul(xn, even_wt_g, j, F32, w_transposed=True)
            y = rwkv7_mix(p_a, rwkv_mu[j], rwkv_w0[j], rwkv_w_up[j], rwkv_a0[j], rwkv_a_up[j],
                          rwkv_k_k[j], rwkv_k_a[j], rwkv_r_k[j], rwkv_ln_w[j], rwkv_ln_b[j])
            gates_t = p_g[:, :2 * B_H].reshape(s // CHUNK, CHUNK, 2 * B_H).transpose(0, 2, 1)
            y = mlstm_mix(p_b, gates_t, mlstm_i_b[j], mlstm_f_b[j], mlstm_norm_g[j], y)
            w_out = even_w_out
        else:
            p = matmul(xn, odd_w_in, j, F32)
            y = retention_mix(p, rope_cc, rope_ss, log_gamma, ret_norm_g[j])
            y = rglru_mix(p, lru_conv_w[j], lru_conv_b[j], lru_w_r[j], lru_b_r[j],
                          lru_w_i[j], lru_b_i[j], lru_lambda[j], y)
            w_out = odd_w_out
        h = matmul(y, w_out, j, F32, epilogue="residual", residual=h)

        kmem = matmul(mem_n, xattn_wk, layer, BF16)
        vmem_ = matmul(mem_n, xattn_wv, layer, BF16)
        h, xn = xattn_block(h, norm_xattn_g[layer], norm_mlp_g[layer], wq_bf[layer], kmem, vmem_, wo_bf[layer])
        u = matmul(xn, mlp_w1, layer, BF16, epilogue="relu2")
        h = matmul(u, mlp_w2, layer, F32, epilogue="residual", residual=h)

    return rmsnorm(h, final_norm_g, F32).reshape(b, s, d)
```

```python
import functools

import jax
import jax.numpy as jnp
from jax import lax
from jax.experimental import pallas as pl
from jax.experimental.pallas import tpu as pltpu

D_MODEL = 4096
DEPTH = 4
CHUNK = 64
NORM_EPS = 1e-6
MIX_W = D_MODEL
A_W = MIX_W // 2
A_HEAD = 64
A_H = A_W // A_HEAD
A_LORA = 96
A_LN_EPS = 64e-5
B_W = MIX_W // 2
B_H = 8
B_HEAD = B_W // B_H
GATE_CAP = 15.0
C_VW = MIX_W // 2
C_H = 8
C_VHEAD = C_VW // C_H
C_KHEAD = C_VHEAD // 2
C_KW = C_H * C_KHEAD
ROPE_BASE = 10000.0
D_W = MIX_W // 2
D_BLOCKS = 16
D_BW = D_W // D_BLOCKS
CONV_W = 4
LRU_C = 8.0
D_FF = 4 * D_MODEL
XA_H = 4
XA_HEAD = 256
XA_W = XA_H * XA_HEAD
A_COLS = 4 * A_W + 2 * A_LORA
B_COLS = 4 * B_W + 2 * B_H

LANES = 128
SUBLANES = 8
VMEM_BYTES_V7X = 64 * 1024 * 1024

EVEN_COLS = A_COLS + B_COLS
EV_TN = 512
EV_LORA = 4 * A_W
EV_A_COLS = EV_LORA + EV_TN
EV_GATE_TILE = (EV_A_COLS + 4 * B_W) // EV_TN
EV_COLS = (EV_GATE_TILE + 1) * EV_TN
assert A_COLS <= EV_LORA + 2 * LANES and EV_A_COLS % B_HEAD == 0

BF16 = jnp.bfloat16
F32 = jnp.float32
EXP_M_HALF = 0.6065306597126334


def _cparams(dims, vmem_bytes):
    limit = min(int(vmem_bytes), VMEM_BYTES_V7X - 4 * 1024 * 1024)
    return pltpu.CompilerParams(dimension_semantics=dims, vmem_limit_bytes=limit)


def _dot(a, b):
    return jnp.dot(a, b, preferred_element_type=F32)


def _dot_nt(a, b):
    return lax.dot_general(a, b, (((1,), (1,)), ((), ())), preferred_element_type=F32)


def _dot_tn(a, b):
    return lax.dot_general(a, b, (((0,), (0,)), ((), ())), preferred_element_type=F32)


def _sigmoid(x):
    return 0.5 * jnp.tanh(0.5 * x) + 0.5


def _softplus(x):
    return jnp.maximum(x, 0.0) + jnp.log1p(jnp.exp(-jnp.abs(x)))


def _rmsnorm_kernel(x_ref, g_ref, o_ref):
    x = x_ref[...]
    ms = jnp.mean(x * x, axis=-1, keepdims=True)
    o_ref[...] = (x * lax.rsqrt(ms + NORM_EPS) * g_ref[...]).astype(o_ref.dtype)


def rmsnorm(x, g, out_dtype, tm=512):
    m, d = x.shape
    tm = min(tm, m)
    vmem = 2 * tm * d * (4 + jnp.dtype(out_dtype).itemsize) + tm * d * 4 + 4 * d * 4 + (4 << 20)
    return pl.pallas_call(
        _rmsnorm_kernel,
        grid=(m // tm,),
        in_specs=[pl.BlockSpec((tm, d), lambda i: (i, 0)), pl.BlockSpec((1, d), lambda i: (0, 0))],
        out_specs=pl.BlockSpec((tm, d), lambda i: (i, 0)),
        out_shape=jax.ShapeDtypeStruct((m, d), out_dtype),
        compiler_params=_cparams(("parallel",), vmem),
        name="rmsnorm",
    )(x, g.reshape(1, d))


def _mm_kernel(*refs, nk, epilogue, w_transposed):
    if epilogue == "residual":
        x_ref, w_ref, r_ref, o_ref = refs
    else:
        x_ref, w_ref, o_ref = refs
        r_ref = None

    if nk > 1:
        @pl.when(pl.program_id(2) == 0)
        def _():
            o_ref[...] = r_ref[...]

    if w_transposed:
        part = _dot_nt(x_ref[...], w_ref[0].astype(BF16))
    else:
        part = _dot(x_ref[...], w_ref[...].astype(BF16))
    if nk > 1:
        o_ref[...] += part
    elif epilogue == "residual":
        o_ref[...] = r_ref[...] + part
    elif epilogue == "relu2":
        o_ref[...] = jnp.square(jnp.maximum(part, 0.0)).astype(o_ref.dtype)
    else:
        o_ref[...] = part.astype(o_ref.dtype)


def matmul(x, w, layer, out_dtype, n=None, w_transposed=False, segments=((0, 0),), epilogue="none", residual=None,
           tm=2048, tn=None, tk=None):
    m, kdim = x.shape
    n_all = w.shape[1] if w_transposed else w.shape[2]
    n = n_all if n is None else n
    tm = min(tm, m)
    if tk is None:
        tk = kdim if kdim <= 4096 else 2048
    nk = kdim // tk
    single_x = nk == 1 and m > tm
    if tn is None:
        tn = 256 if (nk == 1 and not single_x) else 512
    tn = min(tn, n)
    assert m % tm == 0 and n % tn == 0 and kdim % tk == 0
    if w_transposed:
        ends = [jb for jb, _ in segments[1:]] + [n // tn]
        assert all(r0 % SUBLANES == 0 and r0 + (je - jb) * tn <= n_all for (jb, r0), je in zip(segments, ends))

        def w_row(j):
            row = segments[0][1] + j * tn
            for jb, r0 in segments[1:]:
                row = jnp.where(j >= jb, r0 + (j - jb) * tn, row)
            return pl.multiple_of(row, SUBLANES)

        w_spec = pl.BlockSpec((pl.Element(1), pl.Element(tn), pl.Element(tk)),
                              lambda i, j, k: (layer, w_row(j), pl.multiple_of(k * tk, LANES)))
    else:
        assert tuple(segments) == ((0, 0),)
        w_spec = pl.BlockSpec((pl.Squeezed(), tk, tn), lambda i, j, k: (layer, k, j))
    if single_x:
        x_spec = pl.BlockSpec((tm, tk), lambda i, j, k: (i, k), pipeline_mode=pl.Buffered(1))
    else:
        x_spec = pl.BlockSpec((tm, tk), lambda i, j, k: (i, k))
    in_specs = [x_spec, w_spec]
    args = [x, w]
    osz = jnp.dtype(out_dtype).itemsize
    wsz = jnp.dtype(w.dtype).itemsize
    vmem = ((1 if single_x else 2) * tm * tk * 2 + 2 * (tk * tn * wsz + tm * tn * osz)
            + tk * tn * 2 + 2 * tm * tn * 4 + (8 << 20))
    if epilogue == "residual":
        in_specs.append(pl.BlockSpec((tm, tn), lambda i, j, k: (i, j)))
        args.append(residual)
        vmem += 2 * tm * tn * 4
    assert nk == 1 or (epilogue == "residual" and out_dtype == F32)
    return pl.pallas_call(
        functools.partial(_mm_kernel, nk=nk, epilogue=epilogue, w_transposed=w_transposed),
        grid=(m // tm, n // tn, nk),
        in_specs=in_specs,
        out_specs=pl.BlockSpec((tm, tn), lambda i, j, k: (i, j)),
        out_shape=jax.ShapeDtypeStruct((m, n), out_dtype),
        compiler_params=_cparams(("parallel", "arbitrary", "arbitrary"), vmem),
        name="matmul_" + epilogue,
    )(*args)


def _shift_rows(x, prev_ref, d):
    ext = jnp.concatenate([prev_ref[...], x], axis=0)
    return pltpu.roll(ext, d, 0)[SUBLANES:]


def _pair_sum(x, first):
    s1 = jnp.sum(jnp.where(first, x, 0.0), axis=1, keepdims=True)
    s2 = jnp.sum(jnp.where(first, 0.0, x), axis=1, keepdims=True)
    return jnp.where(first, s1, s2)


def _rwkv_kernel(r_ref, k_ref, v_ref, g_ref, lo_ref, cp_ref, mulo_ref, wup_ref, aup_ref, o_ref,
                 state_ref, pr_ref, pk_ref, pv_ref, pg_ref, plo_ref, *, tb):
    L = CHUNK
    i = pl.program_id(1)

    @pl.when(i == 0)
    def _():
        state_ref[...] = jnp.zeros_like(state_ref)
        pr_ref[...] = jnp.zeros_like(pr_ref)
        pk_ref[...] = jnp.zeros_like(pk_ref)
        pv_ref[...] = jnp.zeros_like(pv_ref)
        pg_ref[...] = jnp.zeros_like(pg_ref)
        plo_ref[...] = jnp.zeros_like(plo_ref)

    cp = cp_ref[...]
    mu_r, mu_k, mu_v, mu_g = cp[0:1], cp[1:2], cp[2:3], cp[3:4]
    w0, a0, k_k, k_a, r_k, ln_w, ln_b = (cp[4:5], cp[5:6], cp[6:7], cp[7:8], cp[8:9], cp[9:10], cp[10:11])

    first_row = lax.broadcasted_iota(jnp.int32, (tb, 1), 0) == 0

    def shift_lerp(x_ref, prev_ref, mu):
        x = x_ref[...]
        sh = jnp.where(first_row, prev_ref[SUBLANES - 1:SUBLANES, :], pltpu.roll(x, 1, 0))
        prev_ref[...] = x[tb - SUBLANES:tb]
        return x + (sh - x) * mu

    r = shift_lerp(r_ref, pr_ref, mu_r)
    k = shift_lerp(k_ref, pk_ref, mu_k)
    v = shift_lerp(v_ref, pv_ref, mu_v)
    g = shift_lerp(g_ref, pg_ref, mu_g)
    lo = shift_lerp(lo_ref, plo_ref, mulo_ref[...])

    lane = lax.broadcasted_iota(jnp.int32, (1, LANES), 1)
    first = lane < A_HEAD
    m1 = first.astype(F32)
    m2 = 1.0 - m1

    w = w0 + _dot(jnp.tanh(lo).astype(BF16), wup_ref[...])
    logw = -_sigmoid(w) * EXP_M_HALF
    a = _sigmoid(a0 + _dot(lo.astype(BF16), aup_ref[...]))
    kk = k * k_k
    k = k * (1.0 + (a - 1.0) * k_a)
    kk = kk * lax.rsqrt(_pair_sum(kk * kk, first) + 1e-12)
    alpha = -kk
    beta = kk * a
    bonus = _pair_sum(r * k * r_k, first) * v

    nc = tb // L
    row = lax.broadcasted_iota(jnp.int32, (L, L), 0)
    col = lax.broadcasted_iota(jnp.int32, (L, L), 1)
    tril_strict = row > col
    row2 = lax.broadcasted_iota(jnp.int32, (L, LANES), 0)
    col2 = lax.broadcasted_iota(jnp.int32, (L, LANES), 1) % L
    tril_incl2 = row2 >= col2
    bd_r = lax.broadcasted_iota(jnp.int32, (LANES, LANES), 0) // A_HEAD
    bd_c = lax.broadcasted_iota(jnp.int32, (LANES, LANES), 1) // A_HEAD
    blockdiag = bd_r == bd_c

    pos = lax.broadcasted_iota(jnp.int32, (tb, 1), 0) % L
    cum = logw
    step = 1
    while step < L:
        cum = cum + jnp.where(pos >= step, pltpu.roll(cum, step, 0), 0.0)
        step *= 2
    e_pos = jnp.exp(cum)
    e_neg = jnp.exp(-cum)
    ab = alpha * jnp.exp(cum - logw)
    rb = r * e_pos
    kt = (k * e_neg).astype(BF16)
    bt = (beta * e_neg).astype(BF16)
    vv = v.astype(BF16)
    heads = ((m1, ab * m1, (rb * m1).astype(BF16)), (m2, ab * m2, (rb * m2).astype(BF16)))
    zero_top = jnp.zeros((L, LANES), BF16)

    sls = [slice(c * L, (c + 1) * L) for c in range(nc)]
    kb = [jnp.concatenate([kt[sl], bt[sl]], axis=0) for sl in sls]
    xa = []
    for c, sl in enumerate(sls):
        x = jnp.concatenate([heads[0][1][sl].astype(BF16), heads[1][1][sl].astype(BF16),
                             heads[0][2][sl], heads[1][2][sl]], axis=0)
        xa.append(_dot_nt(x, kb[c]))
    prob = [(c, h) for c in range(nc) for h in range(2)]
    n_pow, ys, a_r = [], [], []
    for c, h in prob:
        an = xa[c][h * L:(h + 1) * L]
        a_ak = jnp.where(tril_strict, an[:, :L], 0.0).astype(BF16)
        n_pow.append(jnp.where(tril_strict, an[:, L:], 0.0).astype(BF16))
        a_r.append(jnp.where(tril_incl2, xa[c][(2 + h) * L:(3 + h) * L], 0.0).astype(BF16))
        ys.append(_dot(a_ak, vv[sls[c]]) * heads[h][0] + pltpu.roll(heads[h][1][sls[c]], A_HEAD, 1))
    for lvl in range(6):
        ys = [y + _dot(n, y.astype(BF16)) for n, y in zip(n_pow, ys)]
        if lvl < 5:
            n_pow = [_dot(n, n).astype(BF16) for n in n_pow]
    q_c, o0_c, g_c, pb_c = [], [], [], []
    for c, sl in enumerate(sls):
        top = jnp.concatenate([vv[sl], zero_top], axis=1)
        y2 = []
        o0 = jnp.zeros((L, LANES), F32)
        qq = rb[sl]
        for h in range(2):
            mh = heads[h][0]
            y = ys[2 * c + h]
            y = jnp.concatenate([y * mh, pltpu.roll(y * (1.0 - mh), A_HEAD, 1)], axis=1)
            y2.append(y)
            out = _dot(a_r[2 * c + h], jnp.concatenate([top, y.astype(BF16)], axis=0))
            o0 = o0 + out[:, :LANES] * mh
            qq = qq + out[:, LANES:]
        ysum = (y2[0] + y2[1]).astype(BF16)
        gp = _dot_tn(jnp.concatenate([top, ysum], axis=0), kb[c])
        g_c.append(jnp.where(blockdiag, gp[:LANES], 0.0))
        pb_c.append(jnp.where(blockdiag, gp[LANES:], 0.0))
        q_c.append(qq.astype(BF16))
        o0_c.append(o0)

    wl = [e_pos[(c + 1) * L - 1:(c + 1) * L] for c in range(nc)]
    eye_f = (lax.broadcasted_iota(jnp.int32, (LANES, LANES), 0)
             == lax.broadcasted_iota(jnp.int32, (LANES, LANES), 1)).astype(F32)
    pairs = []
    for a in range(0, nc, 2):
        b = a + 1
        pb_b = pb_c[b].astype(BF16)
        e_a = (eye_f + pb_c[a]) * wl[a]
        g_a = g_c[a] * wl[a]
        t = _dot(jnp.concatenate([e_a, g_a], axis=0).astype(BF16), pb_b)
        c_ab = (pb_c[a] * wl[a] + t[:LANES]) * wl[b]
        g_ab = (g_a + t[LANES:] + g_c[b]) * wl[b]
        rhs = jnp.concatenate([c_ab.astype(BF16), pb_c[a].astype(BF16)], axis=1)
        pairs.append((rhs, g_ab, wl[a] * wl[b]))

    state = state_ref[...]
    outs = []
    for pi, (rhs, g_ab, w_ab) in enumerate(pairs):
        a, b = 2 * pi, 2 * pi + 1
        st_b = state.astype(BF16)
        r = _dot(st_b, rhs)
        s_mid = (state + r[:, LANES:] + g_c[a]) * wl[a]
        outs.append(o0_c[a] + _dot_nt(q_c[a], st_b))
        outs.append(o0_c[b] + _dot_nt(q_c[b], s_mid.astype(BF16)))
        state = state * w_ab + r[:, :LANES] + g_ab
    state_ref[...] = state
    o = jnp.concatenate(outs, axis=0)

    mean = _pair_sum(o, first) * (1.0 / A_HEAD)
    d = o - mean
    var = _pair_sum(d * d, first) * (1.0 / A_HEAD)
    y = d * lax.rsqrt(var + A_LN_EPS) * ln_w + ln_b
    o_ref[...] = ((y + bonus) * _sigmoid(g)).astype(o_ref.dtype)


def rwkv7_mix(p, mu, w0, w_up, a0, a_up, k_k, k_a, r_k, ln_w, ln_b, tb=1024):
    s = p.shape[0]
    tb = min(tb, s)
    assert s % tb == 0
    npair = A_H // 2
    nblk = A_W // LANES
    zeros = jnp.zeros((A_W,), F32)
    cp = jnp.stack([mu[0:A_W], mu[A_W:2 * A_W], mu[2 * A_W:3 * A_W], mu[3 * A_W:4 * A_W],
                    w0, a0, k_k, k_a, r_k, ln_w, ln_b, zeros, zeros, zeros, zeros, zeros])
    tail = 2 * LANES - 2 * A_LORA
    mulo = jnp.concatenate([mu[4 * A_W:], jnp.zeros((tail,), F32)]).reshape(1, 2 * LANES)
    wup = jnp.concatenate([w_up, jnp.zeros((2 * LANES - A_LORA, A_W), F32)], axis=0).astype(BF16)
    aup = jnp.concatenate([jnp.zeros((A_LORA, A_W), F32), a_up, jnp.zeros((tail, A_W), F32)], axis=0).astype(BF16)
    lora_blk = EV_LORA // (2 * LANES)

    def col(j):
        return pl.BlockSpec((tb, LANES), lambda hp, i, j=j: (i, j * nblk + hp))

    vmem = 2 * (4 * tb * LANES * 4 + tb * 2 * LANES * 4 + tb * LANES * 2) + (24 << 20)
    return pl.pallas_call(
        functools.partial(_rwkv_kernel, tb=tb),
        grid=(npair, s // tb),
        in_specs=[col(0), col(1), col(2), col(3),
                  pl.BlockSpec((tb, 2 * LANES), lambda hp, i: (i, lora_blk)),
                  pl.BlockSpec((16, LANES), lambda hp, i: (0, hp)),
                  pl.BlockSpec((1, 2 * LANES), lambda hp, i: (0, 0)),
                  pl.BlockSpec((2 * LANES, LANES), lambda hp, i: (0, hp)),
                  pl.BlockSpec((2 * LANES, LANES), lambda hp, i: (0, hp))],
        out_specs=pl.BlockSpec((tb, LANES), lambda hp, i: (i, hp)),
        out_shape=jax.ShapeDtypeStruct((s, MIX_W), BF16),
        scratch_shapes=[pltpu.VMEM((LANES, LANES), F32),
                        pltpu.VMEM((SUBLANES, LANES), F32), pltpu.VMEM((SUBLANES, LANES), F32),
                        pltpu.VMEM((SUBLANES, LANES), F32), pltpu.VMEM((SUBLANES, LANES), F32),
                        pltpu.VMEM((SUBLANES, 2 * LANES), F32)],
        compiler_params=_cparams(("parallel", "arbitrary"), vmem),
        name="rwkv7",
    )(p, p, p, p, p, cp, mulo, wup, aup)


def _mlstm_kernel(ib_ref, fb_ref, q_ref, k_ref, v_ref, og_ref, gt_ref, ng_ref, ybuf_ref, o_ref,
                  c_ref, n_ref, m_ref, *, tb):
    del ybuf_ref
    L = CHUNK
    h = pl.program_id(0)
    i = pl.program_id(1)

    @pl.when(i == 0)
    def _():
        c_ref[...] = jnp.zeros_like(c_ref)
        n_ref[...] = jnp.zeros_like(n_ref)
        m_ref[...] = jnp.zeros_like(m_ref)

    i_b = ib_ref[h]
    f_b = fb_ref[h]
    row = lax.broadcasted_iota(jnp.int32, (L, L), 0)
    col = lax.broadcasted_iota(jnp.int32, (L, L), 1)
    causal = row >= col
    eye = row == col

    def to_col(x_row):
        return jnp.sum(jnp.where(eye, jnp.broadcast_to(x_row, (L, L)), 0.0), axis=1, keepdims=True)

    pre = []
    for c in range(tb // L):
        sl = slice(c * L, (c + 1) * L)
        gi = gt_ref[c, pl.ds(h, 1), :]
        gf = gt_ref[c, pl.ds(B_H + h, 1), :]
        li_row = GATE_CAP * jnp.tanh((gi + i_b) / GATE_CAP)
        lf_row = -_softplus(-(GATE_CAP * jnp.tanh((gf + f_b) / GATE_CAP)))
        li_col = to_col(li_row)
        lf_col = to_col(lf_row)
        cum_col = jnp.sum(jnp.where(causal, jnp.broadcast_to(lf_row, (L, L)), 0.0), axis=1, keepdims=True)
        cum_row = jnp.sum(jnp.where(row <= col, jnp.broadcast_to(lf_col, (L, L)), 0.0), axis=0, keepdims=True)
        total = jnp.sum(lf_row, axis=1, keepdims=True)

        qf = q_ref[sl, :]
        q = qf.astype(BF16)
        kf = k_ref[sl, :] * (B_HEAD ** -0.5)
        vb = v_ref[sl, :].astype(BF16)

        d_log = jnp.where(causal, cum_col - cum_row + li_row, -jnp.inf)
        m_loc = jnp.max(d_log, axis=1, keepdims=True)
        s_loc = _dot_nt(q, kf.astype(BF16)) * jnp.exp(d_log - m_loc)
        sv = _dot(s_loc.astype(BF16), vb)
        s_sum = jnp.sum(s_loc, axis=1, keepdims=True)

        log_w_col = total - cum_col + li_col
        m_k = jnp.max(log_w_col, axis=0, keepdims=True)
        kw = kf * jnp.exp(log_w_col - m_k)
        kv = _dot_tn(kw.astype(BF16), vb)
        k_sum = jnp.sum(kw, axis=0, keepdims=True)
        wide = lambda t: jnp.broadcast_to(t, (t.shape[0], B_HEAD))
        pre.append((qf, q, wide(cum_col), wide(total), wide(m_loc), sv, wide(s_sum), wide(m_k), kv, k_sum))

    c_st = c_ref[...]
    n_st = n_ref[...]
    m_st = m_ref[...]
    outs = []
    for qf, q, cum_b, total_b, m_loc_b, sv, s_sum_b, m_k_b, kv, k_sum in pre:
        m_inter = cum_b + m_st
        m_row = jnp.maximum(m_loc_b, m_inter)
        w_loc = jnp.exp(m_loc_b - m_row)
        w_inter = jnp.exp(m_inter - m_row)
        num = w_loc * sv + w_inter * _dot(q, c_st.astype(BF16))
        den = w_loc * s_sum_b + w_inter * jnp.sum(qf * n_st, axis=1, keepdims=True)
        outs.append(num / jnp.maximum(jnp.abs(den), jnp.exp(-m_row)))

        m_new = jnp.maximum(total_b + m_st, m_k_b)
        carry = jnp.exp(total_b + m_st - m_new)
        w_new = jnp.exp(m_k_b - m_new)
        c_st = carry * c_st + w_new * kv
        n_st = carry * n_st + w_new * k_sum
        m_st = m_new
    c_ref[...] = c_st
    n_ref[...] = n_st
    m_ref[...] = m_st

    hh = jnp.concatenate(outs, axis=0) if len(outs) > 1 else outs[0]
    ms = jnp.mean(hh * hh, axis=1, keepdims=True)
    y = hh * lax.rsqrt(ms + NORM_EPS) * ng_ref[...]
    o_ref[...] = (y * _sigmoid(og_ref[...])).astype(o_ref.dtype)


def mlstm_mix(p, gates_t, i_b, f_b, norm_g, y_buf, tb=1024):
    s = p.shape[0]
    tb = min(tb, s)
    assert s % tb == 0
    nblk = B_W // B_HEAD
    out0 = A_W // B_HEAD
    base = EV_A_COLS // B_HEAD

    def col(j):
        return pl.BlockSpec((tb, B_HEAD), lambda h, i, j=j: (i, base + j * nblk + h))

    smem = pl.BlockSpec(memory_space=pltpu.SMEM)
    vmem = 2 * (4 * tb * B_HEAD * 4 + tb * B_HEAD * 2) + (24 << 20)
    return pl.pallas_call(
        functools.partial(_mlstm_kernel, tb=tb),
        grid=(B_H, s // tb),
        in_specs=[smem, smem, col(0), col(1), col(2), col(3),
                  pl.BlockSpec((tb // CHUNK, 2 * B_H, CHUNK), lambda h, i: (i, 0, 0)),
                  pl.BlockSpec((1, B_HEAD), lambda h, i: (0, h)),
                  pl.BlockSpec(memory_space=pl.ANY)],
        out_specs=pl.BlockSpec((tb, B_HEAD), lambda h, i: (i, out0 + h)),
        out_shape=jax.ShapeDtypeStruct(y_buf.shape, BF16),
        input_output_aliases={8: 0},
        scratch_shapes=[pltpu.VMEM((B_HEAD, B_HEAD), F32), pltpu.VMEM((1, B_HEAD), F32),
                        pltpu.VMEM((1, B_HEAD), F32)],
        compiler_params=_cparams(("parallel", "arbitrary"), vmem),
        name="mlstm",
    )(i_b, f_b, p, p, p, p, gates_t, norm_g.reshape(1, B_W), y_buf)


def _retention_kernel(lg_ref, q_ref, k_ref, v_ref, g_ref, cc_ref, ss_ref, ng_ref, o_ref, r_ref, *, tb):
    L = CHUNK
    h = pl.program_id(0)
    i = pl.program_id(1)

    @pl.when(i == 0)
    def _():
        r_ref[...] = jnp.zeros_like(r_ref)

    lg = lg_ref[h]
    row = lax.broadcasted_iota(jnp.int32, (L, L), 0)
    col = lax.broadcasted_iota(jnp.int32, (L, L), 1)
    intra = jnp.exp(lg * jnp.abs(row - col).astype(F32))
    pos = lax.broadcasted_iota(jnp.int32, (L, 1), 0).astype(F32)
    q_decay = jnp.exp(lg * (pos + 1.0))
    k_decay = jnp.exp(lg * (L - 1.0 - pos))
    chunk_decay = jnp.exp(jnp.full((1, C_VHEAD), lg * L, F32))

    cc = cc_ref[...]
    ss = ss_ref[...]

    def rope(t):
        return t * cc + pltpu.roll(t, C_KHEAD // 2, 1) * ss

    q = rope(q_ref[...]).astype(BF16)
    k = rope(k_ref[...]) * (C_KHEAD ** -0.5)
    vb = v_ref[...].astype(BF16)
    sls = [slice(c * L, (c + 1) * L) for c in range(tb // L)]
    sv = [_dot((_dot_nt(q[sl], k[sl].astype(BF16)) * intra).astype(BF16), vb[sl]) for sl in sls]
    kv = [_dot_tn((k[sl] * k_decay).astype(BF16), vb[sl]) for sl in sls]
    r_st = r_ref[...]
    outs = []
    for c, sl in enumerate(sls):
        outs.append(sv[c] + q_decay * _dot(q[sl], r_st.astype(BF16)))
        r_st = chunk_decay * r_st + kv[c]
    r_ref[...] = r_st
    o = jnp.concatenate(outs, axis=0) if len(outs) > 1 else outs[0]
    ms = jnp.mean(o * o, axis=1, keepdims=True)
    y = o * lax.rsqrt(ms + NORM_EPS) * ng_ref[...]
    g = g_ref[...]
    o_ref[...] = (y * (g * _sigmoid(g))).astype(o_ref.dtype)


def retention_mix(p, cc, ss, log_gamma, norm_g, tb=1024):
    s = p.shape[0]
    tb = min(tb, s)
    assert s % tb == 0
    kb = C_KW // C_KHEAD
    vb = (2 * C_KW) // C_VHEAD
    gb = (2 * C_KW + C_VW) // C_VHEAD
    smem = pl.BlockSpec(memory_space=pltpu.SMEM)
    vmem = 2 * (2 * tb * C_KHEAD * 4 + 2 * tb * C_VHEAD * 4 + 2 * tb * C_KHEAD * 4 + tb * C_VHEAD * 2) + (24 << 20)
    return pl.pallas_call(
        functools.partial(_retention_kernel, tb=tb),
        grid=(C_H, s // tb),
        in_specs=[smem,
                  pl.BlockSpec((tb, C_KHEAD), lambda h, i: (i, h)),
                  pl.BlockSpec((tb, C_KHEAD), lambda h, i: (i, kb + h)),
                  pl.BlockSpec((tb, C_VHEAD), lambda h, i: (i, vb + h)),
                  pl.BlockSpec((tb, C_VHEAD), lambda h, i: (i, gb + h)),
                  pl.BlockSpec((tb, C_KHEAD), lambda h, i: (i, 0)),
                  pl.BlockSpec((tb, C_KHEAD), lambda h, i: (i, 0)),
                  pl.BlockSpec((1, C_VHEAD), lambda h, i: (0, h))],
        out_specs=pl.BlockSpec((tb, C_VHEAD), lambda h, i: (i, h)),
        out_shape=jax.ShapeDtypeStruct((s, MIX_W), BF16),
        scratch_shapes=[pltpu.VMEM((C_KHEAD, C_VHEAD), F32)],
        compiler_params=_cparams(("parallel", "arbitrary"), vmem),
        name="retention",
    )(log_gamma, p, p, p, p, cc, ss, norm_g.reshape(1, C_VW))


def _rglru_kernel(x_ref, gate_ref, cw_ref, cp_ref, wr_ref, wi_ref, ybuf_ref, o_ref, px_ref, h_ref, *, tb, ts):
    del ybuf_ref
    i = pl.program_id(1)

    @pl.when(i == 0)
    def _():
        px_ref[...] = jnp.zeros_like(px_ref)
        h_ref[...] = jnp.zeros_like(h_ref)

    cw = cw_ref[...]
    cp = cp_ref[...]
    conv_b, b_r, b_i, lam = cp[0:1], cp[1:2], cp[2:3], cp[3:4]
    x = x_ref[...]
    ext = jnp.concatenate([px_ref[...], x], axis=0)
    xc = conv_b + x * cw[CONV_W - 1:CONV_W]
    for d in range(1, CONV_W):
        xc = xc + pltpu.roll(ext, d, 0)[SUBLANES:] * cw[CONV_W - 1 - d:CONV_W - d]
    px_ref[...] = x[tb - SUBLANES:tb]

    xcb = xc.astype(BF16)
    r = _sigmoid(_dot(xcb, wr_ref[0]) + b_r)
    ig = _sigmoid(_dot(xcb, wi_ref[0]) + b_i)
    log_a = -LRU_C * r * _softplus(-lam)
    a = jnp.exp(log_a)
    u = jnp.sqrt(-jnp.tanh(log_a) * (a * a + 1.0)) * (ig * xc)

    rows = lax.broadcasted_iota(jnp.int32, (tb, 1), 0) % ts
    d = 1
    while d < ts:
        keep = rows >= d
        a_sh = jnp.where(keep, pltpu.roll(a, d, 0), 1.0)
        u_sh = jnp.where(keep, pltpu.roll(u, d, 0), 0.0)
        u = a * u_sh + u
        a = a * a_sh
        d *= 2
    h_prev = h_ref[0:1, :]
    parts = []
    for c in range(tb // ts):
        h_c = a[c * ts:(c + 1) * ts] * h_prev + u[c * ts:(c + 1) * ts]
        h_prev = h_c[ts - 1:ts]
        parts.append(h_c)
    hh = jnp.concatenate(parts, axis=0) if len(parts) > 1 else parts[0]
    h_ref[...] = jnp.broadcast_to(h_prev, h_ref.shape)

    gt = gate_ref[...]
    gelu = 0.5 * gt * (1.0 + jnp.tanh(0.7978845608028654 * (gt + 0.044715 * (gt * gt * gt))))
    o_ref[...] = (hh * gelu).astype(o_ref.dtype)


def rglru_mix(p, conv_w, conv_b, w_r, b_r, w_i, b_i, lam, y_buf, tb=1024, ts=256):
    s = p.shape[0]
    tb = min(tb, s)
    ts = min(ts, tb)
    assert s % tb == 0 and tb % ts == 0
    xb0 = (2 * C_KW + 2 * C_VW) // D_BW
    gt0 = xb0 + D_W // D_BW
    out0 = C_VW // D_BW
    zeros = jnp.zeros((D_W,), F32)
    cp = jnp.stack([conv_b, b_r, b_i, lam, zeros, zeros, zeros, zeros])
    vmem = 2 * (2 * tb * D_BW * 4 + tb * D_BW * 2 + 2 * D_BW * D_BW * 2) + (24 << 20)
    return pl.pallas_call(
        functools.partial(_rglru_kernel, tb=tb, ts=ts),
        grid=(D_BLOCKS, s // tb),
        in_specs=[pl.BlockSpec((tb, D_BW), lambda n, i: (i, xb0 + n)),
                  pl.BlockSpec((tb, D_BW), lambda n, i: (i, gt0 + n)),
                  pl.BlockSpec((CONV_W, D_BW), lambda n, i: (0, n)),
                  pl.BlockSpec((SUBLANES, D_BW), lambda n, i: (0, n)),
                  pl.BlockSpec((1, D_BW, D_BW), lambda n, i: (n, 0, 0)),
                  pl.BlockSpec((1, D_BW, D_BW), lambda n, i: (n, 0, 0)),
                  pl.BlockSpec(memory_space=pl.ANY)],
        out_specs=pl.BlockSpec((tb, D_BW), lambda n, i: (i, out0 + n)),
        out_shape=jax.ShapeDtypeStruct(y_buf.shape, BF16),
        input_output_aliases={6: 0},
        scratch_shapes=[pltpu.VMEM((SUBLANES, D_BW), F32), pltpu.VMEM((SUBLANES, D_BW), F32)],
        compiler_params=_cparams(("parallel", "arbitrary"), vmem),
        name="rglru",
    )(p, p, conv_w, cp, w_r.astype(BF16), w_i.astype(BF16), y_buf)


def _xattn_block_kernel(h_ref, g_ref, g2_ref, wq_ref, k_ref, v_ref, wo_ref, o_ref, xn2_ref, att_ref):
    x = h_ref[...]
    ms = jnp.mean(x * x, axis=-1, keepdims=True)
    xn = (x * lax.rsqrt(ms + NORM_EPS) * g_ref[...]).astype(BF16)
    q = _dot(xn, wq_ref[...]).astype(BF16)
    for hd in range(XA_H):
        sl = slice(hd * XA_HEAD, (hd + 1) * XA_HEAD)
        sc = _dot_nt(q[:, sl], k_ref[:, sl]) * (XA_HEAD ** -0.5)
        sc = sc - jnp.max(sc, axis=1, keepdims=True)
        e = jnp.exp(sc)
        pr = e / jnp.sum(e, axis=1, keepdims=True)
        att_ref[:, sl] = _dot(pr.astype(BF16), v_ref[:, sl]).astype(BF16)
    h_new = x + _dot(att_ref[...], wo_ref[...])
    o_ref[...] = h_new
    ms2 = jnp.mean(h_new * h_new, axis=-1, keepdims=True)
    xn2_ref[...] = (h_new * lax.rsqrt(ms2 + NORM_EPS) * g2_ref[...]).astype(xn2_ref.dtype)


def xattn_block(h, g, g_next, wq, kmem, vmem_, wo, tm=256):
    s, d = h.shape
    nm = kmem.shape[0]
    tm = min(tm, s)
    const = dict(pipeline_mode=pl.Buffered(1))
    vmem = (4 * tm * d * 4 + 2 * tm * d * 2 + 2 * d * XA_W * 2 + 2 * nm * XA_W * 2
            + tm * d * (2 + 4 + 4) + tm * XA_W * (4 + 2 + 2) + (8 << 20))
    row = pl.BlockSpec((tm, d), lambda i: (i, 0))
    gain = pl.BlockSpec((1, d), lambda i: (0, 0))
    return pl.pallas_call(
        _xattn_block_kernel,
        grid=(s // tm,),
        in_specs=[row, gain, gain,
                  pl.BlockSpec((d, XA_W), lambda i: (0, 0), **const),
                  pl.BlockSpec((nm, XA_W), lambda i: (0, 0), **const),
                  pl.BlockSpec((nm, XA_W), lambda i: (0, 0), **const),
                  pl.BlockSpec((XA_W, d), lambda i: (0, 0), **const)],
        out_specs=[row, row],
        out_shape=[jax.ShapeDtypeStruct((s, d), F32), jax.ShapeDtypeStruct((s, d), BF16)],
        scratch_shapes=[pltpu.VMEM((tm, XA_W), BF16)],
        compiler_params=_cparams(("parallel",), vmem),
        name="xattn_block",
    )(h, g.reshape(1, d), g_next.reshape(1, d), wq, kmem, vmem_, wo)


def kernel(x, mem, mem_norm_g, norm_mix_g, norm_xattn_g, norm_mlp_g, xattn_wq, xattn_wk, xattn_wv, xattn_wo, mlp_w1, mlp_w2, even_w_in, even_w_out, rwkv_mu, rwkv_w0, rwkv_w_up, rwkv_a0, rwkv_a_up, rwkv_k_k, rwkv_k_a, rwkv_r_k, rwkv_ln_w, rwkv_ln_b, mlstm_i_b, mlstm_f_b, mlstm_norm_g, odd_w_in, odd_w_out, ret_norm_g, lru_conv_w, lru_conv_b, lru_w_r, lru_b_r, lru_w_i, lru_b_i, lru_lambda, final_norm_g):
    b, s, d = x.shape
    assert b == 1 and d == D_MODEL and s % 256 == 0
    h = x.reshape(s, d)
    mem_n = rmsnorm(mem.reshape(-1, d), mem_norm_g, BF16)
    wq_bf = xattn_wq.astype(BF16)
    wo_bf = xattn_wo.astype(BF16)
    even_wt = jnp.swapaxes(even_w_in, 1, 2)

    positions = jnp.arange(s, dtype=F32)
    inv_freq = ROPE_BASE ** (-jnp.arange(0, C_KHEAD, 2, dtype=F32) / C_KHEAD)
    ang = positions[:, None] * inv_freq[None, :]
    cos, sin = jnp.cos(ang), jnp.sin(ang)
    rope_cc = jnp.concatenate([cos, cos], axis=1)
    rope_ss = jnp.concatenate([-sin, sin], axis=1)
    log_gamma = jnp.log1p(-jnp.exp2(-5.0 - jnp.arange(C_H, dtype=F32)))

    for layer in range(DEPTH):
        j = layer // 2
        xn = rmsnorm(h, norm_mix_g[layer], BF16)
        if layer % 2 == 0:
            p = matmul(xn, even_wt, j, F32, n=EV_COLS, w_transposed=True, tn=EV_TN,
                       segments=((0, 0), (EV_A_COLS // EV_TN, A_COLS), (EV_GATE_TILE, EVEN_COLS - EV_TN)))
            y = rwkv7_mix(p, rwkv_mu[j], rwkv_w0[j], rwkv_w_up[j], rwkv_a0[j], rwkv_a_up[j],
                          rwkv_k_k[j], rwkv_k_a[j], rwkv_r_k[j], rwkv_ln_w[j], rwkv_ln_b[j])
            gates_t = p[:, EV_COLS - 2 * B_H:].reshape(s // CHUNK, CHUNK, 2 * B_H).transpose(0, 2, 1)
            y = mlstm_mix(p, gates_t, mlstm_i_b[j], mlstm_f_b[j], mlstm_norm_g[j], y)
            w_out = even_w_out
        else:
            p = matmul(xn, odd_w_in, j, F32)
            y = retention_mix(p, rope_cc, rope_ss, log_gamma, ret_norm_g[j])
            y = rglru_mix(p, lru_conv_w[j], lru_conv_b[j], lru_w_r[j], lru_b_r[j],
                          lru_w_i[j], lru_b_i[j], lru_lambda[j], y)
            w_out = odd_w_out
        h = matmul(y, w_out, j, F32, epilogue="residual", residual=h)

        kmem = matmul(mem_n, xattn_wk, layer, BF16)
        vmem_ = matmul(mem_n, xattn_wv, layer, BF16)
        h, xn = xattn_block(h, norm_xattn_g[layer], norm_mlp_g[layer], wq_bf[layer], kmem, vmem_, wo_bf[layer])
        u = matmul(xn, mlp_w1, layer, BF16, epilogue="relu2")
        h = matmul(u, mlp_w2, layer, F32, epilogue="residual", residual=h)

    return rmsnorm(h, final_norm_g, F32).reshape(b, s, d)
```

```python
import functools

import jax
import jax.numpy as jnp
from jax import lax
from jax.experimental import pallas as pl
from jax.experimental.pallas import tpu as pltpu

D_MODEL = 4096
DEPTH = 4
CHUNK = 64
NORM_EPS = 1e-6
MIX_W = D_MODEL
A_W = MIX_W // 2
A_HEAD = 64
A_H = A_W // A_HEAD
A_LORA = 96
A_LN_EPS = 64e-5
B_W = MIX_W // 2
B_H = 8
B_HEAD = B_W // B_H
GATE_CAP = 15.0
C_VW = MIX_W // 2
C_H = 8
C_VHEAD = C_VW // C_H
C_KHEAD = C_VHEAD // 2
C_KW = C_H * C_KHEAD
ROPE_BASE = 10000.0
D_W = MIX_W // 2
D_BLOCKS = 16
D_BW = D_W // D_BLOCKS
CONV_W = 4
LRU_C = 8.0
D_FF = 4 * D_MODEL
XA_H = 4
XA_HEAD = 256
XA_W = XA_H * XA_HEAD
A_COLS = 4 * A_W + 2 * A_LORA
B_COLS = 4 * B_W + 2 * B_H

LANES = 128
SUBLANES = 8
VMEM_BYTES_V7X = 64 * 1024 * 1024

EVEN_COLS = A_COLS + B_COLS
EV_TN = 512
EV_LORA = 4 * A_W
EV_A_COLS = EV_LORA + EV_TN
EV_GATE_TILE = (EV_A_COLS + 4 * B_W) // EV_TN
EV_COLS = (EV_GATE_TILE + 1) * EV_TN
assert A_COLS <= EV_LORA + 2 * LANES and EV_A_COLS % B_HEAD == 0

BF16 = jnp.bfloat16
F32 = jnp.float32
EXP_M_HALF = 0.6065306597126334


def _cparams(dims, vmem_bytes):
    limit = min(int(vmem_bytes), VMEM_BYTES_V7X - 4 * 1024 * 1024)
    return pltpu.CompilerParams(dimension_semantics=dims, vmem_limit_bytes=limit)


def _dot(a, b):
    return jnp.dot(a, b, preferred_element_type=F32)


def _dot_nt(a, b):
    return lax.dot_general(a, b, (((1,), (1,)), ((), ())), preferred_element_type=F32)


def _dot_tn(a, b):
    return lax.dot_general(a, b, (((0,), (0,)), ((), ())), preferred_element_type=F32)


def _sigmoid(x):
    return 0.5 * jnp.tanh(0.5 * x) + 0.5


def _softplus(x):
    return jnp.maximum(x, 0.0) + jnp.log1p(jnp.exp(-jnp.abs(x)))


def _rmsnorm_kernel(x_ref, g_ref, o_ref):
    x = x_ref[...]
    ms = jnp.mean(x * x, axis=-1, keepdims=True)
    o_ref[...] = (x * lax.rsqrt(ms + NORM_EPS) * g_ref[...]).astype(o_ref.dtype)


def rmsnorm(x, g, out_dtype, tm=512):
    m, d = x.shape
    tm = min(tm, m)
    vmem = 2 * tm * d * (4 + jnp.dtype(out_dtype).itemsize) + tm * d * 4 + 4 * d * 4 + (4 << 20)
    return pl.pallas_call(
        _rmsnorm_kernel,
        grid=(m // tm,),
        in_specs=[pl.BlockSpec((tm, d), lambda i: (i, 0)), pl.BlockSpec((1, d), lambda i: (0, 0))],
        out_specs=pl.BlockSpec((tm, d), lambda i: (i, 0)),
        out_shape=jax.ShapeDtypeStruct((m, d), out_dtype),
        compiler_params=_cparams(("parallel",), vmem),
        name="rmsnorm",
    )(x, g.reshape(1, d))


def _mm_kernel(*refs, nk, epilogue, w_transposed):
    if epilogue == "residual":
        x_ref, w_ref, r_ref, o_ref = refs
    else:
        x_ref, w_ref, o_ref = refs
        r_ref = None

    if nk > 1:
        @pl.when(pl.program_id(2) == 0)
        def _():
            o_ref[...] = r_ref[...]

    if w_transposed:
        part = _dot_nt(x_ref[...], w_ref[0].astype(BF16))
    else:
        part = _dot(x_ref[...], w_ref[...].astype(BF16))
    if nk > 1:
        o_ref[...] += part
    elif epilogue == "residual":
        o_ref[...] = r_ref[...] + part
    elif epilogue == "relu2":
        o_ref[...] = jnp.square(jnp.maximum(part, 0.0)).astype(o_ref.dtype)
    else:
        o_ref[...] = part.astype(o_ref.dtype)


def matmul(x, w, layer, out_dtype, n=None, w_transposed=False, segments=((0, 0),), epilogue="none", residual=None,
           tm=2048, tn=None, tk=None):
    m, kdim = x.shape
    n_all = w.shape[1] if w_transposed else w.shape[2]
    n = n_all if n is None else n
    tm = min(tm, m)
    if tk is None:
        tk = kdim if kdim <= 4096 else 2048
    nk = kdim // tk
    single_x = nk == 1 and m > tm
    if tn is None:
        tn = 256 if (nk == 1 and not single_x) else 512
    tn = min(tn, n)
    assert m % tm == 0 and n % tn == 0 and kdim % tk == 0
    if w_transposed:
        ends = [jb for jb, _ in segments[1:]] + [n // tn]
        assert all(r0 % SUBLANES == 0 and r0 + (je - jb) * tn <= n_all for (jb, r0), je in zip(segments, ends))

        def w_row(j):
            row = segments[0][1] + j * tn
            for jb, r0 in segments[1:]:
                row = jnp.where(j >= jb, r0 + (j - jb) * tn, row)
            return pl.multiple_of(row, SUBLANES)

        w_spec = pl.BlockSpec((pl.Element(1), pl.Element(tn), pl.Element(tk)),
                              lambda i, j, k: (layer, w_row(j), pl.multiple_of(k * tk, LANES)))
    else:
        assert tuple(segments) == ((0, 0),)
        w_spec = pl.BlockSpec((pl.Squeezed(), tk, tn), lambda i, j, k: (layer, k, j))
    if single_x:
        x_spec = pl.BlockSpec((tm, tk), lambda i, j, k: (i, k), pipeline_mode=pl.Buffered(1))
    else:
        x_spec = pl.BlockSpec((tm, tk), lambda i, j, k: (i, k))
    in_specs = [x_spec, w_spec]
    args = [x, w]
    osz = jnp.dtype(out_dtype).itemsize
    wsz = jnp.dtype(w.dtype).itemsize
    vmem = ((1 if single_x else 2) * tm * tk * 2 + 2 * (tk * tn * wsz + tm * tn * osz)
            + tk * tn * 2 + 2 * tm * tn * 4 + (8 << 20))
    if epilogue == "residual":
        in_specs.append(pl.BlockSpec((tm, tn), lambda i, j, k: (i, j)))
        args.append(residual)
        vmem += 2 * tm * tn * 4
    assert nk == 1 or (epilogue == "residual" and out_dtype == F32)
    return pl.pallas_call(
        functools.partial(_mm_kernel, nk=nk, epilogue=epilogue, w_transposed=w_transposed),
        grid=(m // tm, n // tn, nk),
        in_specs=in_specs,
        out_specs=pl.BlockSpec((tm, tn), lambda i, j, k: (i, j)),
        out_shape=jax.ShapeDtypeStruct((m, n), out_dtype),
        compiler_params=_cparams(("parallel", "arbitrary", "arbitrary"), vmem),
        name="matmul_" + epilogue,
    )(*args)


def _shift_rows(x, prev_ref, d):
    ext = jnp.concatenate([prev_ref[...], x], axis=0)
    return pltpu.roll(ext, d, 0)[SUBLANES:]


def _pair_sum(x, first):
    s1 = jnp.sum(jnp.where(first, x, 0.0), axis=1, keepdims=True)
    s2 = jnp.sum(jnp.where(first, 0.0, x), axis=1, keepdims=True)
    return jnp.where(first, s1, s2)


def _rwkv_kernel(r_ref, k_ref, v_ref, g_ref, lo_ref, cp_ref, mulo_ref, wup_ref, aup_ref, o_ref,
                 state_ref, pr_ref, pk_ref, pv_ref, pg_ref, plo_ref, *, tb):
    L = CHUNK
    i = pl.program_id(1)

    @pl.when(i == 0)
    def _():
        state_ref[...] = jnp.zeros_like(state_ref)
        pr_ref[...] = jnp.zeros_like(pr_ref)
        pk_ref[...] = jnp.zeros_like(pk_ref)
        pv_ref[...] = jnp.zeros_like(pv_ref)
        pg_ref[...] = jnp.zeros_like(pg_ref)
        plo_ref[...] = jnp.zeros_like(plo_ref)

    cp = cp_ref[...]
    mu_r, mu_k, mu_v, mu_g = cp[0:1], cp[1:2], cp[2:3], cp[3:4]
    w0, a0, k_k, k_a, r_k, ln_w, ln_b = (cp[4:5], cp[5:6], cp[6:7], cp[7:8], cp[8:9], cp[9:10], cp[10:11])

    first_row = lax.broadcasted_iota(jnp.int32, (tb, 1), 0) == 0

    def shift_lerp(x_ref, prev_ref, mu):
        x = x_ref[...]
        sh = jnp.where(first_row, prev_ref[SUBLANES - 1:SUBLANES, :], pltpu.roll(x, 1, 0))
        prev_ref[...] = x[tb - SUBLANES:tb]
        return x + (sh - x) * mu

    r = shift_lerp(r_ref, pr_ref, mu_r)
    k = shift_lerp(k_ref, pk_ref, mu_k)
    v = shift_lerp(v_ref, pv_ref, mu_v)
    g = shift_lerp(g_ref, pg_ref, mu_g)
    lo = shift_lerp(lo_ref, plo_ref, mulo_ref[...])

    lane = lax.broadcasted_iota(jnp.int32, (1, LANES), 1)
    first = lane < A_HEAD
    m1 = first.astype(F32)
    m2 = 1.0 - m1

    w = w0 + _dot(jnp.tanh(lo).astype(BF16), wup_ref[...])
    logw = -_sigmoid(w) * EXP_M_HALF
    a = _sigmoid(a0 + _dot(lo.astype(BF16), aup_ref[...]))
    kk = k * k_k
    k = k * (1.0 + (a - 1.0) * k_a)
    kk = kk * lax.rsqrt(_pair_sum(kk * kk, first) + 1e-12)
    alpha = -kk
    beta = kk * a
    bonus = _pair_sum(r * k * r_k, first) * v

    nc = tb // L
    row = lax.broadcasted_iota(jnp.int32, (L, L), 0)
    col = lax.broadcasted_iota(jnp.int32, (L, L), 1)
    tril_strict = row > col
    row2 = lax.broadcasted_iota(jnp.int32, (L, LANES), 0)
    col2 = lax.broadcasted_iota(jnp.int32, (L, LANES), 1) % L
    tril_incl2 = row2 >= col2
    bd_r = lax.broadcasted_iota(jnp.int32, (LANES, LANES), 0) // A_HEAD
    bd_c = lax.broadcasted_iota(jnp.int32, (LANES, LANES), 1) // A_HEAD
    blockdiag = bd_r == bd_c

    pos = lax.broadcasted_iota(jnp.int32, (tb, 1), 0) % L
    cum = logw
    step = 1
    while step < L:
        cum = cum + jnp.where(pos >= step, pltpu.roll(cum, step, 0), 0.0)
        step *= 2
    e_pos = jnp.exp(cum)
    e_neg = jnp.exp(-cum)
    ab = alpha * jnp.exp(cum - logw)
    rb = r * e_pos
    kt = (k * e_neg).astype(BF16)
    bt = (beta * e_neg).astype(BF16)
    vv = v.astype(BF16)
    heads = ((m1, ab * m1, (rb * m1).astype(BF16)), (m2, ab * m2, (rb * m2).astype(BF16)))
    zero_top = jnp.zeros((L, LANES), BF16)

    sls = [slice(c * L, (c + 1) * L) for c in range(nc)]
    kb = [jnp.concatenate([kt[sl], bt[sl]], axis=0) for sl in sls]
    xa = []
    for c, sl in enumerate(sls):
        x = jnp.concatenate([heads[0][1][sl].astype(BF16), heads[1][1][sl].astype(BF16),
                             heads[0][2][sl], heads[1][2][sl]], axis=0)
        xa.append(_dot_nt(x, kb[c]))
    prob = [(c, h) for c in range(nc) for h in range(2)]
    n_pow, ys, a_r = [], [], []
    for c, h in prob:
        an = xa[c][h * L:(h + 1) * L]
        a_ak = jnp.where(tril_strict, an[:, :L], 0.0).astype(BF16)
        n_pow.append(jnp.where(tril_strict, an[:, L:], 0.0).astype(BF16))
        a_r.append(jnp.where(tril_incl2, xa[c][(2 + h) * L:(3 + h) * L], 0.0).astype(BF16))
        ys.append(_dot(a_ak, vv[sls[c]]) * heads[h][0] + pltpu.roll(heads[h][1][sls[c]], A_HEAD, 1))
    for lvl in range(6):
        ys = [y + _dot(n, y.astype(BF16)) for n, y in zip(n_pow, ys)]
        if lvl < 5:
            n_pow = [_dot(n, n).astype(BF16) for n in n_pow]
    q_c, o0_c, g_c, pb_c = [], [], [], []
    for c, sl in enumerate(sls):
        top = jnp.concatenate([vv[sl], zero_top], axis=1)
        y2 = []
        o0 = jnp.zeros((L, LANES), F32)
        qq = rb[sl]
        for h in range(2):
            mh = heads[h][0]
            y = ys[2 * c + h]
            y = jnp.concatenate([y * mh, pltpu.roll(y * (1.0 - mh), A_HEAD, 1)], axis=1)
            y2.append(y)
            out = _dot(a_r[2 * c + h], jnp.concatenate([top, y.astype(BF16)], axis=0))
            o0 = o0 + out[:, :LANES] * mh
            qq = qq + out[:, LANES:]
        ysum = (y2[0] + y2[1]).astype(BF16)
        gp = _dot_tn(jnp.concatenate([top, ysum], axis=0), kb[c])
        g_c.append(jnp.where(blockdiag, gp[:LANES], 0.0))
        pb_c.append(jnp.where(blockdiag, gp[LANES:], 0.0))
        q_c.append(qq.astype(BF16))
        o0_c.append(o0)

    wl = [e_pos[(c + 1) * L - 1:(c + 1) * L] for c in range(nc)]
    eye_f = (lax.broadcasted_iota(jnp.int32, (LANES, LANES), 0)
             == lax.broadcasted_iota(jnp.int32, (LANES, LANES), 1)).astype(F32)
    pairs = []
    for a in range(0, nc, 2):
        b = a + 1
        pb_b = pb_c[b].astype(BF16)
        e_a = (eye_f + pb_c[a]) * wl[a]
        g_a = g_c[a] * wl[a]
        t = _dot(jnp.concatenate([e_a, g_a], axis=0).astype(BF16), pb_b)
        c_ab = (pb_c[a] * wl[a] + t[:LANES]) * wl[b]
        g_ab = (g_a + t[LANES:] + g_c[b]) * wl[b]
        rhs = jnp.concatenate([c_ab.astype(BF16), pb_c[a].astype(BF16)], axis=1)
        pairs.append((rhs, g_ab, wl[a] * wl[b]))

    state = state_ref[...]
    outs = []
    for pi, (rhs, g_ab, w_ab) in enumerate(pairs):
        a, b = 2 * pi, 2 * pi + 1
        st_b = state.astype(BF16)
        r = _dot(st_b, rhs)
        s_mid = (state + r[:, LANES:] + g_c[a]) * wl[a]
        outs.append(o0_c[a] + _dot_nt(q_c[a], st_b))
        outs.append(o0_c[b] + _dot_nt(q_c[b], s_mid.astype(BF16)))
        state = state * w_ab + r[:, :LANES] + g_ab
    state_ref[...] = state
    o = jnp.concatenate(outs, axis=0)

    mean = _pair_sum(o, first) * (1.0 / A_HEAD)
    d = o - mean
    var = _pair_sum(d * d, first) * (1.0 / A_HEAD)
    y = d * lax.rsqrt(var + A_LN_EPS) * ln_w + ln_b
    o_ref[...] = ((y + bonus) * _sigmoid(g)).astype(o_ref.dtype)


def rwkv7_mix(p, mu, w0, w_up, a0, a_up, k_k, k_a, r_k, ln_w, ln_b, tb=2048):
    s = p.shape[0]
    tb = min(tb, s)
    assert s % tb == 0
    npair = A_H // 2
    nblk = A_W // LANES
    zeros = jnp.zeros((A_W,), F32)
    cp = jnp.stack([mu[0:A_W], mu[A_W:2 * A_W], mu[2 * A_W:3 * A_W], mu[3 * A_W:4 * A_W],
                    w0, a0, k_k, k_a, r_k, ln_w, ln_b, zeros, zeros, zeros, zeros, zeros])
    tail = 2 * LANES - 2 * A_LORA
    mulo = jnp.concatenate([mu[4 * A_W:], jnp.zeros((tail,), F32)]).reshape(1, 2 * LANES)
    wup = jnp.concatenate([w_up, jnp.zeros((2 * LANES - A_LORA, A_W), F32)], axis=0).astype(BF16)
    aup = jnp.concatenate([jnp.zeros((A_LORA, A_W), F32), a_up, jnp.zeros((tail, A_W), F32)], axis=0).astype(BF16)
    lora_blk = EV_LORA // (2 * LANES)

    def col(j):
        return pl.BlockSpec((tb, LANES), lambda hp, i, j=j: (i, j * nblk + hp))

    vmem = 2 * (4 * tb * LANES * 4 + tb * 2 * LANES * 4 + tb * LANES * 2) + (24 << 20)
    return pl.pallas_call(
        functools.partial(_rwkv_kernel, tb=tb),
        grid=(npair, s // tb),
        in_specs=[col(0), col(1), col(2), col(3),
                  pl.BlockSpec((tb, 2 * LANES), lambda hp, i: (i, lora_blk)),
                  pl.BlockSpec((16, LANES), lambda hp, i: (0, hp)),
                  pl.BlockSpec((1, 2 * LANES), lambda hp, i: (0, 0)),
                  pl.BlockSpec((2 * LANES, LANES), lambda hp, i: (0, hp)),
                  pl.BlockSpec((2 * LANES, LANES), lambda hp, i: (0, hp))],
        out_specs=pl.BlockSpec((tb, LANES), lambda hp, i: (i, hp)),
        out_shape=jax.ShapeDtypeStruct((s, MIX_W), BF16),
        scratch_shapes=[pltpu.VMEM((LANES, LANES), F32),
                        pltpu.VMEM((SUBLANES, LANES), F32), pltpu.VMEM((SUBLANES, LANES), F32),
                        pltpu.VMEM((SUBLANES, LANES), F32), pltpu.VMEM((SUBLANES, LANES), F32),
                        pltpu.VMEM((SUBLANES, 2 * LANES), F32)],
        compiler_params=_cparams(("parallel", "arbitrary"), vmem),
        name="rwkv7",
    )(p, p, p, p, p, cp, mulo, wup, aup)


def _mlstm_kernel(ib_ref, fb_ref, q_ref, k_ref, v_ref, og_ref, gt_ref, ng_ref, ybuf_ref, o_ref,
                  c_ref, n_ref, m_ref, *, tb):
    del ybuf_ref
    L = CHUNK
    h = pl.program_id(0)
    i = pl.program_id(1)

    @pl.when(i == 0)
    def _():
        c_ref[...] = jnp.zeros_like(c_ref)
        n_ref[...] = jnp.zeros_like(n_ref)
        m_ref[...] = jnp.zeros_like(m_ref)

    i_b = ib_ref[h]
    f_b = fb_ref[h]
    row = lax.broadcasted_iota(jnp.int32, (L, L), 0)
    col = lax.broadcasted_iota(jnp.int32, (L, L), 1)
    causal = row >= col
    eye = row == col

    def to_col(x_row):
        return jnp.sum(jnp.where(eye, jnp.broadcast_to(x_row, (L, L)), 0.0), axis=1, keepdims=True)

    pre = []
    for c in range(tb // L):
        sl = slice(c * L, (c + 1) * L)
        gi = gt_ref[c, pl.ds(h, 1), :]
        gf = gt_ref[c, pl.ds(B_H + h, 1), :]
        li_row = GATE_CAP * jnp.tanh((gi + i_b) / GATE_CAP)
        lf_row = -_softplus(-(GATE_CAP * jnp.tanh((gf + f_b) / GATE_CAP)))
        li_col = to_col(li_row)
        lf_col = to_col(lf_row)
        cum_col = jnp.sum(jnp.where(causal, jnp.broadcast_to(lf_row, (L, L)), 0.0), axis=1, keepdims=True)
        cum_row = jnp.sum(jnp.where(row <= col, jnp.broadcast_to(lf_col, (L, L)), 0.0), axis=0, keepdims=True)
        total = jnp.sum(lf_row, axis=1, keepdims=True)

        qf = q_ref[sl, :]
        q = qf.astype(BF16)
        kf = k_ref[sl, :] * (B_HEAD ** -0.5)
        vb = v_ref[sl, :].astype(BF16)

        d_log = jnp.where(causal, cum_col - cum_row + li_row, -jnp.inf)
        m_loc = jnp.max(d_log, axis=1, keepdims=True)
        s_loc = _dot_nt(q, kf.astype(BF16)) * jnp.exp(d_log - m_loc)
        sv = _dot(s_loc.astype(BF16), vb)
        s_sum = jnp.sum(s_loc, axis=1, keepdims=True)

        log_w_col = total - cum_col + li_col
        m_k = jnp.max(log_w_col, axis=0, keepdims=True)
        kw = kf * jnp.exp(log_w_col - m_k)
        kv = _dot_tn(kw.astype(BF16), vb)
        k_sum = jnp.sum(kw, axis=0, keepdims=True)
        wide = lambda t: jnp.broadcast_to(t, (t.shape[0], B_HEAD))
        pre.append((qf, q, wide(cum_col), wide(total), wide(m_loc), sv, wide(s_sum), wide(m_k), kv, k_sum))

    c_st = c_ref[...]
    n_st = n_ref[...]
    m_st = m_ref[...]
    outs = []
    for qf, q, cum_b, total_b, m_loc_b, sv, s_sum_b, m_k_b, kv, k_sum in pre:
        m_inter = cum_b + m_st
        m_row = jnp.maximum(m_loc_b, m_inter)
        w_loc = jnp.exp(m_loc_b - m_row)
        w_inter = jnp.exp(m_inter - m_row)
        num = w_loc * sv + w_inter * _dot(q, c_st.astype(BF16))
        den = w_loc * s_sum_b + w_inter * jnp.sum(qf * n_st, axis=1, keepdims=True)
        outs.append(num / jnp.maximum(jnp.abs(den), jnp.exp(-m_row)))

        m_new = jnp.maximum(total_b + m_st, m_k_b)
        carry = jnp.exp(total_b + m_st - m_new)
        w_new = jnp.exp(m_k_b - m_new)
        c_st = carry * c_st + w_new * kv
        n_st = carry * n_st + w_new * k_sum
        m_st = m_new
    c_ref[...] = c_st
    n_ref[...] = n_st
    m_ref[...] = m_st

    hh = jnp.concatenate(outs, axis=0) if len(outs) > 1 else outs[0]
    ms = jnp.mean(hh * hh, axis=1, keepdims=True)
    y = hh * lax.rsqrt(ms + NORM_EPS) * ng_ref[...]
    o_ref[...] = (y * _sigmoid(og_ref[...])).astype(o_ref.dtype)


def mlstm_mix(p, gates_t, i_b, f_b, norm_g, y_buf, tb=2048):
    s = p.shape[0]
    tb = min(tb, s)
    assert s % tb == 0
    nblk = B_W // B_HEAD
    out0 = A_W // B_HEAD
    base = EV_A_COLS // B_HEAD

    def col(j):
        return pl.BlockSpec((tb, B_HEAD), lambda h, i, j=j: (i, base + j * nblk + h))

    smem = pl.BlockSpec(memory_space=pltpu.SMEM)
    vmem = 2 * (4 * tb * B_HEAD * 4 + tb * B_HEAD * 2) + (24 << 20)
    return pl.pallas_call(
        functools.partial(_mlstm_kernel, tb=tb),
        grid=(B_H, s // tb),
        in_specs=[smem, smem, col(0), col(1), col(2), col(3),
                  pl.BlockSpec((tb // CHUNK, 2 * B_H, CHUNK), lambda h, i: (i, 0, 0)),
                  pl.BlockSpec((1, B_HEAD), lambda h, i: (0, h)),
                  pl.BlockSpec(memory_space=pl.ANY)],
        out_specs=pl.BlockSpec((tb, B_HEAD), lambda h, i: (i, out0 + h)),
        out_shape=jax.ShapeDtypeStruct(y_buf.shape, BF16),
        input_output_aliases={8: 0},
        scratch_shapes=[pltpu.VMEM((B_HEAD, B_HEAD), F32), pltpu.VMEM((1, B_HEAD), F32),
                        pltpu.VMEM((1, B_HEAD), F32)],
        compiler_params=_cparams(("parallel", "arbitrary"), vmem),
        name="mlstm",
    )(i_b, f_b, p, p, p, p, gates_t, norm_g.reshape(1, B_W), y_buf)


def _retention_kernel(lg_ref, q_ref, k_ref, v_ref, g_ref, cc_ref, ss_ref, ng_ref, o_ref, r_ref, *, tb):
    L = CHUNK
    h = pl.program_id(0)
    i = pl.program_id(1)

    @pl.when(i == 0)
    def _():
        r_ref[...] = jnp.zeros_like(r_ref)

    lg = lg_ref[h]
    row = lax.broadcasted_iota(jnp.int32, (L, L), 0)
    col = lax.broadcasted_iota(jnp.int32, (L, L), 1)
    intra = jnp.exp(lg * jnp.abs(row - col).astype(F32))
    pos = lax.broadcasted_iota(jnp.int32, (L, 1), 0).astype(F32)
    q_decay = jnp.exp(lg * (pos + 1.0))
    k_decay = jnp.exp(lg * (L - 1.0 - pos))
    chunk_decay = jnp.exp(jnp.full((1, C_VHEAD), lg * L, F32))

    cc = cc_ref[...]
    ss = ss_ref[...]

    def rope(t):
        return t * cc + pltpu.roll(t, C_KHEAD // 2, 1) * ss

    q = rope(q_ref[...]).astype(BF16)
    k = rope(k_ref[...]) * (C_KHEAD ** -0.5)
    vb = v_ref[...].astype(BF16)
    sls = [slice(c * L, (c + 1) * L) for c in range(tb // L)]
    sv = [_dot((_dot_nt(q[sl], k[sl].astype(BF16)) * intra).astype(BF16), vb[sl]) for sl in sls]
    kv = [_dot_tn((k[sl] * k_decay).astype(BF16), vb[sl]) for sl in sls]
    r_st = r_ref[...]
    outs = []
    for c, sl in enumerate(sls):
        outs.append(sv[c] + q_decay * _dot(q[sl], r_st.astype(BF16)))
        r_st = chunk_decay * r_st + kv[c]
    r_ref[...] = r_st
    o = jnp.concatenate(outs, axis=0) if len(outs) > 1 else outs[0]
    ms = jnp.mean(o * o, axis=1, keepdims=True)
    y = o * lax.rsqrt(ms + NORM_EPS) * ng_ref[...]
    g = g_ref[...]
    o_ref[...] = (y * (g * _sigmoid(g))).astype(o_ref.dtype)


def retention_mix(p, cc, ss, log_gamma, norm_g, tb=1024):
    s = p.shape[0]
    tb = min(tb, s)
    assert s % tb == 0
    kb = C_KW // C_KHEAD
    vb = (2 * C_KW) // C_VHEAD
    gb = (2 * C_KW + C_VW) // C_VHEAD
    smem = pl.BlockSpec(memory_space=pltpu.SMEM)
    vmem = 2 * (2 * tb * C_KHEAD * 4 + 2 * tb * C_VHEAD * 4 + 2 * tb * C_KHEAD * 4 + tb * C_VHEAD * 2) + (24 << 20)
    return pl.pallas_call(
        functools.partial(_retention_kernel, tb=tb),
        grid=(C_H, s // tb),
        in_specs=[smem,
                  pl.BlockSpec((tb, C_KHEAD), lambda h, i: (i, h)),
                  pl.BlockSpec((tb, C_KHEAD), lambda h, i: (i, kb + h)),
                  pl.BlockSpec((tb, C_VHEAD), lambda h, i: (i, vb + h)),
                  pl.BlockSpec((tb, C_VHEAD), lambda h, i: (i, gb + h)),
                  pl.BlockSpec((tb, C_KHEAD), lambda h, i: (i, 0)),
                  pl.BlockSpec((tb, C_KHEAD), lambda h, i: (i, 0)),
                  pl.BlockSpec((1, C_VHEAD), lambda h, i: (0, h))],
        out_specs=pl.BlockSpec((tb, C_VHEAD), lambda h, i: (i, h)),
        out_shape=jax.ShapeDtypeStruct((s, MIX_W), BF16),
        scratch_shapes=[pltpu.VMEM((C_KHEAD, C_VHEAD), F32)],
        compiler_params=_cparams(("parallel", "arbitrary"), vmem),
        name="retention",
    )(log_gamma, p, p, p, p, cc, ss, norm_g.reshape(1, C_VW))


def _rglru_kernel(x_ref, gate_ref, cw_ref, cp_ref, wr_ref, wi_ref, ybuf_ref, o_ref, px_ref, h_ref, *, tb, ts):
    del ybuf_ref
    i = pl.program_id(1)

    @pl.when(i == 0)
    def _():
        px_ref[...] = jnp.zeros_like(px_ref)
        h_ref[...] = jnp.zeros_like(h_ref)

    cw = cw_ref[...]
    cp = cp_ref[...]
    conv_b, b_r, b_i, lam = cp[0:1], cp[1:2], cp[2:3], cp[3:4]
    x = x_ref[...]
    ext = jnp.concatenate([px_ref[...], x], axis=0)
    xc = conv_b + x * cw[CONV_W - 1:CONV_W]
    for d in range(1, CONV_W):
        xc = xc + pltpu.roll(ext, d, 0)[SUBLANES:] * cw[CONV_W - 1 - d:CONV_W - d]
    px_ref[...] = x[tb - SUBLANES:tb]

    xcb = xc.astype(BF16)
    r = _sigmoid(_dot(xcb, wr_ref[0]) + b_r)
    ig = _sigmoid(_dot(xcb, wi_ref[0]) + b_i)
    log_a = -LRU_C * r * _softplus(-lam)
    a = jnp.exp(log_a)
    u = jnp.sqrt(-jnp.tanh(log_a) * (a * a + 1.0)) * (ig * xc)

    rows = lax.broadcasted_iota(jnp.int32, (tb, 1), 0) % ts
    d = 1
    while d < ts:
        keep = rows >= d
        a_sh = jnp.where(keep, pltpu.roll(a, d, 0), 1.0)
        u_sh = jnp.where(keep, pltpu.roll(u, d, 0), 0.0)
        u = a * u_sh + u
        a = a * a_sh
        d *= 2
    h_prev = h_ref[0:1, :]
    parts = []
    for c in range(tb // ts):
        h_c = a[c * ts:(c + 1) * ts] * h_prev + u[c * ts:(c + 1) * ts]
        h_prev = h_c[ts - 1:ts]
        parts.append(h_c)
    hh = jnp.concatenate(parts, axis=0) if len(parts) > 1 else parts[0]
    h_ref[...] = jnp.broadcast_to(h_prev, h_ref.shape)

    gt = gate_ref[...]
    gelu = 0.5 * gt * (1.0 + jnp.tanh(0.7978845608028654 * (gt + 0.044715 * (gt * gt * gt))))
    o_ref[...] = (hh * gelu).astype(o_ref.dtype)


def rglru_mix(p, conv_w, conv_b, w_r, b_r, w_i, b_i, lam, y_buf, tb=2048, ts=256):
    s = p.shape[0]
    tb = min(tb, s)
    ts = min(ts, tb)
    assert s % tb == 0 and tb % ts == 0
    xb0 = (2 * C_KW + 2 * C_VW) // D_BW
    gt0 = xb0 + D_W // D_BW
    out0 = C_VW // D_BW
    zeros = jnp.zeros((D_W,), F32)
    cp = jnp.stack([conv_b, b_r, b_i, lam, zeros, zeros, zeros, zeros])
    vmem = 2 * (2 * tb * D_BW * 4 + tb * D_BW * 2 + 2 * D_BW * D_BW * 2) + (24 << 20)
    return pl.pallas_call(
        functools.partial(_rglru_kernel, tb=tb, ts=ts),
        grid=(D_BLOCKS, s // tb),
        in_specs=[pl.BlockSpec((tb, D_BW), lambda n, i: (i, xb0 + n)),
                  pl.BlockSpec((tb, D_BW), lambda n, i: (i, gt0 + n)),
                  pl.BlockSpec((CONV_W, D_BW), lambda n, i: (0, n)),
                  pl.BlockSpec((SUBLANES, D_BW), lambda n, i: (0, n)),
                  pl.BlockSpec((1, D_BW, D_BW), lambda n, i: (n, 0, 0)),
                  pl.BlockSpec((1, D_BW, D_BW), lambda n, i: (n, 0, 0)),
                  pl.BlockSpec(memory_space=pl.ANY)],
        out_specs=pl.BlockSpec((tb, D_BW), lambda n, i: (i, out0 + n)),
        out_shape=jax.ShapeDtypeStruct(y_buf.shape, BF16),
        input_output_aliases={6: 0},
        scratch_shapes=[pltpu.VMEM((SUBLANES, D_BW), F32), pltpu.VMEM((SUBLANES, D_BW), F32)],
        compiler_params=_cparams(("parallel", "arbitrary"), vmem),
        name="rglru",
    )(p, p, conv_w, cp, w_r.astype(BF16), w_i.astype(BF16), y_buf)


def _xattn_block_kernel(h_ref, g_ref, g2_ref, wq_ref, k_ref, v_ref, wo_ref, o_ref, xn2_ref, att_ref):
    x = h_ref[...]
    ms = jnp.mean(x * x, axis=-1, keepdims=True)
    xn = (x * lax.rsqrt(ms + NORM_EPS) * g_ref[...]).astype(BF16)
    q = _dot(xn, wq_ref[...]).astype(BF16)
    for hd in range(XA_H):
        sl = slice(hd * XA_HEAD, (hd + 1) * XA_HEAD)
        sc = _dot_nt(q[:, sl], k_ref[:, sl]) * (XA_HEAD ** -0.5)
        sc = sc - jnp.max(sc, axis=1, keepdims=True)
        e = jnp.exp(sc)
        pr = e / jnp.sum(e, axis=1, keepdims=True)
        att_ref[:, sl] = _dot(pr.astype(BF16), v_ref[:, sl]).astype(BF16)
    h_new = x + _dot(att_ref[...], wo_ref[...])
    o_ref[...] = h_new
    ms2 = jnp.mean(h_new * h_new, axis=-1, keepdims=True)
    xn2_ref[...] = (h_new * lax.rsqrt(ms2 + NORM_EPS) * g2_ref[...]).astype(xn2_ref.dtype)


def xattn_block(h, g, g_next, wq, kmem, vmem_, wo, tm=256):
    s, d = h.shape
    nm = kmem.shape[0]
    tm = min(tm, s)
    const = dict(pipeline_mode=pl.Buffered(1))
    vmem = (4 * tm * d * 4 + 2 * tm * d * 2 + 2 * d * XA_W * 2 + 2 * nm * XA_W * 2
            + tm * d * (2 + 4 + 4) + tm * XA_W * (4 + 2 + 2) + (8 << 20))
    row = pl.BlockSpec((tm, d), lambda i: (i, 0))
    gain = pl.BlockSpec((1, d), lambda i: (0, 0))
    return pl.pallas_call(
        _xattn_block_kernel,
        grid=(s // tm,),
        in_specs=[row, gain, gain,
                  pl.BlockSpec((d, XA_W), lambda i: (0, 0), **const),
                  pl.BlockSpec((nm, XA_W), lambda i: (0, 0), **const),
                  pl.BlockSpec((nm, XA_W), lambda i: (0, 0), **const),
                  pl.BlockSpec((XA_W, d), lambda i: (0, 0), **const)],
        out_specs=[row, row],
        out_shape=[jax.ShapeDtypeStruct((s, d), F32), jax.ShapeDtypeStruct((s, d), BF16)],
        scratch_shapes=[pltpu.VMEM((tm, XA_W), BF16)],
        compiler_params=_cparams(("parallel",), vmem),
        name="xattn_block",
    )(h, g.reshape(1, d), g_next.reshape(1, d), wq, kmem, vmem_, wo)


def kernel(x, mem, mem_norm_g, norm_mix_g, norm_xattn_g, norm_mlp_g, xattn_wq, xattn_wk, xattn_wv, xattn_wo, mlp_w1, mlp_w2, even_w_in, even_w_out, rwkv_mu, rwkv_w0, rwkv_w_up, rwkv_a0, rwkv_a_up, rwkv_k_k, rwkv_k_a, rwkv_r_k, rwkv_ln_w, rwkv_ln_b, mlstm_i_b, mlstm_f_b, mlstm_norm_g, odd_w_in, odd_w_out, ret_norm_g, lru_conv_w, lru_conv_b, lru_w_r, lru_b_r, lru_w_i, lru_b_i, lru_lambda, final_norm_g):
    b, s, d = x.shape
    assert b == 1 and d == D_MODEL and s % 256 == 0
    h = x.reshape(s, d)
    mem_n = rmsnorm(mem.reshape(-1, d), mem_norm_g, BF16)
    wq_bf = xattn_wq.astype(BF16)
    wo_bf = xattn_wo.astype(BF16)
    even_wt = jnp.swapaxes(even_w_in, 1, 2)

    positions = jnp.arange(s, dtype=F32)
    inv_freq = ROPE_BASE ** (-jnp.arange(0, C_KHEAD, 2, dtype=F32) / C_KHEAD)
    ang = positions[:, None] * inv_freq[None, :]
    cos, sin = jnp.cos(ang), jnp.sin(ang)
    rope_cc = jnp.concatenate([cos, cos], axis=1)
    rope_ss = jnp.concatenate([-sin, sin], axis=1)
    log_gamma = jnp.log1p(-jnp.exp2(-5.0 - jnp.arange(C_H, dtype=F32)))

    for layer in range(DEPTH):
        j = layer // 2
        xn = rmsnorm(h, norm_mix_g[layer], BF16)
        if layer % 2 == 0:
            p = matmul(xn, even_wt, j, F32, n=EV_COLS, w_transposed=True, tn=EV_TN,
                       segments=((0, 0), (EV_A_COLS // EV_TN, A_COLS), (EV_GATE_TILE, EVEN_COLS - EV_TN)))
            y = rwkv7_mix(p, rwkv_mu[j], rwkv_w0[j], rwkv_w_up[j], rwkv_a0[j], rwkv_a_up[j],
                          rwkv_k_k[j], rwkv_k_a[j], rwkv_r_k[j], rwkv_ln_w[j], rwkv_ln_b[j])
            gates_t = p[:, EV_COLS - 2 * B_H:].reshape(s // CHUNK, CHUNK, 2 * B_H).transpose(0, 2, 1)
            y = mlstm_mix(p, gates_t, mlstm_i_b[j], mlstm_f_b[j], mlstm_norm_g[j], y)
            w_out = even_w_out
        else:
            p = matmul(xn, odd_w_in, j, F32)
            y = retention_mix(p, rope_cc, rope_ss, log_gamma, ret_norm_g[j])
            y = rglru_mix(p, lru_conv_w[j], lru_conv_b[j], lru_w_r[j], lru_b_r[j],
                          lru_w_i[j], lru_b_i[j], lru_lambda[j], y)
            w_out = odd_w_out
        h = matmul(y, w_out, j, F32, epilogue="residual", residual=h)

        kmem = matmul(mem_n, xattn_wk, layer, BF16)
        vmem_ = matmul(mem_n, xattn_wv, layer, BF16)
        h, xn = xattn_block(h, norm_xattn_g[layer], norm_mlp_g[layer], wq_bf[layer], kmem, vmem_, wo_bf[layer])
        u = matmul(xn, mlp_w1, layer, BF16, epilogue="relu2")
        h = matmul(u, mlp_w2, layer, F32, epilogue="residual", residual=h)

    return rmsnorm(h, final_norm_g, F32).reshape(b, s, d)
```

```python
import functools

import jax
import jax.numpy as jnp
from jax import lax
from jax.experimental import pallas as pl
from jax.experimental.pallas import tpu as pltpu

D_MODEL = 4096
DEPTH = 4
CHUNK = 64
NORM_EPS = 1e-6
MIX_W = D_MODEL
A_W = MIX_W // 2
A_HEAD = 64
A_H = A_W // A_HEAD
A_LORA = 96
A_LN_EPS = 64e-5
B_W = MIX_W // 2
B_H = 8
B_HEAD = B_W // B_H
GATE_CAP = 15.0
C_VW = MIX_W // 2
C_H = 8
C_VHEAD = C_VW // C_H
C_KHEAD = C_VHEAD // 2
C_KW = C_H * C_KHEAD
ROPE_BASE = 10000.0
D_W = MIX_W // 2
D_BLOCKS = 16
D_BW = D_W // D_BLOCKS
CONV_W = 4
LRU_C = 8.0
D_FF = 4 * D_MODEL
XA_H = 4
XA_HEAD = 256
XA_W = XA_H * XA_HEAD
A_COLS = 4 * A_W + 2 * A_LORA
B_COLS = 4 * B_W + 2 * B_H

LANES = 128
SUBLANES = 8
VMEM_BYTES_V7X = 64 * 1024 * 1024

EVEN_COLS = A_COLS + B_COLS
EV_TN = 512
EV_LORA = 4 * A_W
EV_A_COLS = EV_LORA + EV_TN
EV_GATE_TILE = (EV_A_COLS + 4 * B_W) // EV_TN
EV_COLS = (EV_GATE_TILE + 1) * EV_TN
assert A_COLS <= EV_LORA + 2 * LANES and EV_A_COLS % B_HEAD == 0

BF16 = jnp.bfloat16
F32 = jnp.float32
EXP_M_HALF = 0.6065306597126334


def _cparams(dims, vmem_bytes):
    limit = min(int(vmem_bytes), VMEM_BYTES_V7X - 4 * 1024 * 1024)
    return pltpu.CompilerParams(dimension_semantics=dims, vmem_limit_bytes=limit)


def _dot(a, b):
    return jnp.dot(a, b, preferred_element_type=F32)


def _dot_nt(a, b):
    return lax.dot_general(a, b, (((1,), (1,)), ((), ())), preferred_element_type=F32)


def _dot_tn(a, b):
    return lax.dot_general(a, b, (((0,), (0,)), ((), ())), preferred_element_type=F32)


def _sigmoid(x):
    return 0.5 * jnp.tanh(0.5 * x) + 0.5


def _softplus(x):
    return jnp.maximum(x, 0.0) + jnp.log1p(jnp.exp(-jnp.abs(x)))


def _rmsnorm_kernel(x_ref, g_ref, o_ref):
    x = x_ref[...]
    ms = jnp.mean(x * x, axis=-1, keepdims=True)
    o_ref[...] = (x * lax.rsqrt(ms + NORM_EPS) * g_ref[...]).astype(o_ref.dtype)


def rmsnorm(x, g, out_dtype, tm=512):
    m, d = x.shape
    tm = min(tm, m)
    vmem = 2 * tm * d * (4 + jnp.dtype(out_dtype).itemsize) + tm * d * 4 + 4 * d * 4 + (4 << 20)
    return pl.pallas_call(
        _rmsnorm_kernel,
        grid=(m // tm,),
        in_specs=[pl.BlockSpec((tm, d), lambda i: (i, 0)), pl.BlockSpec((1, d), lambda i: (0, 0))],
        out_specs=pl.BlockSpec((tm, d), lambda i: (i, 0)),
        out_shape=jax.ShapeDtypeStruct((m, d), out_dtype),
        compiler_params=_cparams(("parallel",), vmem),
        name="rmsnorm",
    )(x, g.reshape(1, d))


def _mm_kernel(*refs, nk, epilogue, w_transposed):
    if epilogue == "residual":
        x_ref, w_ref, r_ref, o_ref = refs
    else:
        x_ref, w_ref, o_ref = refs
        r_ref = None

    def product():
        if w_transposed:
            return _dot_nt(x_ref[...], w_ref[0].astype(BF16))
        return _dot(x_ref[...], w_ref[...].astype(BF16))

    if nk > 1:
        @pl.when(pl.program_id(2) == 0)
        def _():
            o_ref[...] = r_ref[...] + product()

        @pl.when(pl.program_id(2) > 0)
        def _():
            o_ref[...] += product()

        return
    part = product()
    if epilogue == "residual":
        o_ref[...] = r_ref[...] + part
    elif epilogue == "relu2":
        o_ref[...] = jnp.square(jnp.maximum(part, 0.0)).astype(o_ref.dtype)
    else:
        o_ref[...] = part.astype(o_ref.dtype)


def matmul(x, w, layer, out_dtype, n=None, w_transposed=False, segments=((0, 0),), epilogue="none", residual=None,
           tm=2048, tn=None, tk=None):
    m, kdim = x.shape
    n_all = w.shape[1] if w_transposed else w.shape[2]
    n = n_all if n is None else n
    tm = min(tm, m)
    if tk is None:
        tk = kdim if kdim <= 4096 else 2048
    nk = kdim // tk
    single_x = nk == 1 and m > tm
    if tn is None:
        tn = 256 if (nk == 1 and not single_x) else 512
    tn = min(tn, n)
    assert m % tm == 0 and n % tn == 0 and kdim % tk == 0
    if w_transposed:
        ends = [jb for jb, _ in segments[1:]] + [n // tn]
        assert all(r0 % SUBLANES == 0 and r0 + (je - jb) * tn <= n_all for (jb, r0), je in zip(segments, ends))

        def w_row(j):
            row = segments[0][1] + j * tn
            for jb, r0 in segments[1:]:
                row = jnp.where(j >= jb, r0 + (j - jb) * tn, row)
            return pl.multiple_of(row, SUBLANES)

        w_spec = pl.BlockSpec((pl.Element(1), pl.Element(tn), pl.Element(tk)),
                              lambda i, j, k: (layer, w_row(j), pl.multiple_of(k * tk, LANES)))
    else:
        assert tuple(segments) == ((0, 0),)
        w_spec = pl.BlockSpec((pl.Squeezed(), tk, tn), lambda i, j, k: (layer, k, j))
    if single_x:
        x_spec = pl.BlockSpec((tm, tk), lambda i, j, k: (i, k), pipeline_mode=pl.Buffered(1))
    else:
        x_spec = pl.BlockSpec((tm, tk), lambda i, j, k: (i, k))
    in_specs = [x_spec, w_spec]
    args = [x, w]
    osz = jnp.dtype(out_dtype).itemsize
    wsz = jnp.dtype(w.dtype).itemsize
    vmem = ((1 if single_x else 2) * tm * tk * 2 + 2 * (tk * tn * wsz + tm * tn * osz)
            + tk * tn * 2 + 2 * tm * tn * 4 + (8 << 20))
    if epilogue == "residual":
        in_specs.append(pl.BlockSpec((tm, tn), lambda i, j, k: (i, j)))
        args.append(residual)
        vmem += 2 * tm * tn * 4
    assert nk == 1 or (epilogue == "residual" and out_dtype == F32)
    return pl.pallas_call(
        functools.partial(_mm_kernel, nk=nk, epilogue=epilogue, w_transposed=w_transposed),
        grid=(m // tm, n // tn, nk),
        in_specs=in_specs,
        out_specs=pl.BlockSpec((tm, tn), lambda i, j, k: (i, j)),
        out_shape=jax.ShapeDtypeStruct((m, n), out_dtype),
        compiler_params=_cparams(("parallel", "arbitrary", "arbitrary"), vmem),
        name="matmul_" + epilogue,
    )(*args)


def _pair_sum(x, first):
    s1 = jnp.sum(jnp.where(first, x, 0.0), axis=1, keepdims=True)
    s2 = jnp.sum(jnp.where(first, 0.0, x), axis=1, keepdims=True)
    return jnp.where(first, s1, s2)


def _rwkv_kernel(r_ref, k_ref, v_ref, g_ref, lo_ref, cp_ref, mulo_ref, wup_ref, aup_ref, o_ref,
                 state_ref, pr_ref, pk_ref, pv_ref, pg_ref, plo_ref, *, tb):
    L = CHUNK
    i = pl.program_id(1)

    @pl.when(i == 0)
    def _():
        state_ref[...] = jnp.zeros_like(state_ref)
        pr_ref[...] = jnp.zeros_like(pr_ref)
        pk_ref[...] = jnp.zeros_like(pk_ref)
        pv_ref[...] = jnp.zeros_like(pv_ref)
        pg_ref[...] = jnp.zeros_like(pg_ref)
        plo_ref[...] = jnp.zeros_like(plo_ref)

    cp = cp_ref[...]
    mu_r, mu_k, mu_v, mu_g = cp[0:1], cp[1:2], cp[2:3], cp[3:4]
    w0, a0, k_k, k_a, r_k, ln_w, ln_b = (cp[4:5], cp[5:6], cp[6:7], cp[7:8], cp[8:9], cp[9:10], cp[10:11])

    first_row = lax.broadcasted_iota(jnp.int32, (tb, 1), 0) == 0

    def shift_lerp(x_ref, prev_ref, mu):
        x = x_ref[...]
        sh = jnp.where(first_row, prev_ref[SUBLANES - 1:SUBLANES, :], pltpu.roll(x, 1, 0))
        prev_ref[...] = x[tb - SUBLANES:tb]
        return x + (sh - x) * mu

    r = shift_lerp(r_ref, pr_ref, mu_r)
    k = shift_lerp(k_ref, pk_ref, mu_k)
    v = shift_lerp(v_ref, pv_ref, mu_v)
    g = shift_lerp(g_ref, pg_ref, mu_g)
    lo = shift_lerp(lo_ref, plo_ref, mulo_ref[...])

    lane = lax.broadcasted_iota(jnp.int32, (1, LANES), 1)
    first = lane < A_HEAD
    m1 = first.astype(F32)
    m2 = 1.0 - m1

    w = w0 + _dot(jnp.tanh(lo).astype(BF16), wup_ref[...])
    logw = -_sigmoid(w) * EXP_M_HALF
    a = _sigmoid(a0 + _dot(lo.astype(BF16), aup_ref[...]))
    kk = k * k_k
    k = k * (1.0 + (a - 1.0) * k_a)
    kk = kk * lax.rsqrt(_pair_sum(kk * kk, first) + 1e-12)
    alpha = -kk
    beta = kk * a
    bonus = _pair_sum(r * k * r_k, first) * v

    nc = tb // L
    row = lax.broadcasted_iota(jnp.int32, (L, L), 0)
    col = lax.broadcasted_iota(jnp.int32, (L, L), 1)
    tril_strict = row > col
    row2 = lax.broadcasted_iota(jnp.int32, (L, LANES), 0)
    col2 = lax.broadcasted_iota(jnp.int32, (L, LANES), 1) % L
    tril_incl2 = row2 >= col2
    bd_r = lax.broadcasted_iota(jnp.int32, (LANES, LANES), 0) // A_HEAD
    bd_c = lax.broadcasted_iota(jnp.int32, (LANES, LANES), 1) // A_HEAD
    blockdiag = bd_r == bd_c

    pos = lax.broadcasted_iota(jnp.int32, (tb, 1), 0) % L
    cum = logw
    step = 1
    while step < L:
        cum = cum + jnp.where(pos >= step, pltpu.roll(cum, step, 0), 0.0)
        step *= 2
    e_pos = jnp.exp(cum)
    e_neg = jnp.exp(-cum)
    ab = alpha * jnp.exp(cum - logw)
    rb = r * e_pos
    kt = (k * e_neg).astype(BF16)
    bt = (beta * e_neg).astype(BF16)
    vv = v.astype(BF16)
    heads = ((m1, ab * m1, (rb * m1).astype(BF16)), (m2, ab * m2, (rb * m2).astype(BF16)))
    zero_top = jnp.zeros((L, LANES), BF16)

    sls = [slice(c * L, (c + 1) * L) for c in range(nc)]
    kb = [jnp.concatenate([kt[sl], bt[sl]], axis=0) for sl in sls]
    xa = []
    for c, sl in enumerate(sls):
        x = jnp.concatenate([heads[0][1][sl].astype(BF16), heads[1][1][sl].astype(BF16),
                             heads[0][2][sl], heads[1][2][sl]], axis=0)
        xa.append(_dot_nt(x, kb[c]))
    prob = [(c, h) for c in range(nc) for h in range(2)]
    n_pow, ys, a_r = [], [], []
    for c, h in prob:
        an = xa[c][h * L:(h + 1) * L]
        a_ak = jnp.where(tril_strict, an[:, :L], 0.0).astype(BF16)
        n_pow.append(jnp.where(tril_strict, an[:, L:], 0.0).astype(BF16))
        a_r.append(jnp.where(tril_incl2, xa[c][(2 + h) * L:(3 + h) * L], 0.0).astype(BF16))
        ys.append(_dot(a_ak, vv[sls[c]]) * heads[h][0] + pltpu.roll(heads[h][1][sls[c]], A_HEAD, 1))
    for lvl in range(6):
        ys = [y + _dot(n, y.astype(BF16)) for n, y in zip(n_pow, ys)]
        if lvl < 5:
            n_pow = [_dot(n, n).astype(BF16) for n in n_pow]
    q_c, o0_c, g_c, pb_c = [], [], [], []
    for c, sl in enumerate(sls):
        top = jnp.concatenate([vv[sl], zero_top], axis=1)
        y2 = []
        o0 = jnp.zeros((L, LANES), F32)
        qq = rb[sl]
        for h in range(2):
            mh = heads[h][0]
            y = ys[2 * c + h]
            y = jnp.concatenate([y * mh, pltpu.roll(y * (1.0 - mh), A_HEAD, 1)], axis=1)
            y2.append(y)
            out = _dot(a_r[2 * c + h], jnp.concatenate([top, y.astype(BF16)], axis=0))
            o0 = o0 + out[:, :LANES] * mh
            qq = qq + out[:, LANES:]
        ysum = (y2[0] + y2[1]).astype(BF16)
        gp = _dot_tn(jnp.concatenate([top, ysum], axis=0), kb[c])
        g_c.append(jnp.where(blockdiag, gp[:LANES], 0.0))
        pb_c.append(jnp.where(blockdiag, gp[LANES:], 0.0))
        q_c.append(qq.astype(BF16))
        o0_c.append(o0)

    wl = [e_pos[(c + 1) * L - 1:(c + 1) * L] for c in range(nc)]
    eye_f = (lax.broadcasted_iota(jnp.int32, (LANES, LANES), 0)
             == lax.broadcasted_iota(jnp.int32, (LANES, LANES), 1)).astype(F32)
    pairs = []
    for a in range(0, nc, 2):
        b = a + 1
        pb_b = pb_c[b].astype(BF16)
        e_a = (eye_f + pb_c[a]) * wl[a]
        g_a = g_c[a] * wl[a]
        t = _dot(jnp.concatenate([e_a, g_a], axis=0).astype(BF16), pb_b)
        c_ab = (pb_c[a] * wl[a] + t[:LANES]) * wl[b]
        g_ab = (g_a + t[LANES:] + g_c[b]) * wl[b]
        rhs = jnp.concatenate([c_ab.astype(BF16), pb_c[a].astype(BF16)], axis=1)
        pairs.append((rhs, g_ab, wl[a] * wl[b]))

    state = state_ref[...]
    outs = []
    for pi, (rhs, g_ab, w_ab) in enumerate(pairs):
        a, b = 2 * pi, 2 * pi + 1
        st_b = state.astype(BF16)
        r = _dot(st_b, rhs)
        s_mid = (state + r[:, LANES:] + g_c[a]) * wl[a]
        outs.append(o0_c[a] + _dot_nt(q_c[a], st_b))
        outs.append(o0_c[b] + _dot_nt(q_c[b], s_mid.astype(BF16)))
        state = state * w_ab + r[:, :LANES] + g_ab
    state_ref[...] = state
    o = jnp.concatenate(outs, axis=0)

    mean = _pair_sum(o, first) * (1.0 / A_HEAD)
    d = o - mean
    var = _pair_sum(d * d, first) * (1.0 / A_HEAD)
    y = d * lax.rsqrt(var + A_LN_EPS) * ln_w + ln_b
    o_ref[...] = ((y + bonus) * _sigmoid(g)).astype(o_ref.dtype)


def rwkv7_mix(p, mu, w0, w_up, a0, a_up, k_k, k_a, r_k, ln_w, ln_b, tb=2048):
    s = p.shape[0]
    tb = min(tb, s)
    assert s % tb == 0
    npair = A_H // 2
    nblk = A_W // LANES
    zeros = jnp.zeros((A_W,), F32)
    cp = jnp.stack([mu[0:A_W], mu[A_W:2 * A_W], mu[2 * A_W:3 * A_W], mu[3 * A_W:4 * A_W],
                    w0, a0, k_k, k_a, r_k, ln_w, ln_b, zeros, zeros, zeros, zeros, zeros])
    tail = 2 * LANES - 2 * A_LORA
    mulo = jnp.concatenate([mu[4 * A_W:], jnp.zeros((tail,), F32)]).reshape(1, 2 * LANES)
    wup = jnp.concatenate([w_up, jnp.zeros((2 * LANES - A_LORA, A_W), F32)], axis=0).astype(BF16)
    aup = jnp.concatenate([jnp.zeros((A_LORA, A_W), F32), a_up, jnp.zeros((tail, A_W), F32)], axis=0).astype(BF16)
    lora_blk = EV_LORA // (2 * LANES)

    def col(j):
        return pl.BlockSpec((tb, LANES), lambda hp, i, j=j: (i, j * nblk + hp))

    vmem = 2 * (4 * tb * LANES * 4 + tb * 2 * LANES * 4 + tb * LANES * 2) + (24 << 20)
    return pl.pallas_call(
        functools.partial(_rwkv_kernel, tb=tb),
        grid=(npair, s // tb),
        in_specs=[col(0), col(1), col(2), col(3),
                  pl.BlockSpec((tb, 2 * LANES), lambda hp, i: (i, lora_blk)),
                  pl.BlockSpec((16, LANES), lambda hp, i: (0, hp)),
                  pl.BlockSpec((1, 2 * LANES), lambda hp, i: (0, 0)),
                  pl.BlockSpec((2 * LANES, LANES), lambda hp, i: (0, hp)),
                  pl.BlockSpec((2 * LANES, LANES), lambda hp, i: (0, hp))],
        out_specs=pl.BlockSpec((tb, LANES), lambda hp, i: (i, hp)),
        out_shape=jax.ShapeDtypeStruct((s, MIX_W), BF16),
        scratch_shapes=[pltpu.VMEM((LANES, LANES), F32),
                        pltpu.VMEM((SUBLANES, LANES), F32), pltpu.VMEM((SUBLANES, LANES), F32),
                        pltpu.VMEM((SUBLANES, LANES), F32), pltpu.VMEM((SUBLANES, LANES), F32),
                        pltpu.VMEM((SUBLANES, 2 * LANES), F32)],
        compiler_params=_cparams(("parallel", "arbitrary"), vmem),
        name="rwkv7",
    )(p, p, p, p, p, cp, mulo, wup, aup)


def _mlstm_kernel(ib_ref, fb_ref, q_ref, k_ref, v_ref, og_ref, gt_ref, ng_ref, ybuf_ref, o_ref,
                  c_ref, n_ref, m_ref, *, tb):
    del ybuf_ref
    L = CHUNK
    h = pl.program_id(0)
    i = pl.program_id(1)

    @pl.when(i == 0)
    def _():
        c_ref[...] = jnp.zeros_like(c_ref)
        n_ref[...] = jnp.zeros_like(n_ref)
        m_ref[...] = jnp.zeros_like(m_ref)

    i_b = ib_ref[h]
    f_b = fb_ref[h]
    row = lax.broadcasted_iota(jnp.int32, (L, L), 0)
    col = lax.broadcasted_iota(jnp.int32, (L, L), 1)
    causal = row >= col
    eye = row == col

    def to_col(x_row):
        return jnp.sum(jnp.where(eye, jnp.broadcast_to(x_row, (L, L)), 0.0), axis=1, keepdims=True)

    pre = []
    for c in range(tb // L):
        sl = slice(c * L, (c + 1) * L)
        gi = gt_ref[c, pl.ds(h, 1), :]
        gf = gt_ref[c, pl.ds(B_H + h, 1), :]
        li_row = GATE_CAP * jnp.tanh((gi + i_b) / GATE_CAP)
        lf_row = -_softplus(-(GATE_CAP * jnp.tanh((gf + f_b) / GATE_CAP)))
        li_col = to_col(li_row)
        lf_col = to_col(lf_row)
        cum_col = jnp.sum(jnp.where(causal, jnp.broadcast_to(lf_row, (L, L)), 0.0), axis=1, keepdims=True)
        cum_row = jnp.sum(jnp.where(row <= col, jnp.broadcast_to(lf_col, (L, L)), 0.0), axis=0, keepdims=True)
        total = jnp.sum(lf_row, axis=1, keepdims=True)

        qf = q_ref[sl, :]
        q = qf.astype(BF16)
        kf = k_ref[sl, :] * (B_HEAD ** -0.5)
        vb = v_ref[sl, :].astype(BF16)

        d_log = jnp.where(causal, cum_col - cum_row + li_row, -jnp.inf)
        m_loc = jnp.max(d_log, axis=1, keepdims=True)
        s_loc = _dot_nt(q, kf.astype(BF16)) * jnp.exp(d_log - m_loc)
        sv = _dot(s_loc.astype(BF16), vb)
        s_sum = jnp.sum(s_loc, axis=1, keepdims=True)

        log_w_col = total - cum_col + li_col
        m_k = jnp.max(log_w_col, axis=0, keepdims=True)
        kw = kf * jnp.exp(log_w_col - m_k)
        kv = _dot_tn(kw.astype(BF16), vb)
        k_sum = jnp.sum(kw, axis=0, keepdims=True)
        wide = lambda t: jnp.broadcast_to(t, (t.shape[0], B_HEAD))
        pre.append((qf, q, wide(cum_col), wide(total), wide(m_loc), sv, wide(s_sum), wide(m_k), kv, k_sum))

    c_st = c_ref[...]
    n_st = n_ref[...]
    m_st = m_ref[...]
    outs = []
    for qf, q, cum_b, total_b, m_loc_b, sv, s_sum_b, m_k_b, kv, k_sum in pre:
        m_inter = cum_b + m_st
        m_row = jnp.maximum(m_loc_b, m_inter)
        w_loc = jnp.exp(m_loc_b - m_row)
        w_inter = jnp.exp(m_inter - m_row)
        num = w_loc * sv + w_inter * _dot(q, c_st.astype(BF16))
        den = w_loc * s_sum_b + w_inter * jnp.sum(qf * n_st, axis=1, keepdims=True)
        outs.append(num / jnp.maximum(jnp.abs(den), jnp.exp(-m_row)))

        m_new = jnp.maximum(total_b + m_st, m_k_b)
        carry = jnp.exp(total_b + m_st - m_new)
        w_new = jnp.exp(m_k_b - m_new)
        c_st = carry * c_st + w_new * kv
        n_st = carry * n_st + w_new * k_sum
        m_st = m_new
    c_ref[...] = c_st
    n_ref[...] = n_st
    m_ref[...] = m_st

    hh = jnp.concatenate(outs, axis=0) if len(outs) > 1 else outs[0]
    ms = jnp.mean(hh * hh, axis=1, keepdims=True)
    y = hh * lax.rsqrt(ms + NORM_EPS) * ng_ref[...]
    o_ref[...] = (y * _sigmoid(og_ref[...])).astype(o_ref.dtype)


def mlstm_mix(p, gates_t, i_b, f_b, norm_g, y_buf, tb=2048):
    s = p.shape[0]
    tb = min(tb, s)
    assert s % tb == 0
    nblk = B_W // B_HEAD
    out0 = A_W // B_HEAD
    base = EV_A_COLS // B_HEAD

    def col(j):
        return pl.BlockSpec((tb, B_HEAD), lambda h, i, j=j: (i, base + j * nblk + h))

    smem = pl.BlockSpec(memory_space=pltpu.SMEM)
    vmem = 2 * (4 * tb * B_HEAD * 4 + tb * B_HEAD * 2) + (24 << 20)
    return pl.pallas_call(
        functools.partial(_mlstm_kernel, tb=tb),
        grid=(B_H, s // tb),
        in_specs=[smem, smem, col(0), col(1), col(2), col(3),
                  pl.BlockSpec((tb // CHUNK, 2 * B_H, CHUNK), lambda h, i: (i, 0, 0)),
                  pl.BlockSpec((1, B_HEAD), lambda h, i: (0, h)),
                  pl.BlockSpec(memory_space=pl.ANY)],
        out_specs=pl.BlockSpec((tb, B_HEAD), lambda h, i: (i, out0 + h)),
        out_shape=jax.ShapeDtypeStruct(y_buf.shape, BF16),
        input_output_aliases={8: 0},
        scratch_shapes=[pltpu.VMEM((B_HEAD, B_HEAD), F32), pltpu.VMEM((1, B_HEAD), F32),
                        pltpu.VMEM((1, B_HEAD), F32)],
        compiler_params=_cparams(("parallel", "arbitrary"), vmem),
        name="mlstm",
    )(i_b, f_b, p, p, p, p, gates_t, norm_g.reshape(1, B_W), y_buf)


def _retention_kernel(lg_ref, q_ref, k_ref, v_ref, g_ref, cc_ref, ss_ref, ng_ref, o_ref, r_ref, *, tb):
    L = CHUNK
    h = pl.program_id(0)
    i = pl.program_id(1)

    @pl.when(i == 0)
    def _():
        r_ref[...] = jnp.zeros_like(r_ref)

    lg = lg_ref[h]
    row = lax.broadcasted_iota(jnp.int32, (L, L), 0)
    col = lax.broadcasted_iota(jnp.int32, (L, L), 1)
    intra = jnp.exp(lg * jnp.abs(row - col).astype(F32))
    pos = lax.broadcasted_iota(jnp.int32, (L, 1), 0).astype(F32)
    q_decay = jnp.exp(lg * (pos + 1.0))
    k_decay = jnp.exp(lg * (L - 1.0 - pos))
    chunk_decay = jnp.exp(jnp.full((1, C_VHEAD), lg * L, F32))

    cc = cc_ref[...]
    ss = ss_ref[...]

    def rope(t):
        return t * cc + pltpu.roll(t, C_KHEAD // 2, 1) * ss

    q = rope(q_ref[...]).astype(BF16)
    k = rope(k_ref[...]) * (C_KHEAD ** -0.5)
    vb = v_ref[...].astype(BF16)
    sls = [slice(c * L, (c + 1) * L) for c in range(tb // L)]
    sv = [_dot((_dot_nt(q[sl], k[sl].astype(BF16)) * intra).astype(BF16), vb[sl]) for sl in sls]
    kv = [_dot_tn((k[sl] * k_decay).astype(BF16), vb[sl]) for sl in sls]
    r_st = r_ref[...]
    outs = []
    for c, sl in enumerate(sls):
        outs.append(sv[c] + q_decay * _dot(q[sl], r_st.astype(BF16)))
        r_st = chunk_decay * r_st + kv[c]
    r_ref[...] = r_st
    o = jnp.concatenate(outs, axis=0) if len(outs) > 1 else outs[0]
    ms = jnp.mean(o * o, axis=1, keepdims=True)
    y = o * lax.rsqrt(ms + NORM_EPS) * ng_ref[...]
    g = g_ref[...]
    o_ref[...] = (y * (g * _sigmoid(g))).astype(o_ref.dtype)


def retention_mix(p, cc, ss, log_gamma, norm_g, tb=2048):
    s = p.shape[0]
    tb = min(tb, s)
    assert s % tb == 0
    kb = C_KW // C_KHEAD
    vb = (2 * C_KW) // C_VHEAD
    gb = (2 * C_KW + C_VW) // C_VHEAD
    smem = pl.BlockSpec(memory_space=pltpu.SMEM)
    vmem = 2 * (2 * tb * C_KHEAD * 4 + 2 * tb * C_VHEAD * 4 + 2 * tb * C_KHEAD * 4 + tb * C_VHEAD * 2) + (24 << 20)
    return pl.pallas_call(
        functools.partial(_retention_kernel, tb=tb),
        grid=(C_H, s // tb),
        in_specs=[smem,
                  pl.BlockSpec((tb, C_KHEAD), lambda h, i: (i, h)),
                  pl.BlockSpec((tb, C_KHEAD), lambda h, i: (i, kb + h)),
                  pl.BlockSpec((tb, C_VHEAD), lambda h, i: (i, vb + h)),
                  pl.BlockSpec((tb, C_VHEAD), lambda h, i: (i, gb + h)),
                  pl.BlockSpec((tb, C_KHEAD), lambda h, i: (i, 0)),
                  pl.BlockSpec((tb, C_KHEAD), lambda h, i: (i, 0)),
                  pl.BlockSpec((1, C_VHEAD), lambda h, i: (0, h))],
        out_specs=pl.BlockSpec((tb, C_VHEAD), lambda h, i: (i, h)),
        out_shape=jax.ShapeDtypeStruct((s, MIX_W), BF16),
        scratch_shapes=[pltpu.VMEM((C_KHEAD, C_VHEAD), F32)],
        compiler_params=_cparams(("parallel", "arbitrary"), vmem),
        name="retention",
    )(log_gamma, p, p, p, p, cc, ss, norm_g.reshape(1, C_VW))


def _rglru_kernel(x_ref, gate_ref, cw_ref, cp_ref, wr_ref, wi_ref, ybuf_ref, o_ref, px_ref, h_ref, *, tb, ts):
    del ybuf_ref
    i = pl.program_id(1)

    @pl.when(i == 0)
    def _():
        px_ref[...] = jnp.zeros_like(px_ref)
        h_ref[...] = jnp.zeros_like(h_ref)

    cw = cw_ref[...]
    cp = cp_ref[...]
    conv_b, b_r, b_i, lam = cp[0:1], cp[1:2], cp[2:3], cp[3:4]
    x = x_ref[...]
    ext = jnp.concatenate([px_ref[...], x], axis=0)
    xc = conv_b + x * cw[CONV_W - 1:CONV_W]
    for d in range(1, CONV_W):
        xc = xc + pltpu.roll(ext, d, 0)[SUBLANES:] * cw[CONV_W - 1 - d:CONV_W - d]
    px_ref[...] = x[tb - SUBLANES:tb]

    xcb = xc.astype(BF16)
    r = _sigmoid(_dot(xcb, wr_ref[0]) + b_r)
    ig = _sigmoid(_dot(xcb, wi_ref[0]) + b_i)
    log_a = -LRU_C * r * _softplus(-lam)
    a = jnp.exp(log_a)
    u = jnp.sqrt(-jnp.tanh(log_a) * (a * a + 1.0)) * (ig * xc)

    rows = lax.broadcasted_iota(jnp.int32, (tb, 1), 0) % ts
    d = 1
    while d < ts:
        keep = rows >= d
        a_sh = jnp.where(keep, pltpu.roll(a, d, 0), 1.0)
        u_sh = jnp.where(keep, pltpu.roll(u, d, 0), 0.0)
        u = a * u_sh + u
        a = a * a_sh
        d *= 2
    h_prev = h_ref[0:1, :]
    parts = []
    for c in range(tb // ts):
        h_c = a[c * ts:(c + 1) * ts] * h_prev + u[c * ts:(c + 1) * ts]
        h_prev = h_c[ts - 1:ts]
        parts.append(h_c)
    hh = jnp.concatenate(parts, axis=0) if len(parts) > 1 else parts[0]
    h_ref[...] = jnp.broadcast_to(h_prev, h_ref.shape)

    gt = gate_ref[...]
    gelu = 0.5 * gt * (1.0 + jnp.tanh(0.7978845608028654 * (gt + 0.044715 * (gt * gt * gt))))
    o_ref[...] = (hh * gelu).astype(o_ref.dtype)


def rglru_mix(p, conv_w, conv_b, w_r, b_r, w_i, b_i, lam, y_buf, tb=2048, ts=256):
    s = p.shape[0]
    tb = min(tb, s)
    ts = min(ts, tb)
    assert s % tb == 0 and tb % ts == 0
    xb0 = (2 * C_KW + 2 * C_VW) // D_BW
    gt0 = xb0 + D_W // D_BW
    out0 = C_VW // D_BW
    zeros = jnp.zeros((D_W,), F32)
    cp = jnp.stack([conv_b, b_r, b_i, lam, zeros, zeros, zeros, zeros])
    vmem = 2 * (2 * tb * D_BW * 4 + tb * D_BW * 2 + 2 * D_BW * D_BW * 2) + (24 << 20)
    return pl.pallas_call(
        functools.partial(_rglru_kernel, tb=tb, ts=ts),
        grid=(D_BLOCKS, s // tb),
        in_specs=[pl.BlockSpec((tb, D_BW), lambda n, i: (i, xb0 + n)),
                  pl.BlockSpec((tb, D_BW), lambda n, i: (i, gt0 + n)),
                  pl.BlockSpec((CONV_W, D_BW), lambda n, i: (0, n)),
                  pl.BlockSpec((SUBLANES, D_BW), lambda n, i: (0, n)),
                  pl.BlockSpec((1, D_BW, D_BW), lambda n, i: (n, 0, 0)),
                  pl.BlockSpec((1, D_BW, D_BW), lambda n, i: (n, 0, 0)),
                  pl.BlockSpec(memory_space=pl.ANY)],
        out_specs=pl.BlockSpec((tb, D_BW), lambda n, i: (i, out0 + n)),
        out_shape=jax.ShapeDtypeStruct(y_buf.shape, BF16),
        input_output_aliases={6: 0},
        scratch_shapes=[pltpu.VMEM((SUBLANES, D_BW), F32), pltpu.VMEM((SUBLANES, D_BW), F32)],
        compiler_params=_cparams(("parallel", "arbitrary"), vmem),
        name="rglru",
    )(p, p, conv_w, cp, w_r.astype(BF16), w_i.astype(BF16), y_buf)


def _xattn_block_kernel(h_ref, g_ref, g2_ref, wq_ref, k_ref, v_ref, wo_ref, o_ref, xn2_ref, att_ref):
    x = h_ref[...]
    ms = jnp.mean(x * x, axis=-1, keepdims=True)
    xn = (x * lax.rsqrt(ms + NORM_EPS) * g_ref[...]).astype(BF16)
    q = _dot(xn, wq_ref[...]).astype(BF16)
    for hd in range(XA_H):
        sl = slice(hd * XA_HEAD, (hd + 1) * XA_HEAD)
        sc = _dot_nt(q[:, sl], k_ref[:, sl]) * (XA_HEAD ** -0.5)
        sc = sc - jnp.max(sc, axis=1, keepdims=True)
        e = jnp.exp(sc)
        pr = e / jnp.sum(e, axis=1, keepdims=True)
        att_ref[:, sl] = _dot(pr.astype(BF16), v_ref[:, sl]).astype(BF16)
    h_new = x + _dot(att_ref[...], wo_ref[...])
    o_ref[...] = h_new
    ms2 = jnp.mean(h_new * h_new, axis=-1, keepdims=True)
    xn2_ref[...] = (h_new * lax.rsqrt(ms2 + NORM_EPS) * g2_ref[...]).astype(xn2_ref.dtype)


def xattn_block(h, g, g_next, wq, kmem, vmem_, wo, tm=256):
    s, d = h.shape
    nm = kmem.shape[0]
    tm = min(tm, s)
    const = dict(pipeline_mode=pl.Buffered(1))
    vmem = (4 * tm * d * 4 + 2 * tm * d * 2 + 2 * d * XA_W * 2 + 2 * nm * XA_W * 2
            + tm * d * (2 + 4 + 4) + tm * XA_W * (4 + 2 + 2) + (8 << 20))
    row = pl.BlockSpec((tm, d), lambda i: (i, 0))
    gain = pl.BlockSpec((1, d), lambda i: (0, 0))
    return pl.pallas_call(
        _xattn_block_kernel,
        grid=(s // tm,),
        in_specs=[row, gain, gain,
                  pl.BlockSpec((d, XA_W), lambda i: (0, 0), **const),
                  pl.BlockSpec((nm, XA_W), lambda i: (0, 0), **const),
                  pl.BlockSpec((nm, XA_W), lambda i: (0, 0), **const),
                  pl.BlockSpec((XA_W, d), lambda i: (0, 0), **const)],
        out_specs=[row, row],
        out_shape=[jax.ShapeDtypeStruct((s, d), F32), jax.ShapeDtypeStruct((s, d), BF16)],
        scratch_shapes=[pltpu.VMEM((tm, XA_W), BF16)],
        compiler_params=_cparams(("parallel",), vmem),
        name="xattn_block",
    )(h, g.reshape(1, d), g_next.reshape(1, d), wq, kmem, vmem_, wo)


def kernel(x, mem, mem_norm_g, norm_mix_g, norm_xattn_g, norm_mlp_g, xattn_wq, xattn_wk, xattn_wv, xattn_wo, mlp_w1, mlp_w2, even_w_in, even_w_out, rwkv_mu, rwkv_w0, rwkv_w_up, rwkv_a0, rwkv_a_up, rwkv_k_k, rwkv_k_a, rwkv_r_k, rwkv_ln_w, rwkv_ln_b, mlstm_i_b, mlstm_f_b, mlstm_norm_g, odd_w_in, odd_w_out, ret_norm_g, lru_conv_w, lru_conv_b, lru_w_r, lru_b_r, lru_w_i, lru_b_i, lru_lambda, final_norm_g):
    b, s, d = x.shape
    assert b == 1 and d == D_MODEL and s % 256 == 0
    h = x.reshape(s, d)
    mem_n = rmsnorm(mem.reshape(-1, d), mem_norm_g, BF16)
    wq_bf = xattn_wq.astype(BF16)
    wo_bf = xattn_wo.astype(BF16)
    even_wt = jnp.swapaxes(even_w_in, 1, 2)

    positions = jnp.arange(s, dtype=F32)
    inv_freq = ROPE_BASE ** (-jnp.arange(0, C_KHEAD, 2, dtype=F32) / C_KHEAD)
    ang = positions[:, None] * inv_freq[None, :]
    cos, sin = jnp.cos(ang), jnp.sin(ang)
    rope_cc = jnp.concatenate([cos, cos], axis=1)
    rope_ss = jnp.concatenate([-sin, sin], axis=1)
    log_gamma = jnp.log1p(-jnp.exp2(-5.0 - jnp.arange(C_H, dtype=F32)))

    for layer in range(DEPTH):
        j = layer // 2
        xn = rmsnorm(h, norm_mix_g[layer], BF16)
        if layer % 2 == 0:
            p = matmul(xn, even_wt, j, F32, n=EV_COLS, w_transposed=True, tn=EV_TN,
                       segments=((0, 0), (EV_A_COLS // EV_TN, A_COLS), (EV_GATE_TILE, EVEN_COLS - EV_TN)))
            y = rwkv7_mix(p, rwkv_mu[j], rwkv_w0[j], rwkv_w_up[j], rwkv_a0[j], rwkv_a_up[j],
                          rwkv_k_k[j], rwkv_k_a[j], rwkv_r_k[j], rwkv_ln_w[j], rwkv_ln_b[j])
            gates_t = p[:, EV_COLS - 2 * B_H:].reshape(s // CHUNK, CHUNK, 2 * B_H).transpose(0, 2, 1)
            y = mlstm_mix(p, gates_t, mlstm_i_b[j], mlstm_f_b[j], mlstm_norm_g[j], y)
            w_out = even_w_out
        else:
            p = matmul(xn, odd_w_in, j, F32)
            y = retention_mix(p, rope_cc, rope_ss, log_gamma, ret_norm_g[j])
            y = rglru_mix(p, lru_conv_w[j], lru_conv_b[j], lru_w_r[j], lru_b_r[j],
                          lru_w_i[j], lru_b_i[j], lru_lambda[j], y)
            w_out = odd_w_out
        h = matmul(y, w_out, j, F32, epilogue="residual", residual=h)

        kmem = matmul(mem_n, xattn_wk, layer, BF16)
        vmem_ = matmul(mem_n, xattn_wv, layer, BF16)
        h, xn = xattn_block(h, norm_xattn_g[layer], norm_mlp_g[layer], wq_bf[layer], kmem, vmem_, wo_bf[layer])
        u = matmul(xn, mlp_w1, layer, BF16, epilogue="relu2")
        h = matmul(u, mlp_w2, layer, F32, epilogue="residual", residual=h)

    return rmsnorm(h, final_norm_g, F32).reshape(b, s, d)
```

```python
import functools

import jax
import jax.numpy as jnp
from jax import lax
from jax.experimental import pallas as pl
from jax.experimental.pallas import tpu as pltpu

D_MODEL = 4096
DEPTH = 4
CHUNK = 64
NORM_EPS = 1e-6
MIX_W = D_MODEL
A_W = MIX_W // 2
A_HEAD = 64
A_H = A_W // A_HEAD
A_LORA = 96
A_LN_EPS = 64e-5
B_W = MIX_W // 2
B_H = 8
B_HEAD = B_W // B_H
GATE_CAP = 15.0
C_VW = MIX_W // 2
C_H = 8
C_VHEAD = C_VW // C_H
C_KHEAD = C_VHEAD // 2
C_KW = C_H * C_KHEAD
ROPE_BASE = 10000.0
D_W = MIX_W // 2
D_BLOCKS = 16
D_BW = D_W // D_BLOCKS
CONV_W = 4
LRU_C = 8.0
D_FF = 4 * D_MODEL
XA_H = 4
XA_HEAD = 256
XA_W = XA_H * XA_HEAD
A_COLS = 4 * A_W + 2 * A_LORA
B_COLS = 4 * B_W + 2 * B_H

LANES = 128
SUBLANES = 8
VMEM_BYTES_V7X = 64 * 1024 * 1024

EVEN_COLS = A_COLS + B_COLS
EV_TN = 512
EV_LORA = 4 * A_W
EV_A_COLS = EV_LORA + EV_TN
EV_GATE_TILE = (EV_A_COLS + 4 * B_W) // EV_TN
EV_COLS = (EV_GATE_TILE + 1) * EV_TN
assert A_COLS <= EV_LORA + 2 * LANES and EV_A_COLS % B_HEAD == 0

BF16 = jnp.bfloat16
F32 = jnp.float32
EXP_M_HALF = 0.6065306597126334


def _cparams(dims, vmem_bytes):
    limit = min(int(vmem_bytes), VMEM_BYTES_V7X - 4 * 1024 * 1024)
    return pltpu.CompilerParams(dimension_semantics=dims, vmem_limit_bytes=limit)


def _dot(a, b):
    return jnp.dot(a, b, preferred_element_type=F32)


def _dot_nt(a, b):
    return lax.dot_general(a, b, (((1,), (1,)), ((), ())), preferred_element_type=F32)


def _dot_tn(a, b):
    return lax.dot_general(a, b, (((0,), (0,)), ((), ())), preferred_element_type=F32)


def _sigmoid(x):
    return 0.5 * jnp.tanh(0.5 * x) + 0.5


def _softplus(x):
    return jnp.maximum(x, 0.0) + jnp.log1p(jnp.exp(-jnp.abs(x)))


def _rmsnorm_kernel(x_ref, g_ref, o_ref):
    x = x_ref[...]
    ms = jnp.mean(x * x, axis=-1, keepdims=True)
    o_ref[...] = (x * lax.rsqrt(ms + NORM_EPS) * g_ref[...]).astype(o_ref.dtype)


def rmsnorm(x, g, out_dtype, tm=512):
    m, d = x.shape
    tm = min(tm, m)
    vmem = 2 * tm * d * (4 + jnp.dtype(out_dtype).itemsize) + tm * d * 4 + 4 * d * 4 + (4 << 20)
    return pl.pallas_call(
        _rmsnorm_kernel,
        grid=(m // tm,),
        in_specs=[pl.BlockSpec((tm, d), lambda i: (i, 0)), pl.BlockSpec((1, d), lambda i: (0, 0))],
        out_specs=pl.BlockSpec((tm, d), lambda i: (i, 0)),
        out_shape=jax.ShapeDtypeStruct((m, d), out_dtype),
        compiler_params=_cparams(("parallel",), vmem),
        name="rmsnorm",
    )(x, g.reshape(1, d))


def _mm_kernel(*refs, nk, epilogue, w_transposed):
    if epilogue == "residual":
        x_ref, w_ref, r_ref, o_ref = refs
    else:
        x_ref, w_ref, o_ref = refs
        r_ref = None

    def product():
        if w_transposed:
            return _dot_nt(x_ref[...], w_ref[0].astype(BF16))
        return _dot(x_ref[...], w_ref[...].astype(BF16))

    if nk > 1:
        @pl.when(pl.program_id(2) == 0)
        def _():
            o_ref[...] = r_ref[...] + product()

        @pl.when(pl.program_id(2) > 0)
        def _():
            o_ref[...] += product()

        return
    part = product()
    if epilogue == "residual":
        o_ref[...] = r_ref[...] + part
    elif epilogue == "relu2":
        o_ref[...] = jnp.square(jnp.maximum(part, 0.0)).astype(o_ref.dtype)
    else:
        o_ref[...] = part.astype(o_ref.dtype)


def matmul(x, w, layer, out_dtype, n=None, w_transposed=False, segments=((0, 0),), epilogue="none", residual=None,
           tm=2048, tn=None, tk=None):
    m, kdim = x.shape
    n_all = w.shape[1] if w_transposed else w.shape[2]
    n = n_all if n is None else n
    tm = min(tm, m)
    if tk is None:
        tk = kdim if kdim <= 4096 else 1024
    nk = kdim // tk
    single_x = nk == 1 and m > tm
    if tn is None:
        tn = 1024 if nk > 1 else (512 if single_x else 256)
    tn = min(tn, n)
    assert m % tm == 0 and n % tn == 0 and kdim % tk == 0
    if w_transposed:
        ends = [jb for jb, _ in segments[1:]] + [n // tn]
        assert all(r0 % SUBLANES == 0 and r0 + (je - jb) * tn <= n_all for (jb, r0), je in zip(segments, ends))

        def w_row(j):
            row = segments[0][1] + j * tn
            for jb, r0 in segments[1:]:
                row = jnp.where(j >= jb, r0 + (j - jb) * tn, row)
            return pl.multiple_of(row, SUBLANES)

        w_spec = pl.BlockSpec((pl.Element(1), pl.Element(tn), pl.Element(tk)),
                              lambda i, j, k: (layer, w_row(j), pl.multiple_of(k * tk, LANES)))
    else:
        assert tuple(segments) == ((0, 0),)
        w_spec = pl.BlockSpec((pl.Squeezed(), tk, tn), lambda i, j, k: (layer, k, j))
    if single_x:
        x_spec = pl.BlockSpec((tm, tk), lambda i, j, k: (i, k), pipeline_mode=pl.Buffered(1))
    else:
        x_spec = pl.BlockSpec((tm, tk), lambda i, j, k: (i, k))
    in_specs = [x_spec, w_spec]
    args = [x, w]
    osz = jnp.dtype(out_dtype).itemsize
    wsz = jnp.dtype(w.dtype).itemsize
    vmem = ((1 if single_x else 2) * tm * tk * 2 + 2 * (tk * tn * wsz + tm * tn * osz)
            + tk * tn * 2 + 2 * tm * tn * 4 + (8 << 20))
    if epilogue == "residual":
        in_specs.append(pl.BlockSpec((tm, tn), lambda i, j, k: (i, j)))
        args.append(residual)
        vmem += 2 * tm * tn * 4
    assert nk == 1 or (epilogue == "residual" and out_dtype == F32)
    return pl.pallas_call(
        functools.partial(_mm_kernel, nk=nk, epilogue=epilogue, w_transposed=w_transposed),
        grid=(m // tm, n // tn, nk),
        in_specs=in_specs,
        out_specs=pl.BlockSpec((tm, tn), lambda i, j, k: (i, j)),
        out_shape=jax.ShapeDtypeStruct((m, n), out_dtype),
        compiler_params=_cparams(("parallel", "arbitrary", "arbitrary"), vmem),
        name="matmul_" + epilogue,
    )(*args)


def _pair_sum(x, first):
    s1 = jnp.sum(jnp.where(first, x, 0.0), axis=1, keepdims=True)
    s2 = jnp.sum(jnp.where(first, 0.0, x), axis=1, keepdims=True)
    return jnp.where(first, s1, s2)


def _rwkv_kernel(r_ref, k_ref, v_ref, g_ref, lo_ref, cp_ref, mulo_ref, wup_ref, aup_ref, o_ref,
                 state_ref, pr_ref, pk_ref, pv_ref, pg_ref, plo_ref, *, tb):
    L = CHUNK
    i = pl.program_id(1)

    @pl.when(i == 0)
    def _():
        state_ref[...] = jnp.zeros_like(state_ref)
        pr_ref[...] = jnp.zeros_like(pr_ref)
        pk_ref[...] = jnp.zeros_like(pk_ref)
        pv_ref[...] = jnp.zeros_like(pv_ref)
        pg_ref[...] = jnp.zeros_like(pg_ref)
        plo_ref[...] = jnp.zeros_like(plo_ref)

    cp = cp_ref[...]
    mu_r, mu_k, mu_v, mu_g = cp[0:1], cp[1:2], cp[2:3], cp[3:4]
    w0, a0, k_k, k_a, r_k, ln_w, ln_b = (cp[4:5], cp[5:6], cp[6:7], cp[7:8], cp[8:9], cp[9:10], cp[10:11])

    first_row = lax.broadcasted_iota(jnp.int32, (tb, 1), 0) == 0

    def shift_lerp(x_ref, prev_ref, mu):
        x = x_ref[...]
        sh = jnp.where(first_row, prev_ref[SUBLANES - 1:SUBLANES, :], pltpu.roll(x, 1, 0))
        prev_ref[...] = x[tb - SUBLANES:tb]
        return x + (sh - x) * mu

    r = shift_lerp(r_ref, pr_ref, mu_r)
    k = shift_lerp(k_ref, pk_ref, mu_k)
    v = shift_lerp(v_ref, pv_ref, mu_v)
    g = shift_lerp(g_ref, pg_ref, mu_g)
    lo = shift_lerp(lo_ref, plo_ref, mulo_ref[...])

    lane = lax.broadcasted_iota(jnp.int32, (1, LANES), 1)
    first = lane < A_HEAD
    m1 = first.astype(F32)
    m2 = 1.0 - m1

    w = w0 + _dot(jnp.tanh(lo).astype(BF16), wup_ref[...])
    logw = -_sigmoid(w) * EXP_M_HALF
    a = _sigmoid(a0 + _dot(lo.astype(BF16), aup_ref[...]))
    kk = k * k_k
    k = k * (1.0 + (a - 1.0) * k_a)
    kk = kk * lax.rsqrt(_pair_sum(kk * kk, first) + 1e-12)
    alpha = -kk
    beta = kk * a
    bonus = _pair_sum(r * k * r_k, first) * v

    nc = tb // L
    row = lax.broadcasted_iota(jnp.int32, (L, L), 0)
    col = lax.broadcasted_iota(jnp.int32, (L, L), 1)
    tril_strict = row > col
    row2 = lax.broadcasted_iota(jnp.int32, (L, LANES), 0)
    col2 = lax.broadcasted_iota(jnp.int32, (L, LANES), 1) % L
    tril_incl2 = row2 >= col2
    bd_r = lax.broadcasted_iota(jnp.int32, (LANES, LANES), 0) // A_HEAD
    bd_c = lax.broadcasted_iota(jnp.int32, (LANES, LANES), 1) // A_HEAD
    blockdiag = bd_r == bd_c

    pos = lax.broadcasted_iota(jnp.int32, (tb, 1), 0) % L
    cum = logw
    step = 1
    while step < L:
        cum = cum + jnp.where(pos >= step, pltpu.roll(cum, step, 0), 0.0)
        step *= 2
    e_pos = jnp.exp(cum)
    e_neg = jnp.exp(-cum)
    ab = alpha * jnp.exp(cum - logw)
    rb = r * e_pos
    kt = (k * e_neg).astype(BF16)
    bt = (beta * e_neg).astype(BF16)
    vv = v.astype(BF16)
    heads = ((m1, ab * m1, (rb * m1).astype(BF16)), (m2, ab * m2, (rb * m2).astype(BF16)))
    zero_top = jnp.zeros((L, LANES), BF16)

    sls = [slice(c * L, (c + 1) * L) for c in range(nc)]
    kb = [jnp.concatenate([kt[sl], bt[sl]], axis=0) for sl in sls]
    xa = []
    for c, sl in enumerate(sls):
        x = jnp.concatenate([heads[0][1][sl].astype(BF16), heads[1][1][sl].astype(BF16),
                             heads[0][2][sl], heads[1][2][sl]], axis=0)
        xa.append(_dot_nt(x, kb[c]))
    prob = [(c, h) for c in range(nc) for h in range(2)]
    n_pow, ys, a_r = [], [], []
    for c, h in prob:
        an = xa[c][h * L:(h + 1) * L]
        a_ak = jnp.where(tril_strict, an[:, :L], 0.0).astype(BF16)
        n_pow.append(jnp.where(tril_strict, an[:, L:], 0.0).astype(BF16))
        a_r.append(jnp.where(tril_incl2, xa[c][(2 + h) * L:(3 + h) * L], 0.0).astype(BF16))
        ys.append(_dot(a_ak, vv[sls[c]]) * heads[h][0] + pltpu.roll(heads[h][1][sls[c]], A_HEAD, 1))
    for lvl in range(6):
        ys = [y + _dot(n, y.astype(BF16)) for n, y in zip(n_pow, ys)]
        if lvl < 5:
            n_pow = [_dot(n, n).astype(BF16) for n in n_pow]
    q_c, o0_c, g_c, pb_c = [], [], [], []
    for c, sl in enumerate(sls):
        top = jnp.concatenate([vv[sl], zero_top], axis=1)
        y2 = []
        o0 = jnp.zeros((L, LANES), F32)
        qq = rb[sl]
        for h in range(2):
            mh = heads[h][0]
            y = ys[2 * c + h]
            y = jnp.concatenate([y * mh, pltpu.roll(y * (1.0 - mh), A_HEAD, 1)], axis=1)
            y2.append(y)
            out = _dot(a_r[2 * c + h], jnp.concatenate([top, y.astype(BF16)], axis=0))
            o0 = o0 + out[:, :LANES] * mh
            qq = qq + out[:, LANES:]
        ysum = (y2[0] + y2[1]).astype(BF16)
        gp = _dot_tn(jnp.concatenate([top, ysum], axis=0), kb[c])
        g_c.append(jnp.where(blockdiag, gp[:LANES], 0.0))
        pb_c.append(jnp.where(blockdiag, gp[LANES:], 0.0))
        q_c.append(qq.astype(BF16))
        o0_c.append(o0)

    wl = [e_pos[(c + 1) * L - 1:(c + 1) * L] for c in range(nc)]
    eye_f = (lax.broadcasted_iota(jnp.int32, (LANES, LANES), 0)
             == lax.broadcasted_iota(jnp.int32, (LANES, LANES), 1)).astype(F32)
    pairs = []
    for a in range(0, nc, 2):
        b = a + 1
        pb_b = pb_c[b].astype(BF16)
        e_a = (eye_f + pb_c[a]) * wl[a]
        g_a = g_c[a] * wl[a]
        t = _dot(jnp.concatenate([e_a, g_a], axis=0).astype(BF16), pb_b)
        c_ab = (pb_c[a] * wl[a] + t[:LANES]) * wl[b]
        g_ab = (g_a + t[LANES:] + g_c[b]) * wl[b]
        rhs = jnp.concatenate([c_ab.astype(BF16), pb_c[a].astype(BF16)], axis=1)
        pairs.append((rhs, g_ab, wl[a] * wl[b]))

    state = state_ref[...]
    outs = []
    for pi, (rhs, g_ab, w_ab) in enumerate(pairs):
        a, b = 2 * pi, 2 * pi + 1
        st_b = state.astype(BF16)
        r = _dot(st_b, rhs)
        s_mid = (state + r[:, LANES:] + g_c[a]) * wl[a]
        outs.append(o0_c[a] + _dot_nt(q_c[a], st_b))
        outs.append(o0_c[b] + _dot_nt(q_c[b], s_mid.astype(BF16)))
        state = state * w_ab + r[:, :LANES] + g_ab
    state_ref[...] = state
    o = jnp.concatenate(outs, axis=0)

    mean = _pair_sum(o, first) * (1.0 / A_HEAD)
    d = o - mean
    var = _pair_sum(d * d, first) * (1.0 / A_HEAD)
    y = d * lax.rsqrt(var + A_LN_EPS) * ln_w + ln_b
    o_ref[...] = ((y + bonus) * _sigmoid(g)).astype(o_ref.dtype)


def rwkv7_mix(p, mu, w0, w_up, a0, a_up, k_k, k_a, r_k, ln_w, ln_b, tb=2048):
    s = p.shape[0]
    tb = min(tb, s)
    assert s % tb == 0
    npair = A_H // 2
    nblk = A_W // LANES
    zeros = jnp.zeros((A_W,), F32)
    cp = jnp.stack([mu[0:A_W], mu[A_W:2 * A_W], mu[2 * A_W:3 * A_W], mu[3 * A_W:4 * A_W],
                    w0, a0, k_k, k_a, r_k, ln_w, ln_b, zeros, zeros, zeros, zeros, zeros])
    tail = 2 * LANES - 2 * A_LORA
    mulo = jnp.concatenate([mu[4 * A_W:], jnp.zeros((tail,), F32)]).reshape(1, 2 * LANES)
    wup = jnp.concatenate([w_up, jnp.zeros((2 * LANES - A_LORA, A_W), F32)], axis=0).astype(BF16)
    aup = jnp.concatenate([jnp.zeros((A_LORA, A_W), F32), a_up, jnp.zeros((tail, A_W), F32)], axis=0).astype(BF16)
    lora_blk = EV_LORA // (2 * LANES)

    def col(j):
        return pl.BlockSpec((tb, LANES), lambda hp, i, j=j: (i, j * nblk + hp))

    vmem = 2 * (4 * tb * LANES * 4 + tb * 2 * LANES * 4 + tb * LANES * 2) + (24 << 20)
    return pl.pallas_call(
        functools.partial(_rwkv_kernel, tb=tb),
        grid=(npair, s // tb),
        in_specs=[col(0), col(1), col(2), col(3),
                  pl.BlockSpec((tb, 2 * LANES), lambda hp, i: (i, lora_blk)),
                  pl.BlockSpec((16, LANES), lambda hp, i: (0, hp)),
                  pl.BlockSpec((1, 2 * LANES), lambda hp, i: (0, 0)),
                  pl.BlockSpec((2 * LANES, LANES), lambda hp, i: (0, hp)),
                  pl.BlockSpec((2 * LANES, LANES), lambda hp, i: (0, hp))],
        out_specs=pl.BlockSpec((tb, LANES), lambda hp, i: (i, hp)),
        out_shape=jax.ShapeDtypeStruct((s, MIX_W), BF16),
        scratch_shapes=[pltpu.VMEM((LANES, LANES), F32),
                        pltpu.VMEM((SUBLANES, LANES), F32), pltpu.VMEM((SUBLANES, LANES), F32),
                        pltpu.VMEM((SUBLANES, LANES), F32), pltpu.VMEM((SUBLANES, LANES), F32),
                        pltpu.VMEM((SUBLANES, 2 * LANES), F32)],
        compiler_params=_cparams(("parallel", "arbitrary"), vmem),
        name="rwkv7",
    )(p, p, p, p, p, cp, mulo, wup, aup)


def _mlstm_kernel(ib_ref, fb_ref, q_ref, k_ref, v_ref, og_ref, gt_ref, ng_ref, ybuf_ref, o_ref,
                  c_ref, n_ref, m_ref, *, tb):
    del ybuf_ref
    L = CHUNK
    h = pl.program_id(0)
    i = pl.program_id(1)

    @pl.when(i == 0)
    def _():
        c_ref[...] = jnp.zeros_like(c_ref)
        n_ref[...] = jnp.zeros_like(n_ref)
        m_ref[...] = jnp.zeros_like(m_ref)

    i_b = ib_ref[h]
    f_b = fb_ref[h]
    row = lax.broadcasted_iota(jnp.int32, (L, L), 0)
    col = lax.broadcasted_iota(jnp.int32, (L, L), 1)
    causal = row >= col
    eye = row == col

    def to_col(x_row):
        return jnp.sum(jnp.where(eye, jnp.broadcast_to(x_row, (L, L)), 0.0), axis=1, keepdims=True)

    pre = []
    for c in range(tb // L):
        sl = slice(c * L, (c + 1) * L)
        gi = gt_ref[c, pl.ds(h, 1), :]
        gf = gt_ref[c, pl.ds(B_H + h, 1), :]
        li_row = GATE_CAP * jnp.tanh((gi + i_b) / GATE_CAP)
        lf_row = -_softplus(-(GATE_CAP * jnp.tanh((gf + f_b) / GATE_CAP)))
        li_col = to_col(li_row)
        lf_col = to_col(lf_row)
        cum_col = jnp.sum(jnp.where(causal, jnp.broadcast_to(lf_row, (L, L)), 0.0), axis=1, keepdims=True)
        cum_row = jnp.sum(jnp.where(row <= col, jnp.broadcast_to(lf_col, (L, L)), 0.0), axis=0, keepdims=True)
        total = jnp.sum(lf_row, axis=1, keepdims=True)

        qf = q_ref[sl, :]
        q = qf.astype(BF16)
        kf = k_ref[sl, :] * (B_HEAD ** -0.5)
        vb = v_ref[sl, :].astype(BF16)

        d_log = jnp.where(causal, cum_col - cum_row + li_row, -jnp.inf)
        m_loc = jnp.max(d_log, axis=1, keepdims=True)
        s_loc = _dot_nt(q, kf.astype(BF16)) * jnp.exp(d_log - m_loc)
        sv = _dot(s_loc.astype(BF16), vb)
        s_sum = jnp.sum(s_loc, axis=1, keepdims=True)

        log_w_col = total - cum_col + li_col
        m_k = jnp.max(log_w_col, axis=0, keepdims=True)
        kw = kf * jnp.exp(log_w_col - m_k)
        kv = _dot_tn(kw.astype(BF16), vb)
        k_sum = jnp.sum(kw, axis=0, keepdims=True)
        wide = lambda t: jnp.broadcast_to(t, (t.shape[0], B_HEAD))
        pre.append((qf, q, wide(cum_col), wide(total), wide(m_loc), sv, wide(s_sum), wide(m_k), kv, k_sum))

    c_st = c_ref[...]
    n_st = n_ref[...]
    m_st = m_ref[...]
    outs = []
    for qf, q, cum_b, total_b, m_loc_b, sv, s_sum_b, m_k_b, kv, k_sum in pre:
        m_inter = cum_b + m_st
        m_row = jnp.maximum(m_loc_b, m_inter)
        w_loc = jnp.exp(m_loc_b - m_row)
        w_inter = jnp.exp(m_inter - m_row)
        num = w_loc * sv + w_inter * _dot(q, c_st.astype(BF16))
        den = w_loc * s_sum_b + w_inter * jnp.sum(qf * n_st, axis=1, keepdims=True)
        outs.append(num / jnp.maximum(jnp.abs(den), jnp.exp(-m_row)))

        m_new = jnp.maximum(total_b + m_st, m_k_b)
        carry = jnp.exp(total_b + m_st - m_new)
        w_new = jnp.exp(m_k_b - m_new)
        c_st = carry * c_st + w_new * kv
        n_st = carry * n_st + w_new * k_sum
        m_st = m_new
    c_ref[...] = c_st
    n_ref[...] = n_st
    m_ref[...] = m_st

    hh = jnp.concatenate(outs, axis=0) if len(outs) > 1 else outs[0]
    ms = jnp.mean(hh * hh, axis=1, keepdims=True)
    y = hh * lax.rsqrt(ms + NORM_EPS) * ng_ref[...]
    o_ref[...] = (y * _sigmoid(og_ref[...])).astype(o_ref.dtype)


def mlstm_mix(p, gates_t, i_b, f_b, norm_g, y_buf, tb=2048):
    s = p.shape[0]
    tb = min(tb, s)
    assert s % tb == 0
    nblk = B_W // B_HEAD
    out0 = A_W // B_HEAD
    base = EV_A_COLS // B_HEAD

    def col(j):
        return pl.BlockSpec((tb, B_HEAD), lambda h, i, j=j: (i, base + j * nblk + h))

    smem = pl.BlockSpec(memory_space=pltpu.SMEM)
    vmem = 2 * (4 * tb * B_HEAD * 4 + tb * B_HEAD * 2) + (24 << 20)
    return pl.pallas_call(
        functools.partial(_mlstm_kernel, tb=tb),
        grid=(B_H, s // tb),
        in_specs=[smem, smem, col(0), col(1), col(2), col(3),
                  pl.BlockSpec((tb // CHUNK, 2 * B_H, CHUNK), lambda h, i: (i, 0, 0)),
                  pl.BlockSpec((1, B_HEAD), lambda h, i: (0, h)),
                  pl.BlockSpec(memory_space=pl.ANY)],
        out_specs=pl.BlockSpec((tb, B_HEAD), lambda h, i: (i, out0 + h)),
        out_shape=jax.ShapeDtypeStruct(y_buf.shape, BF16),
        input_output_aliases={8: 0},
        scratch_shapes=[pltpu.VMEM((B_HEAD, B_HEAD), F32), pltpu.VMEM((1, B_HEAD), F32),
                        pltpu.VMEM((1, B_HEAD), F32)],
        compiler_params=_cparams(("parallel", "arbitrary"), vmem),
        name="mlstm",
    )(i_b, f_b, p, p, p, p, gates_t, norm_g.reshape(1, B_W), y_buf)


def _retention_kernel(lg_ref, q_ref, k_ref, v_ref, g_ref, cc_ref, ss_ref, ng_ref, o_ref, r_ref, *, tb):
    L = CHUNK
    h = pl.program_id(0)
    i = pl.program_id(1)

    @pl.when(i == 0)
    def _():
        r_ref[...] = jnp.zeros_like(r_ref)

    lg = lg_ref[h]
    row = lax.broadcasted_iota(jnp.int32, (L, L), 0)
    col = lax.broadcasted_iota(jnp.int32, (L, L), 1)
    intra = jnp.exp(lg * jnp.abs(row - col).astype(F32))
    pos = lax.broadcasted_iota(jnp.int32, (L, 1), 0).astype(F32)
    q_decay = jnp.exp(lg * (pos + 1.0))
    k_decay = jnp.exp(lg * (L - 1.0 - pos))
    chunk_decay = jnp.exp(jnp.full((1, C_VHEAD), lg * L, F32))

    cc = cc_ref[...]
    ss = ss_ref[...]

    def rope(t):
        return t * cc + pltpu.roll(t, C_KHEAD // 2, 1) * ss

    q = rope(q_ref[...]).astype(BF16)
    k = rope(k_ref[...]) * (C_KHEAD ** -0.5)
    vb = v_ref[...].astype(BF16)
    sls = [slice(c * L, (c + 1) * L) for c in range(tb // L)]
    sv = [_dot((_dot_nt(q[sl], k[sl].astype(BF16)) * intra).astype(BF16), vb[sl]) for sl in sls]
    kv = [_dot_tn((k[sl] * k_decay).astype(BF16), vb[sl]) for sl in sls]
    r_st = r_ref[...]
    outs = []
    for c, sl in enumerate(sls):
        outs.append(sv[c] + q_decay * _dot(q[sl], r_st.astype(BF16)))
        r_st = chunk_decay * r_st + kv[c]
    r_ref[...] = r_st
    o = jnp.concatenate(outs, axis=0) if len(outs) > 1 else outs[0]
    ms = jnp.mean(o * o, axis=1, keepdims=True)
    y = o * lax.rsqrt(ms + NORM_EPS) * ng_ref[...]
    g = g_ref[...]
    o_ref[...] = (y * (g * _sigmoid(g))).astype(o_ref.dtype)


def retention_mix(p, cc, ss, log_gamma, norm_g, tb=2048):
    s = p.shape[0]
    tb = min(tb, s)
    assert s % tb == 0
    kb = C_KW // C_KHEAD
    vb = (2 * C_KW) // C_VHEAD
    gb = (2 * C_KW + C_VW) // C_VHEAD
    smem = pl.BlockSpec(memory_space=pltpu.SMEM)
    vmem = 2 * (2 * tb * C_KHEAD * 4 + 2 * tb * C_VHEAD * 4 + 2 * tb * C_KHEAD * 4 + tb * C_VHEAD * 2) + (24 << 20)
    return pl.pallas_call(
        functools.partial(_retention_kernel, tb=tb),
        grid=(C_H, s // tb),
        in_specs=[smem,
                  pl.BlockSpec((tb, C_KHEAD), lambda h, i: (i, h)),
                  pl.BlockSpec((tb, C_KHEAD), lambda h, i: (i, kb + h)),
                  pl.BlockSpec((tb, C_VHEAD), lambda h, i: (i, vb + h)),
                  pl.BlockSpec((tb, C_VHEAD), lambda h, i: (i, gb + h)),
                  pl.BlockSpec((tb, C_KHEAD), lambda h, i: (i, 0)),
                  pl.BlockSpec((tb, C_KHEAD), lambda h, i: (i, 0)),
                  pl.BlockSpec((1, C_VHEAD), lambda h, i: (0, h))],
        out_specs=pl.BlockSpec((tb, C_VHEAD), lambda h, i: (i, h)),
        out_shape=jax.ShapeDtypeStruct((s, MIX_W), BF16),
        scratch_shapes=[pltpu.VMEM((C_KHEAD, C_VHEAD), F32)],
        compiler_params=_cparams(("parallel", "arbitrary"), vmem),
        name="retention",
    )(log_gamma, p, p, p, p, cc, ss, norm_g.reshape(1, C_VW))


def _rglru_kernel(x_ref, gate_ref, cw_ref, cp_ref, wr_ref, wi_ref, ybuf_ref, o_ref, px_ref, h_ref, *, tb, ts):
    del ybuf_ref
    i = pl.program_id(1)

    @pl.when(i == 0)
    def _():
        px_ref[...] = jnp.zeros_like(px_ref)
        h_ref[...] = jnp.zeros_like(h_ref)

    cw = cw_ref[...]
    cp = cp_ref[...]
    conv_b, b_r, b_i, lam = cp[0:1], cp[1:2], cp[2:3], cp[3:4]
    x = x_ref[...]
    ext = jnp.concatenate([px_ref[...], x], axis=0)
    xc = conv_b + x * cw[CONV_W - 1:CONV_W]
    for d in range(1, CONV_W):
        xc = xc + pltpu.roll(ext, d, 0)[SUBLANES:] * cw[CONV_W - 1 - d:CONV_W - d]
    px_ref[...] = x[tb - SUBLANES:tb]

    xcb = xc.astype(BF16)
    r = _sigmoid(_dot(xcb, wr_ref[0]) + b_r)
    ig = _sigmoid(_dot(xcb, wi_ref[0]) + b_i)
    log_a = -LRU_C * r * _softplus(-lam)
    a = jnp.exp(log_a)
    u = jnp.sqrt(-jnp.tanh(log_a) * (a * a + 1.0)) * (ig * xc)

    rows = lax.broadcasted_iota(jnp.int32, (tb, 1), 0) % ts
    d = 1
    while d < ts:
        keep = rows >= d
        a_sh = jnp.where(keep, pltpu.roll(a, d, 0), 1.0)
        u_sh = jnp.where(keep, pltpu.roll(u, d, 0), 0.0)
        u = a * u_sh + u
        a = a * a_sh
        d *= 2
    h_prev = h_ref[0:1, :]
    parts = []
    for c in range(tb // ts):
        h_c = a[c * ts:(c + 1) * ts] * h_prev + u[c * ts:(c + 1) * ts]
        h_prev = h_c[ts - 1:ts]
        parts.append(h_c)
    hh = jnp.concatenate(parts, axis=0) if len(parts) > 1 else parts[0]
    h_ref[...] = jnp.broadcast_to(h_prev, h_ref.shape)

    gt = gate_ref[...]
    gelu = 0.5 * gt * (1.0 + jnp.tanh(0.7978845608028654 * (gt + 0.044715 * (gt * gt * gt))))
    o_ref[...] = (hh * gelu).astype(o_ref.dtype)


def rglru_mix(p, conv_w, conv_b, w_r, b_r, w_i, b_i, lam, y_buf, tb=2048, ts=256):
    s = p.shape[0]
    tb = min(tb, s)
    ts = min(ts, tb)
    assert s % tb == 0 and tb % ts == 0
    xb0 = (2 * C_KW + 2 * C_VW) // D_BW
    gt0 = xb0 + D_W // D_BW
    out0 = C_VW // D_BW
    zeros = jnp.zeros((D_W,), F32)
    cp = jnp.stack([conv_b, b_r, b_i, lam, zeros, zeros, zeros, zeros])
    vmem = 2 * (2 * tb * D_BW * 4 + tb * D_BW * 2 + 2 * D_BW * D_BW * 2) + (24 << 20)
    return pl.pallas_call(
        functools.partial(_rglru_kernel, tb=tb, ts=ts),
        grid=(D_BLOCKS, s // tb),
        in_specs=[pl.BlockSpec((tb, D_BW), lambda n, i: (i, xb0 + n)),
                  pl.BlockSpec((tb, D_BW), lambda n, i: (i, gt0 + n)),
                  pl.BlockSpec((CONV_W, D_BW), lambda n, i: (0, n)),
                  pl.BlockSpec((SUBLANES, D_BW), lambda n, i: (0, n)),
                  pl.BlockSpec((1, D_BW, D_BW), lambda n, i: (n, 0, 0)),
                  pl.BlockSpec((1, D_BW, D_BW), lambda n, i: (n, 0, 0)),
                  pl.BlockSpec(memory_space=pl.ANY)],
        out_specs=pl.BlockSpec((tb, D_BW), lambda n, i: (i, out0 + n)),
        out_shape=jax.ShapeDtypeStruct(y_buf.shape, BF16),
        input_output_aliases={6: 0},
        scratch_shapes=[pltpu.VMEM((SUBLANES, D_BW), F32), pltpu.VMEM((SUBLANES, D_BW), F32)],
        compiler_params=_cparams(("parallel", "arbitrary"), vmem),
        name="rglru",
    )(p, p, conv_w, cp, w_r.astype(BF16), w_i.astype(BF16), y_buf)


def _xattn_block_kernel(h_ref, g_ref, g2_ref, wq_ref, k_ref, v_ref, wo_ref, o_ref, xn2_ref, att_ref):
    x = h_ref[...]
    ms = jnp.mean(x * x, axis=-1, keepdims=True)
    xn = (x * lax.rsqrt(ms + NORM_EPS) * g_ref[...]).astype(BF16)
    q = _dot(xn, wq_ref[...]).astype(BF16)
    for hd in range(XA_H):
        sl = slice(hd * XA_HEAD, (hd + 1) * XA_HEAD)
        sc = _dot_nt(q[:, sl], k_ref[:, sl]) * (XA_HEAD ** -0.5)
        sc = sc - jnp.max(sc, axis=1, keepdims=True)
        e = jnp.exp(sc)
        pr = e / jnp.sum(e, axis=1, keepdims=True)
        att_ref[:, sl] = _dot(pr.astype(BF16), v_ref[:, sl]).astype(BF16)
    h_new = x + _dot(att_ref[...], wo_ref[...])
    o_ref[...] = h_new
    ms2 = jnp.mean(h_new * h_new, axis=-1, keepdims=True)
    xn2_ref[...] = (h_new * lax.rsqrt(ms2 + NORM_EPS) * g2_ref[...]).astype(xn2_ref.dtype)


def xattn_block(h, g, g_next, wq, kmem, vmem_, wo, tm=256):
    s, d = h.shape
    nm = kmem.shape[0]
    tm = min(tm, s)
    const = dict(pipeline_mode=pl.Buffered(1))
    vmem = (4 * tm * d * 4 + 2 * tm * d * 2 + 2 * d * XA_W * 2 + 2 * nm * XA_W * 2
            + tm * d * (2 + 4 + 4) + tm * XA_W * (4 + 2 + 2) + (8 << 20))
    row = pl.BlockSpec((tm, d), lambda i: (i, 0))
    gain = pl.BlockSpec((1, d), lambda i: (0, 0))
    return pl.pallas_call(
        _xattn_block_kernel,
        grid=(s // tm,),
        in_specs=[row, gain, gain,
                  pl.BlockSpec((d, XA_W), lambda i: (0, 0), **const),
                  pl.BlockSpec((nm, XA_W), lambda i: (0, 0), **const),
                  pl.BlockSpec((nm, XA_W), lambda i: (0, 0), **const),
                  pl.BlockSpec((XA_W, d), lambda i: (0, 0), **const)],
        out_specs=[row, row],
        out_shape=[jax.ShapeDtypeStruct((s, d), F32), jax.ShapeDtypeStruct((s, d), BF16)],
        scratch_shapes=[pltpu.VMEM((tm, XA_W), BF16)],
        compiler_params=_cparams(("parallel",), vmem),
        name="xattn_block",
    )(h, g.reshape(1, d), g_next.reshape(1, d), wq, kmem, vmem_, wo)


def kernel(x, mem, mem_norm_g, norm_mix_g, norm_xattn_g, norm_mlp_g, xattn_wq, xattn_wk, xattn_wv, xattn_wo, mlp_w1, mlp_w2, even_w_in, even_w_out, rwkv_mu, rwkv_w0, rwkv_w_up, rwkv_a0, rwkv_a_up, rwkv_k_k, rwkv_k_a, rwkv_r_k, rwkv_ln_w, rwkv_ln_b, mlstm_i_b, mlstm_f_b, mlstm_norm_g, odd_w_in, odd_w_out, ret_norm_g, lru_conv_w, lru_conv_b, lru_w_r, lru_b_r, lru_w_i, lru_b_i, lru_lambda, final_norm_g):
    b, s, d = x.shape
    assert b == 1 and d == D_MODEL and s % 256 == 0
    h = x.reshape(s, d)
    mem_n = rmsnorm(mem.reshape(-1, d), mem_norm_g, BF16)
    wq_bf = xattn_wq.astype(BF16)
    wo_bf = xattn_wo.astype(BF16)
    even_wt = jnp.swapaxes(even_w_in, 1, 2)

    positions = jnp.arange(s, dtype=F32)
    inv_freq = ROPE_BASE ** (-jnp.arange(0, C_KHEAD, 2, dtype=F32) / C_KHEAD)
    ang = positions[:, None] * inv_freq[None, :]
    cos, sin = jnp.cos(ang), jnp.sin(ang)
    rope_cc = jnp.concatenate([cos, cos], axis=1)
    rope_ss = jnp.concatenate([-sin, sin], axis=1)
    log_gamma = jnp.log1p(-jnp.exp2(-5.0 - jnp.arange(C_H, dtype=F32)))

    for layer in range(DEPTH):
        j = layer // 2
        xn = rmsnorm(h, norm_mix_g[layer], BF16)
        if layer % 2 == 0:
            p = matmul(xn, even_wt, j, F32, n=EV_COLS, w_transposed=True, tn=EV_TN,
                       segments=((0, 0), (EV_A_COLS // EV_TN, A_COLS), (EV_GATE_TILE, EVEN_COLS - EV_TN)))
            y = rwkv7_mix(p, rwkv_mu[j], rwkv_w0[j], rwkv_w_up[j], rwkv_a0[j], rwkv_a_up[j],
                          rwkv_k_k[j], rwkv_k_a[j], rwkv_r_k[j], rwkv_ln_w[j], rwkv_ln_b[j])
            gates_t = p[:, EV_COLS - 2 * B_H:].reshape(s // CHUNK, CHUNK, 2 * B_H).transpose(0, 2, 1)
            y = mlstm_mix(p, gates_t, mlstm_i_b[j], mlstm_f_b[j], mlstm_norm_g[j], y)
            w_out = even_w_out
        else:
            p = matmul(xn, odd_w_in, j, F32)
            y = retention_mix(p, rope_cc, rope_ss, log_gamma, ret_norm_g[j])
            y = rglru_mix(p, lru_conv_w[j], lru_conv_b[j], lru_w_r[j], lru_b_r[j],
                          lru_w_i[j], lru_b_i[j], lru_lambda[j], y)
            w_out = odd_w_out
        h = matmul(y, w_out, j, F32, epilogue="residual", residual=h)

        kmem = matmul(mem_n, xattn_wk, layer, BF16)
        vmem_ = matmul(mem_n, xattn_wv, layer, BF16)
        h, xn = xattn_block(h, norm_xattn_g[layer], norm_mlp_g[layer], wq_bf[layer], kmem, vmem_, wo_bf[layer])
        u = matmul(xn, mlp_w1, layer, BF16, epilogue="relu2")
        h = matmul(u, mlp_w2, layer, F32, epilogue="residual", residual=h)

    return rmsnorm(h, final_norm_g, F32).reshape(b, s, d)
```
